```python
import math
import jax, jax.numpy as jnp
from jax import lax
import numpy as np

D_MODEL = 2048
BATCH = 1
SEQ = 16384
DEPTH = 2

GRID_W = 64
CTX_LEN = 256
Q_BLOCK = 128
ROPE_THETA = 10000.0
LN_EPS = 1e-5
RMS_EPS = 1e-6

MLA_HEADS = 8
MLA_Q_LORA = 512
MLA_KV_LORA = 512
MLA_NOPE = 128
MLA_ROPE = 64
MLA_V = 128
MLA_WIDTH = MLA_HEADS * MLA_V

GQA_HEADS = 8
GQA_KV_HEADS = 2
GQA_HEAD_DIM = 128
GQA_WIDTH = GQA_HEADS * GQA_HEAD_DIM

DIFF_HEADS = 4
DIFF_HEAD_DIM = 128
DIFF_V_DIM = 2 * DIFF_HEAD_DIM
DIFF_WIDTH = DIFF_HEADS * DIFF_V_DIM

N_BRANCHES = 3
IN_SPLITS = (
    MLA_Q_LORA, MLA_KV_LORA, MLA_ROPE,
    GQA_HEADS * GQA_HEAD_DIM, GQA_KV_HEADS * GQA_HEAD_DIM,
    GQA_KV_HEADS * GQA_HEAD_DIM,
    DIFF_HEADS * 2 * DIFF_HEAD_DIM, DIFF_HEADS * 2 * DIFF_HEAD_DIM,
    DIFF_HEADS * DIFF_V_DIM,
    N_BRANCHES * D_MODEL,
)
IN_COLS = sum(IN_SPLITS)

N_EXPERTS = 16
N_GROUPS = 4
EXPERTS_PER_GROUP = N_EXPERTS // N_GROUPS
TOPK_GROUPS = 1
TOP_K = 2
MOE_D_FF = 512
ROUTED_SCALE = 1.0

DEEPNORM_ALPHA = (2 * DEPTH) ** 0.25
DEEPNORM_BETA = (8 * DEPTH) ** -0.25

kernel_name = "hybrid_gated_mla_gqa_diffattn_grouped_moe_dit"


def layer_norm(x, g, b):
    xf = x.astype(jnp.float32)
    mu = jnp.mean(xf, axis=-1, keepdims=True)
    var = jnp.mean(jnp.square(xf - mu), axis=-1, keepdims=True)
    return ((xf - mu) * lax.rsqrt(var + LN_EPS) * g + b).astype(x.dtype)


def rms_norm(x, g):
    xf = x.astype(jnp.float32)
    y = xf * lax.rsqrt(jnp.mean(jnp.square(xf), axis=-1, keepdims=True) + RMS_EPS)
    return (y * g).astype(x.dtype)


def axial_rope_tables(row, col, dim):
    quarter = dim // 4
    inv_freq = ROPE_THETA ** (-jnp.arange(quarter, dtype=jnp.float32) / quarter)
    ang = jnp.stack([row[:, None] * inv_freq, col[:, None] * inv_freq], axis=1)
    return jnp.cos(ang), jnp.sin(ang)


def apply_rope(x, cos, sin):
    dim = x.shape[-1]
    xr = x.astype(jnp.float32).reshape(*x.shape[:-1], 2, 2, dim // 4)
    x1, x2 = xr[..., 0, :], xr[..., 1, :]
    c, s = cos[:, None], sin[:, None]
    out = jnp.stack([x1 * c - x2 * s, x2 * c + x1 * s], axis=-2)
    return out.reshape(x.shape).astype(x.dtype)


def _blocked(q):
    B, Sq, H, d = q.shape
    return q.reshape(B, Sq // Q_BLOCK, Q_BLOCK, H, d).swapaxes(0, 1)


def _unblocked(o):
    nb, B, Qb, H, d = o.shape
    return o.swapaxes(0, 1).reshape(B, nb * Qb, H, d)


def softmax_attention(q, k, v):
    B, _, H, dk = q.shape
    Hk = k.shape[2]
    G = H // Hk
    dv = v.shape[-1]
    scale = dk ** -0.5

    def block(qb):
        qb = qb.reshape(B, Q_BLOCK, Hk, G, dk)
        s = jnp.einsum('bqkgd,bskd->bkgqs', qb, k).astype(jnp.float32) * scale
        p = jax.nn.softmax(s, axis=-1).astype(v.dtype)
        o = jnp.einsum('bkgqs,bskv->bqkgv', p, v)
        return o.reshape(B, Q_BLOCK, H, dv)

    return _unblocked(lax.map(block, _blocked(q)))


def differential_attention(q1, q2, k1, k2, v, lam):
    scale = q1.shape[-1] ** -0.5

    def block(qs):
        a, b = qs
        s1 = jnp.einsum('bqhd,bshd->bhqs', a, k1).astype(jnp.float32) * scale
        s2 = jnp.einsum('bqhd,bshd->bhqs', b, k2).astype(jnp.float32) * scale
        p = jax.nn.softmax(s1, axis=-1) - lam * jax.nn.softmax(s2, axis=-1)
        return jnp.einsum('bhqs,bshv->bqhv', p.astype(v.dtype), v)

    return _unblocked(lax.map(block, (_blocked(q1), _blocked(q2))))


def project_stream(h, w_in, mla_q_norm, mla_w_uq, mla_kv_norm, mla_w_ukv, gqa_q_norm, gqa_k_norm, rope):
    B, S, _ = h.shape
    offs = np.cumsum(IN_SPLITS)[:-1].tolist()
    cq, ckv, kr, gq, gk, gv, dq, dk, dv, gates = jnp.split(h @ w_in, offs, axis=-1)
    q = (rms_norm(cq, mla_q_norm) @ mla_w_uq).reshape(B, S, MLA_HEADS, MLA_NOPE + MLA_ROPE)
    kv = (rms_norm(ckv, mla_kv_norm) @ mla_w_ukv).reshape(B, S, MLA_HEADS, MLA_NOPE + MLA_V)
    q_nope, q_rope = q[..., :MLA_NOPE], q[..., MLA_NOPE:]
    k_nope, v_mla = kv[..., :MLA_NOPE], kv[..., MLA_NOPE:]
    k_rope = kr.reshape(B, S, 1, MLA_ROPE)
    q_gqa = rms_norm(gq.reshape(B, S, GQA_HEADS, GQA_HEAD_DIM), gqa_q_norm)
    k_gqa = rms_norm(gk.reshape(B, S, GQA_KV_HEADS, GQA_HEAD_DIM), gqa_k_norm)
    v_gqa = gv.reshape(B, S, GQA_KV_HEADS, GQA_HEAD_DIM)
    q_diff = dq.reshape(B, S, 2 * DIFF_HEADS, DIFF_HEAD_DIM)
    k_diff = dk.reshape(B, S, 2 * DIFF_HEADS, DIFF_HEAD_DIM)
    v_diff = dv.reshape(B, S, DIFF_HEADS, DIFF_V_DIM)
    if rope is not None:
        (cos_r, sin_r), (cos_h, sin_h) = rope
        q_rope = apply_rope(q_rope, cos_r, sin_r)
        k_rope = apply_rope(k_rope, cos_r, sin_r)
        q_gqa = apply_rope(q_gqa, cos_h, sin_h)
        k_gqa = apply_rope(k_gqa, cos_h, sin_h)
        q_diff = apply_rope(q_diff, cos_h, sin_h)
        k_diff = apply_rope(k_diff, cos_h, sin_h)
    q_mla = jnp.concatenate([q_nope, q_rope], axis=-1)
    k_mla = jnp.concatenate([k_nope, jnp.broadcast_to(k_rope, (B, S, MLA_HEADS, MLA_ROPE))], axis=-1)
    q_diff = q_diff.reshape(B, S, DIFF_HEADS, 2, DIFF_HEAD_DIM)
    k_diff = k_diff.reshape(B, S, DIFF_HEADS, 2, DIFF_HEAD_DIM)
    queries = (q_mla, q_gqa, q_diff[..., 0, :], q_diff[..., 1, :], gates)
    keys = (k_mla, v_mla, k_gqa, v_gqa, k_diff[..., 0, :], k_diff[..., 1, :], v_diff)
    return queries, keys


def mix_and_merge(queries, keys, lam, lam_init, diff_norm, w_br_mla, w_br_gqa, w_br_diff, w_o):
    q_mla, q_gqa, q1, q2, gates = queries
    k_mla, v_mla, k_gqa, v_gqa, k1, k2, v_diff = keys
    B, S = q_mla.shape[:2]
    D = gates.shape[-1] // N_BRANCHES
    o_mla = softmax_attention(q_mla, k_mla, v_mla).reshape(B, S, MLA_WIDTH)
    o_gqa = softmax_attention(q_gqa, k_gqa, v_gqa).reshape(B, S, GQA_WIDTH)
    o_diff = differential_attention(q1, q2, k1, k2, v_diff, lam)
    o_diff = (rms_norm(o_diff, diff_norm) * (1.0 - lam_init)).reshape(B, S, DIFF_WIDTH)
    g = jax.nn.sigmoid(gates.astype(jnp.float32)).astype(gates.dtype).reshape(B, S, N_BRANCHES, D)
    merged = (g[..., 0, :] * (o_mla @ w_br_mla)
              + g[..., 1, :] * (o_gqa @ w_br_gqa)
              + g[..., 2, :] * (o_diff @ w_br_diff))
    return merged @ w_o


def route(t, w_router, b_router):
    T = t.shape[0]
    scores = jax.nn.sigmoid((t @ w_router).astype(jnp.float32))
    biased = scores + b_router.astype(jnp.float32)
    group_score = lax.top_k(biased.reshape(T, N_GROUPS, EXPERTS_PER_GROUP), 2)[0].sum(-1)
    _, top_group = lax.top_k(group_score, TOPK_GROUPS)
    group_mask = jax.nn.one_hot(top_group, N_GROUPS, dtype=jnp.float32).sum(-2) > 0
    expert_mask = jnp.repeat(group_mask, EXPERTS_PER_GROUP, axis=-1)
    _, top_idx = lax.top_k(jnp.where(expert_mask, biased, -jnp.inf), TOP_K)
    w = jnp.take_along_axis(scores, top_idx, axis=-1)
    w = ROUTED_SCALE * w / jnp.sum(w, axis=-1, keepdims=True)
    return jnp.sum(jax.nn.one_hot(top_idx, N_EXPERTS, dtype=jnp.float32) * w[..., None], axis=-2)


def swiglu(t, wg, wu, wd):
    return (jax.nn.silu(t @ wg) * (t @ wu)) @ wd


def moe_ffn(h, w_router, b_router, w_gate, w_up, w_down, sw_gate, sw_up, sw_down):
    B, S, D = h.shape
    t = h.reshape(B * S, D)
    combine = route(t, w_router, b_router).astype(h.dtype)
    out = swiglu(t, sw_gate, sw_up, sw_down)
    for e in range(N_EXPERTS):
        out = out + combine[:, e:e + 1] * swiglu(t, w_gate[e], w_up[e], w_down[e])
    return out.reshape(B, S, D)


def setup_inputs(seed: int = 0) -> dict:
    key = jax.random.key(seed)
    ks = jax.random.split(key, 32)
    L, D, E, F = DEPTH, D_MODEL, N_EXPERTS, MOE_D_FF

    def nrm(k, shape, scale):
        return jax.random.normal(k, shape, jnp.float32) * scale

    def gain(k, shape):
        return 1.0 + 0.02 * jax.random.normal(k, shape, jnp.float32)

    return {
        "x": nrm(ks[0], (BATCH, SEQ, D), 1.0),
        "c": nrm(ks[1], (BATCH, D), 1.0),
        "ctx": nrm(ks[2], (BATCH, CTX_LEN, D), 1.0),
        "c_ctx": nrm(ks[3], (D,), 1.0),
        "w_ada": nrm(ks[4], (L, D, 6 * D), 0.5 * D ** -0.5),
        "b_ada": nrm(ks[5], (L, 6 * D), 0.02),
        "w_in": nrm(ks[6], (L, D, IN_COLS), D ** -0.5),
        "mla_q_norm": gain(ks[7], (L, MLA_Q_LORA)),
        "mla_w_uq": nrm(ks[8], (L, MLA_Q_LORA, MLA_HEADS * (MLA_NOPE + MLA_ROPE)), MLA_Q_LORA ** -0.5),
        "mla_kv_norm": gain(ks[9], (L, MLA_KV_LORA)),
        "mla_w_ukv": nrm(ks[10], (L, MLA_KV_LORA, MLA_HEADS * (MLA_NOPE + MLA_V)), MLA_KV_LORA ** -0.5),
        "gqa_q_norm": gain(ks[11], (L, GQA_HEAD_DIM)),
        "gqa_k_norm": gain(ks[12], (L, GQA_HEAD_DIM)),
        "diff_lambda": nrm(ks[13], (L, 4, DIFF_HEAD_DIM), 0.1),
        "diff_norm": gain(ks[14], (L, DIFF_V_DIM)),
        "w_br_mla": nrm(ks[15], (L, MLA_WIDTH, D), MLA_WIDTH ** -0.5 * DEEPNORM_BETA),
        "w_br_gqa": nrm(ks[16], (L, GQA_WIDTH, D), GQA_WIDTH ** -0.5 * DEEPNORM_BETA),
        "w_br_diff": nrm(ks[17], (L, DIFF_WIDTH, D), DIFF_WIDTH ** -0.5 * DEEPNORM_BETA),
        "w_o": nrm(ks[18], (L, D, D), D ** -0.5 * DEEPNORM_BETA),
        "ln1_g": gain(ks[19], (L, D)),
        "ln1_b": nrm(ks[20], (L, D), 0.02),
        "w_router": nrm(ks[21], (D, E), D ** -0.5),
        "b_router": nrm(ks[22], (E,), 0.01),
        "moe_w_gate": nrm(ks[23], (L, E, D, F), D ** -0.5),
        "moe_w_up": nrm(ks[24], (L, E, D, F), D ** -0.5),
        "moe_w_down": nrm(ks[25], (L, E, F, D), F ** -0.5 * DEEPNORM_BETA),
        "shared_w_gate": nrm(ks[26], (L, D, F), D ** -0.5),
        "shared_w_up": nrm(ks[27], (L, D, F), D ** -0.5),
        "shared_w_down": nrm(ks[28], (L, F, D), F ** -0.5 * DEEPNORM_BETA),
        "ln2_g": gain(ks[29], (L, D)),
        "ln2_b": nrm(ks[30], (L, D), 0.02),
    }


def reference(x, c, ctx, c_ctx, w_ada, b_ada, w_in, mla_q_norm, mla_w_uq, mla_kv_norm, mla_w_ukv,
              gqa_q_norm, gqa_k_norm, diff_lambda, diff_norm, w_br_mla, w_br_gqa, w_br_diff, w_o,
              ln1_g, ln1_b, w_router, b_router, moe_w_gate, moe_w_up, moe_w_down,
              shared_w_gate, shared_w_up, shared_w_down, ln2_g, ln2_b):
    B, S, D = x.shape
    n_ctx = ctx.shape[1]
    rows = S // GRID_W
    row = jnp.repeat(jnp.arange(rows, dtype=jnp.float32), GRID_W)
    col = jnp.tile(jnp.arange(GRID_W, dtype=jnp.float32), rows)
    ropes = (axial_rope_tables(row, col, MLA_ROPE), axial_rope_tables(row, col, GQA_HEAD_DIM))
    alpha = DEEPNORM_ALPHA

    for l in range(DEPTH):
        last = l == DEPTH - 1
        mod_lat = jax.nn.silu(c) @ w_ada[l] + b_ada[l]
        mod_ctx = jax.nn.silu(c_ctx) @ w_ada[l] + b_ada[l]
        sh1, sc1, g1, sh2, sc2, g2 = jnp.split(mod_lat[:, None, :], 6, axis=-1)
        csh1, csc1, cg1, csh2, csc2, cg2 = jnp.split(mod_ctx, 6, axis=-1)

        h_lat = x * (1.0 + sc1) + sh1
        h_ctx = ctx * (1.0 + csc1) + csh1
        proj_args = (w_in[l], mla_q_norm[l], mla_w_uq[l], mla_kv_norm[l], mla_w_ukv[l],
                     gqa_q_norm[l], gqa_k_norm[l])
        q_ctx, k_ctx = project_stream(h_ctx, *proj_args, None)
        q_lat, k_lat = project_stream(h_lat, *proj_args, ropes)
        k_all = tuple(jnp.concatenate([kc, kl], axis=1) for kc, kl in zip(k_ctx, k_lat))

        lam_init = 0.8 - 0.6 * math.exp(-0.3 * l)
        lq1, lk1, lq2, lk2 = diff_lambda[l].astype(jnp.float32)
        lam = jnp.exp(jnp.sum(lq1 * lk1)) - jnp.exp(jnp.sum(lq2 * lk2)) + lam_init
        merge_args = (lam, lam_init, diff_norm[l], w_br_mla[l], w_br_gqa[l], w_br_diff[l], w_o[l])

        y_lat = mix_and_merge(q_lat, k_all, *merge_args)
        x_new = layer_norm(alpha * x + g1 * y_lat, ln1_g[l], ln1_b[l])
        if not last:
            y_ctx = mix_and_merge(q_ctx, k_ctx, *merge_args)
            ctx = layer_norm(alpha * ctx + cg1 * y_ctx, ln1_g[l], ln1_b[l])
        x = x_new

        moe_args = (w_router, b_router, moe_w_gate[l], moe_w_up[l], moe_w_down[l],
                    shared_w_gate[l], shared_w_up[l], shared_w_down[l])
        h2_lat = x * (1.0 + sc2) + sh2
        if not last:
            h2_ctx = ctx * (1.0 + csc2) + csh2
            out = moe_ffn(jnp.concatenate([h2_ctx, h2_lat], axis=1), *moe_args)
            y_ctx, y_lat = out[:, :n_ctx], out[:, n_ctx:]
            ctx = layer_norm(alpha * ctx + cg2 * y_ctx, ln2_g[l], ln2_b[l])
        else:
            y_lat = moe_ffn(h2_lat, *moe_args)
        x = layer_norm(alpha * x + g2 * y_lat, ln2_g[l], ln2_b[l])
    return x
```

```python
import functools
import math

import jax
import jax.numpy as jnp
from jax import lax
from jax.experimental import pallas as pl
from jax.experimental.pallas import tpu as pltpu

F32 = jnp.float32
BF16 = jnp.bfloat16
HIGHEST = lax.Precision.HIGHEST

D_MODEL = 2048
DEPTH = 2
GRID_W = 64
ROPE_THETA = 10000.0
LN_EPS = 1e-5
RMS_EPS = 1e-6
MLA_HEADS = 8
MLA_Q_LORA = 512
MLA_KV_LORA = 512
MLA_NOPE = 128
MLA_ROPE = 64
MLA_V = 128
GQA_HEADS = 8
GQA_KV_HEADS = 2
GQA_HEAD_DIM = 128
DIFF_HEADS = 4
DIFF_HEAD_DIM = 128
DIFF_V_DIM = 2 * DIFF_HEAD_DIM
N_EXPERTS = 16
N_GROUPS = 4
EXPERTS_PER_GROUP = N_EXPERTS // N_GROUPS
MOE_D_FF = 512
ROUTED_SCALE = 1.0
DEEPNORM_ALPHA = (2 * DEPTH) ** 0.25

LANE = 128
MLA_QK_PAD = 2 * LANE
LOG2E = math.log2(math.e)
VMEM_LIMIT = 56 * 1024 * 1024


def _tile(rows, prefs):
    for t in prefs:
        if rows % t == 0:
            return t
    raise ValueError(f"no tile in {prefs} divides {rows}")


def _params(sem, vmem=VMEM_LIMIT):
    return pltpu.CompilerParams(dimension_semantics=sem, vmem_limit_bytes=vmem)


def _row_mod(mod_ref, k, row0, tm, s_lat):
    d = D_MODEL
    lat = mod_ref[0:1, k * d:(k + 1) * d]
    ctx = mod_ref[1:2, k * d:(k + 1) * d]
    rows = row0 + lax.broadcasted_iota(jnp.int32, (tm, 1), 0)
    return jnp.where(rows >= s_lat, ctx, lat)


def _rms(x, g):
    return x * lax.rsqrt(jnp.mean(x * x, axis=-1, keepdims=True) + RMS_EPS) * g


def _rope(t, cos, sa, sb, quarter):
    return t * cos + pltpu.roll(t, LANE - quarter, 1) * sa + pltpu.roll(t, quarter, 1) * sb


def _adaln_kernel(c_ref, w_ref, b_ref, o_ref):
    a = c_ref[...]
    a = a * jax.nn.sigmoid(a)
    o_ref[0] = jnp.dot(a, w_ref[0], precision=HIGHEST, preferred_element_type=F32) + b_ref[0]


def _adaln(cc, w_ada, b_ada):
    L, d, n = w_ada.shape
    tn = 1024
    return pl.pallas_call(
        _adaln_kernel,
        grid=(L, n // tn),
        in_specs=[pl.BlockSpec((8, d), lambda l, j: (0, 0)),
                  pl.BlockSpec((1, d, tn), lambda l, j: (l, 0, j)),
                  pl.BlockSpec((1, 1, tn), lambda l, j: (l, 0, j))],
        out_specs=pl.BlockSpec((1, 8, tn), lambda l, j: (l, 0, j)),
        out_shape=jax.ShapeDtypeStruct((L, 8, n), F32),
        compiler_params=_params(("arbitrary", "arbitrary")),
        name="adaln",
    )(cc, w_ada, b_ada.reshape(L, 1, n))


def _mla_proj_kernel(x_ref, mod_ref, wa_ref, qn_ref, kvn_ref, wuq_ref, wukv_ref, cos_ref, sa_ref, sb_ref,
                     h_ref, q_ref, k_ref, v_ref, *, tm, s_lat, q_scale):
    row0 = pl.program_id(0) * tm
    sh = _row_mod(mod_ref, 0, row0, tm, s_lat)
    sc = _row_mod(mod_ref, 1, row0, tm, s_lat)
    h = (x_ref[...] * (1.0 + sc) + sh).astype(BF16)
    h_ref[...] = h
    a = jnp.dot(h, wa_ref[...], preferred_element_type=F32)
    cqn = _rms(a[:, :MLA_Q_LORA], qn_ref[...]).astype(BF16)
    ckvn = _rms(a[:, MLA_Q_LORA:MLA_Q_LORA + MLA_KV_LORA], kvn_ref[...]).astype(BF16)
    kr = a[:, MLA_Q_LORA + MLA_KV_LORA:]
    q = jnp.dot(cqn, wuq_ref[...], preferred_element_type=F32)
    kv = jnp.dot(ckvn, wukv_ref[...], preferred_element_type=F32)
    cos, sa, sb = cos_ref[...], sa_ref[...], sb_ref[...]
    quarter = MLA_ROPE // 4
    kr2 = _rope(kr, cos, sa, sb, quarter).astype(BF16)
    for hh in range(MLA_HEADS):
        c0 = hh * MLA_QK_PAD
        q_ref[:, c0:c0 + LANE] = (q[:, c0:c0 + LANE] * q_scale).astype(BF16)
        q_ref[:, c0 + LANE:c0 + 2 * LANE] = (_rope(q[:, c0 + LANE:c0 + 2 * LANE], cos, sa, sb, quarter)
                                             * q_scale).astype(BF16)
        k_ref[:, c0:c0 + LANE] = kv[:, hh * LANE:(hh + 1) * LANE].astype(BF16)
        k_ref[:, c0 + LANE:c0 + 2 * LANE] = kr2
    v_ref[...] = kv[:, MLA_HEADS * MLA_NOPE:].astype(BF16)


def _mla_proj(xs, mod, wa, qn, kvn, wuq, wukv, tabs, s_lat):
    rows, d = xs.shape
    tm = _tile(rows, (640, 512, 256))
    cos, sa, sb = tabs
    row = lambda i: (i, 0)
    full = lambda i: (0, 0)
    qk_w = MLA_HEADS * MLA_QK_PAD
    v_w = MLA_HEADS * MLA_V
    q_scale = (MLA_NOPE + MLA_ROPE) ** -0.5 * LOG2E
    return pl.pallas_call(
        functools.partial(_mla_proj_kernel, tm=tm, s_lat=s_lat, q_scale=q_scale),
        grid=(rows // tm,),
        in_specs=[pl.BlockSpec((tm, d), row), pl.BlockSpec(mod.shape, full),
                  pl.BlockSpec(wa.shape, full), pl.BlockSpec(qn.shape, full), pl.BlockSpec(kvn.shape, full),
                  pl.BlockSpec(wuq.shape, full), pl.BlockSpec(wukv.shape, full),
                  pl.BlockSpec((tm, LANE), row), pl.BlockSpec((tm, LANE), row), pl.BlockSpec((tm, LANE), row)],
        out_specs=[pl.BlockSpec((tm, d), row), pl.BlockSpec((tm, qk_w), row),
                   pl.BlockSpec((tm, qk_w), row), pl.BlockSpec((tm, v_w), row)],
        out_shape=[jax.ShapeDtypeStruct((rows, d), BF16), jax.ShapeDtypeStruct((rows, qk_w), BF16),
                   jax.ShapeDtypeStruct((rows, qk_w), BF16), jax.ShapeDtypeStruct((rows, v_w), BF16)],
        compiler_params=_params(("parallel",)),
        name="mla_proj",
    )(xs, mod, wa, qn, kvn, wuq, wukv, cos, sa, sb)


def _gqa_proj_kernel(h_ref, w_ref, qn_ref, kn_ref, cos_ref, sa_ref, sb_ref, q_ref, k_ref, v_ref, *, q_scale):
    a = jnp.dot(h_ref[...], w_ref[...], preferred_element_type=F32)
    cos, sa, sb = cos_ref[...], sa_ref[...], sb_ref[...]
    quarter = GQA_HEAD_DIM // 4
    for hh in range(GQA_HEADS):
        x = _rms(a[:, hh * LANE:(hh + 1) * LANE], qn_ref[...])
        q_ref[:, hh * LANE:(hh + 1) * LANE] = (_rope(x, cos, sa, sb, quarter) * q_scale).astype(BF16)
    k0 = GQA_HEADS * GQA_HEAD_DIM
    for hh in range(GQA_KV_HEADS):
        x = _rms(a[:, k0 + hh * LANE:k0 + (hh + 1) * LANE], kn_ref[...])
        k_ref[:, hh * LANE:(hh + 1) * LANE] = _rope(x, cos, sa, sb, quarter).astype(BF16)
    v_ref[...] = a[:, k0 + GQA_KV_HEADS * GQA_HEAD_DIM:].astype(BF16)


def _gqa_proj(h, w, qn, kn, tabs):
    rows, d = h.shape
    tm = _tile(rows, (640, 512, 256))
    cos, sa, sb = tabs
    row = lambda i: (i, 0)
    full = lambda i: (0, 0)
    qw = GQA_HEADS * GQA_HEAD_DIM
    kw = GQA_KV_HEADS * GQA_HEAD_DIM
    return pl.pallas_call(
        functools.partial(_gqa_proj_kernel, q_scale=GQA_HEAD_DIM ** -0.5 * LOG2E),
        grid=(rows // tm,),
        in_specs=[pl.BlockSpec((tm, d), row), pl.BlockSpec(w.shape, full),
                  pl.BlockSpec(qn.shape, full), pl.BlockSpec(kn.shape, full),
                  pl.BlockSpec((tm, LANE), row), pl.BlockSpec((tm, LANE), row), pl.BlockSpec((tm, LANE), row)],
        out_specs=[pl.BlockSpec((tm, qw), row), pl.BlockSpec((tm, kw), row), pl.BlockSpec((tm, kw), row)],
        out_shape=[jax.ShapeDtypeStruct((rows, qw), BF16), jax.ShapeDtypeStruct((rows, kw), BF16),
                   jax.ShapeDtypeStruct((rows, kw), BF16)],
        compiler_params=_params(("parallel",)),
        name="gqa_proj",
    )(h, w, qn, kn, cos, sa, sb)


def _diff_proj_kernel(h_ref, w_ref, cos_ref, sa_ref, sb_ref, q_ref, k_ref, v_ref, *, q_scale):
    a = jnp.dot(h_ref[...], w_ref[...], preferred_element_type=F32)
    cos, sa, sb = cos_ref[...], sa_ref[...], sb_ref[...]
    quarter = DIFF_HEAD_DIM // 4
    n = 2 * DIFF_HEADS
    for hh in range(n):
        q_ref[:, hh * LANE:(hh + 1) * LANE] = (_rope(a[:, hh * LANE:(hh + 1) * LANE], cos, sa, sb, quarter)
                                               * q_scale).astype(BF16)
        k_ref[:, hh * LANE:(hh + 1) * LANE] = _rope(a[:, (n + hh) * LANE:(n + hh + 1) * LANE],
                                                    cos, sa, sb, quarter).astype(BF16)
    v_ref[...] = a[:, 2 * n * LANE:].astype(BF16)


def _diff_proj(h, w, tabs):
    rows, d = h.shape
    tm = _tile(rows, (640, 512, 256))
    cos, sa, sb = tabs
    row = lambda i: (i, 0)
    full = lambda i: (0, 0)
    ww = 2 * DIFF_HEADS * DIFF_HEAD_DIM
    vw = DIFF_HEADS * DIFF_V_DIM
    return pl.pallas_call(
        functools.partial(_diff_proj_kernel, q_scale=DIFF_HEAD_DIM ** -0.5 * LOG2E),
        grid=(rows // tm,),
        in_specs=[pl.BlockSpec((tm, d), row), pl.BlockSpec(w.shape, full),
                  pl.BlockSpec((tm, LANE), row), pl.BlockSpec((tm, LANE), row), pl.BlockSpec((tm, LANE), row)],
        out_specs=[pl.BlockSpec((tm, ww), row), pl.BlockSpec((tm, ww), row), pl.BlockSpec((tm, vw), row)],
        out_shape=[jax.ShapeDtypeStruct((rows, ww), BF16), jax.ShapeDtypeStruct((rows, ww), BF16),
                   jax.ShapeDtypeStruct((rows, vw), BF16)],
        compiler_params=_params(("parallel",)),
        name="diff_proj",
    )(h, w, cos, sa, sb)


def _gate_proj_kernel(h_ref, w_ref, o_ref):
    a = jnp.dot(h_ref[...], w_ref[...], preferred_element_type=F32)
    o_ref[...] = jax.nn.sigmoid(a).astype(BF16)


def _gate_proj(h, w):
    rows, d = h.shape
    n = w.shape[1]
    tm = _tile(rows, (640, 512, 256))
    tn = 1536
    return pl.pallas_call(
        _gate_proj_kernel,
        grid=(rows // tm, n // tn),
        in_specs=[pl.BlockSpec((tm, d), lambda i, j: (i, 0)), pl.BlockSpec((d, tn), lambda i, j: (0, j))],
        out_specs=pl.BlockSpec((tm, tn), lambda i, j: (i, j)),
        out_shape=jax.ShapeDtypeStruct((rows, n), BF16),
        compiler_params=_params(("parallel", "arbitrary")),
        name="gate_proj",
    )(h, w)


def _attend(q, k_ref, kcol, dk, v_ref, kv_start, n_chunks, bkv, m_ref, l_ref, acc_ref):
    m_ref[...] = jnp.full(m_ref.shape, -jnp.inf, F32)
    l_ref[...] = jnp.zeros(l_ref.shape, F32)
    acc_ref[...] = jnp.zeros(acc_ref.shape, F32)

    def body(c, carry):
        start = pl.multiple_of(kv_start + c * bkv, LANE)
        kc = k_ref[pl.ds(start, bkv), kcol:kcol + dk]
        vc = v_ref[pl.ds(start, bkv), :]
        s = lax.dot_general(q, kc, (((1,), (1,)), ((), ())), preferred_element_type=F32)
        m_prev = m_ref[...]
        m_new = jnp.maximum(m_prev, jnp.max(s, axis=1, keepdims=True))
        alpha = jnp.exp2(m_prev - m_new)
        p = jnp.exp2(s - m_new)
        l_ref[...] = alpha * l_ref[...] + jnp.sum(p, axis=1, keepdims=True)
        acc_ref[...] = alpha * acc_ref[...] + jnp.dot(p.astype(BF16), vc, preferred_element_type=F32)
        m_ref[...] = m_new
        return carry

    lax.fori_loop(0, n_chunks, body, 0)
    return acc_ref[...] / l_ref[...]


def _kv_ranges(n_lat_tiles, s_lat, s_all, bkv, with_ctx):
    i = pl.program_id(1)
    out = [(i < n_lat_tiles, 0, s_all // bkv, bkv)]
    if with_ctx:
        out.append((i >= n_lat_tiles, s_lat, 1, s_all - s_lat))
    return out


def _mla_attn_kernel(q_ref, k_ref, v_ref, o_ref, m_ref, l_ref, acc_ref, *, geom):
    for pred, kv_start, n_chunks, bkv in _kv_ranges(*geom):
        @pl.when(pred)
        def _():
            o = _attend(q_ref[...], k_ref, 0, MLA_QK_PAD, v_ref, kv_start, n_chunks, bkv, m_ref, l_ref, acc_ref)
            o_ref[...] = o.astype(BF16)


def _gqa_attn_kernel(q_ref, k_ref, v_ref, o_ref, m_ref, l_ref, acc_ref, *, geom, bq):
    g = GQA_HEADS // GQA_KV_HEADS
    for pred, kv_start, n_chunks, bkv in _kv_ranges(*geom):
        @pl.when(pred)
        def _():
            q = jnp.concatenate([q_ref[:, j * LANE:(j + 1) * LANE] for j in range(g)], axis=0)
            o = _attend(q, k_ref, 0, GQA_HEAD_DIM, v_ref, kv_start, n_chunks, bkv, m_ref, l_ref, acc_ref)
            for j in range(g):
                o_ref[:, j * LANE:(j + 1) * LANE] = o[j * bq:(j + 1) * bq].astype(BF16)


def _diff_attn_kernel(q_ref, k_ref, v_ref, lam_ref, dn_ref, o_ref, m_ref, l_ref, acc_ref, *, geom, lam_init):
    lp = lam_ref[...]
    lam = (jnp.exp(jnp.sum(lp[0:1] * lp[1:2], axis=1, keepdims=True))
           - jnp.exp(jnp.sum(lp[2:3] * lp[3:4], axis=1, keepdims=True)) + lam_init)
    for pred, kv_start, n_chunks, bkv in _kv_ranges(*geom):
        @pl.when(pred)
        def _():
            d = DIFF_HEAD_DIM
            o1 = _attend(q_ref[:, :d], k_ref, 0, d, v_ref, kv_start, n_chunks, bkv, m_ref, l_ref, acc_ref)
            o2 = _attend(q_ref[:, d:], k_ref, d, d, v_ref, kv_start, n_chunks, bkv, m_ref, l_ref, acc_ref)
            o = o1 - lam * o2
            o_ref[...] = (_rms(o, dn_ref[...]) * (1.0 - lam_init)).astype(BF16)


def _attention(kind, q, k, v, s_lat, with_ctx, extra=()):
    s_all = q.shape[0]
    bq = 256
    assert s_lat % bq == 0 and (s_all - s_lat) == bq
    n_lat = s_lat // bq
    nq = n_lat + (1 if with_ctx else 0)
    bkv = _tile(s_all, (1280, 1024, 512, 256))
    geom = (n_lat, s_lat, s_all, bkv, with_ctx)
    if kind == "mla":
        heads, qw, kw, vw, ow, m_rows = MLA_HEADS, MLA_QK_PAD, MLA_QK_PAD, MLA_V, MLA_V, bq
        body = functools.partial(_mla_attn_kernel, geom=geom)
    elif kind == "gqa":
        g = GQA_HEADS // GQA_KV_HEADS
        heads, qw, kw, vw, ow, m_rows = GQA_KV_HEADS, g * LANE, LANE, LANE, g * LANE, g * bq
        body = functools.partial(_gqa_attn_kernel, geom=geom, bq=bq)
    else:
        heads, qw, kw, vw, ow, m_rows = DIFF_HEADS, 2 * LANE, 2 * LANE, DIFF_V_DIM, DIFF_V_DIM, bq
        body = functools.partial(_diff_attn_kernel, geom=geom, lam_init=extra[2])
        extra = extra[:2]
    in_specs = [pl.BlockSpec((bq, qw), lambda h, i: (i, h)),
                pl.BlockSpec((s_all, kw), lambda h, i: (0, h)),
                pl.BlockSpec((s_all, vw), lambda h, i: (0, h))]
    in_specs += [pl.BlockSpec(e.shape, lambda h, i: (0, 0)) for e in extra]
    return pl.pallas_call(
        body,
        grid=(heads, nq),
        in_specs=in_specs,
        out_specs=pl.BlockSpec((bq, ow), lambda h, i: (i, h)),
        out_shape=jax.ShapeDtypeStruct((s_all, heads * ow), BF16),
        scratch_shapes=[pltpu.VMEM((m_rows, 1), F32), pltpu.VMEM((m_rows, 1), F32), pltpu.VMEM((m_rows, vw), F32)],
        compiler_params=_params(("parallel", "arbitrary")),
        name=kind + "_attn",
    )(q, k, v, *extra)


def _layer_norm(z, g, b):
    mu = jnp.mean(z, axis=-1, keepdims=True)
    zc = z - mu
    var = jnp.mean(zc * zc, axis=-1, keepdims=True)
    return zc * lax.rsqrt(var + LN_EPS) * g + b


def _merge_kernel(om_ref, og_ref, od_ref, g_ref, x_ref, mod_ref, wbm_ref, wbg_ref, wbd_ref, wo_ref,
                  lng_ref, lnb_ref, xo_ref, h2_ref, *, tm, s_lat):
    d = D_MODEL
    row0 = pl.program_id(0) * tm
    merged = g_ref[:, 0:d].astype(F32) * jnp.dot(om_ref[...], wbm_ref[...], preferred_element_type=F32)
    merged += g_ref[:, d:2 * d].astype(F32) * jnp.dot(og_ref[...], wbg_ref[...], preferred_element_type=F32)
    merged += g_ref[:, 2 * d:3 * d].astype(F32) * jnp.dot(od_ref[...], wbd_ref[...], preferred_element_type=F32)
    y = jnp.dot(merged.astype(BF16), wo_ref[...], preferred_element_type=F32)
    z = DEEPNORM_ALPHA * x_ref[...] + _row_mod(mod_ref, 2, row0, tm, s_lat) * y
    xn = _layer_norm(z, lng_ref[...], lnb_ref[...])
    xo_ref[...] = xn
    h2 = xn * (1.0 + _row_mod(mod_ref, 4, row0, tm, s_lat)) + _row_mod(mod_ref, 3, row0, tm, s_lat)
    h2_ref[...] = h2.astype(BF16)


def _merge(om, og, od, gates, xs, mod, wbm, wbg, wbd, wo, lng, lnb, rows, s_lat):
    d = D_MODEL
    tm = 256
    row = lambda i: (i, 0)
    full = lambda i: (0, 0)
    resident = lambda a: pl.BlockSpec(a.shape, full, pipeline_mode=pl.Buffered(1))
    return pl.pallas_call(
        functools.partial(_merge_kernel, tm=tm, s_lat=s_lat),
        grid=(rows // tm,),
        in_specs=[pl.BlockSpec((tm, om.shape[1]), row), pl.BlockSpec((tm, og.shape[1]), row),
                  pl.BlockSpec((tm, od.shape[1]), row), pl.BlockSpec((tm, gates.shape[1]), row),
                  pl.BlockSpec((tm, d), row), pl.BlockSpec(mod.shape, full),
                  resident(wbm), resident(wbg), resident(wbd), resident(wo),
                  pl.BlockSpec(lng.shape, full), pl.BlockSpec(lnb.shape, full)],
        out_specs=[pl.BlockSpec((tm, d), row), pl.BlockSpec((tm, d), row)],
        out_shape=[jax.ShapeDtypeStruct((rows, d), F32), jax.ShapeDtypeStruct((rows, d), BF16)],
        compiler_params=_params(("parallel",)),
        name="merge",
    )(om, og, od, gates, xs, mod, wbm, wbg, wbd, wo, lng, lnb)


def _router_kernel(x_ref, mod_ref, wr_ref, br_ref, o_ref, *, tm, s_lat):
    row0 = pl.program_id(0) * tm
    h2 = x_ref[...] * (1.0 + _row_mod(mod_ref, 4, row0, tm, s_lat)) + _row_mod(mod_ref, 3, row0, tm, s_lat)
    logits = lax.dot_general(wr_ref[...], h2, (((1,), (1,)), ((), ())),
                             precision=HIGHEST, preferred_element_type=F32)
    scores = jax.nn.sigmoid(logits)
    biased = scores + br_ref[...]
    sc = [scores[e:e + 1, :] for e in range(N_EXPERTS)]
    bi = [biased[e:e + 1, :] for e in range(N_EXPERTS)]
    gs = []
    for g in range(N_GROUPS):
        a, b, c, dd = bi[4 * g:4 * g + 4]
        hi1, lo1, hi2, lo2 = jnp.maximum(a, b), jnp.minimum(a, b), jnp.maximum(c, dd), jnp.minimum(c, dd)
        gs.append(jnp.maximum(hi1, hi2) + jnp.maximum(jnp.minimum(hi1, hi2), jnp.maximum(lo1, lo2)))
    best = jnp.maximum(jnp.maximum(gs[0], gs[1]), jnp.maximum(gs[2], gs[3]))
    gsel = jnp.where(gs[0] == best, 0, jnp.where(gs[1] == best, 1, jnp.where(gs[2] == best, 2, 3)))
    pick = lambda vals, j: jnp.where(gsel == 0, vals[j], jnp.where(gsel == 1, vals[4 + j],
                                     jnp.where(gsel == 2, vals[8 + j], vals[12 + j])))
    xb = [pick(bi, j) for j in range(EXPERTS_PER_GROUP)]
    xs = [pick(sc, j) for j in range(EXPERTS_PER_GROUP)]
    w = []
    for i in range(EXPERTS_PER_GROUP):
        rank = jnp.zeros_like(gsel)
        for j in range(EXPERTS_PER_GROUP):
            if j == i:
                continue
            beats = (xb[j] >= xb[i]) if j < i else (xb[j] > xb[i])
            rank = rank + beats.astype(jnp.int32)
        w.append(jnp.where(rank < 2, xs[i], 0.0))
    inv = ROUTED_SCALE / (w[0] + w[1] + w[2] + w[3])
    for e in range(N_EXPERTS):
        o_ref[e:e + 1, :] = jnp.where(gsel == e // EXPERTS_PER_GROUP, w[e % EXPERTS_PER_GROUP] * inv, 0.0)


def _router(xs, mod, wr_t, br, rows, s_lat):
    d = D_MODEL
    tm = _tile(rows, (640, 512, 256))
    return pl.pallas_call(
        functools.partial(_router_kernel, tm=tm, s_lat=s_lat),
        grid=(rows // tm,),
        in_specs=[pl.BlockSpec((tm, d), lambda i: (i, 0)), pl.BlockSpec(mod.shape, lambda i: (0, 0)),
                  pl.BlockSpec(wr_t.shape, lambda i: (0, 0)), pl.BlockSpec(br.shape, lambda i: (0, 0))],
        out_specs=pl.BlockSpec((N_EXPERTS, tm), lambda i: (0, i)),
        out_shape=jax.ShapeDtypeStruct((N_EXPERTS, rows), F32),
        compiler_params=_params(("parallel",)),
        name="router",
    )(xs, mod, wr_t, br)


def _moe_kernel(h_ref, cmb_ref, wg_ref, wu_ref, wd_ref, x_ref, mod_ref, lng_ref, lnb_ref, o_ref, acc_ref,
                *, tm, s_lat, n_slots):
    e = pl.program_id(1)
    h = h_ref[...]
    a = jnp.dot(h, wg_ref[0], preferred_element_type=F32)
    b = jnp.dot(h, wu_ref[0], preferred_element_type=F32)
    t = (a * jax.nn.sigmoid(a) * b).astype(BF16)
    y = jnp.dot(t, wd_ref[0], preferred_element_type=F32)
    lane = lax.broadcasted_iota(jnp.int32, cmb_ref.shape, 1)
    c = jnp.sum(jnp.where(lane == e, cmb_ref[...], 0.0), axis=1, keepdims=True)

    @pl.when(e == 0)
    def _():
        acc_ref[...] = c * y

    @pl.when(e > 0)
    def _():
        acc_ref[...] += c * y

    @pl.when(e == n_slots - 1)
    def _():
        row0 = pl.program_id(0) * tm
        z = DEEPNORM_ALPHA * x_ref[...] + _row_mod(mod_ref, 5, row0, tm, s_lat) * acc_ref[...]
        o_ref[...] = _layer_norm(z, lng_ref[...], lnb_ref[...])


def _moe(h2, cmb, wg, wu, wd, xs, mod, lng, lnb, rows, s_lat):
    d = D_MODEL
    n_slots, _, f = wg.shape
    tm = _tile(rows, (640, 512, 256))
    row = lambda i, e: (i, 0)
    full = lambda i, e: (0, 0)
    return pl.pallas_call(
        functools.partial(_moe_kernel, tm=tm, s_lat=s_lat, n_slots=n_slots),
        grid=(rows // tm, n_slots),
        in_specs=[pl.BlockSpec((tm, d), row), pl.BlockSpec((tm, cmb.shape[1]), row),
                  pl.BlockSpec((1, d, f), lambda i, e: (e, 0, 0)), pl.BlockSpec((1, d, f), lambda i, e: (e, 0, 0)),
                  pl.BlockSpec((1, f, d), lambda i, e: (e, 0, 0)),
                  pl.BlockSpec((tm, d), row), pl.BlockSpec(mod.shape, full),
                  pl.BlockSpec(lng.shape, full), pl.BlockSpec(lnb.shape, full)],
        out_specs=pl.BlockSpec((tm, d), row),
        out_shape=jax.ShapeDtypeStruct((rows, d), F32),
        scratch_shapes=[pltpu.VMEM((tm, d), F32)],
        compiler_params=_params(("parallel", "arbitrary")),
        name="moe",
    )(h2, cmb, wg, wu, wd, xs, mod, lng, lnb)


def _rope_tables(s_lat, n_ctx, dim):
    quarter = dim // 4
    rows = s_lat // GRID_W
    row = jnp.repeat(jnp.arange(rows, dtype=F32), GRID_W)
    col = jnp.tile(jnp.arange(GRID_W, dtype=F32), rows)
    inv_freq = ROPE_THETA ** (-jnp.arange(quarter, dtype=F32) / quarter)
    ar, ac = row[:, None] * inv_freq, col[:, None] * inv_freq
    zero = jnp.zeros_like(ar)
    pad = jnp.zeros((s_lat, LANE - dim), F32)
    cos = jnp.concatenate([jnp.cos(ar), jnp.cos(ar), jnp.cos(ac), jnp.cos(ac), pad + 1.0], axis=1)
    sa = jnp.concatenate([-jnp.sin(ar), zero, -jnp.sin(ac), zero, pad], axis=1)
    sb = jnp.concatenate([zero, jnp.sin(ar), zero, jnp.sin(ac), pad], axis=1)
    ctx1 = jnp.ones((n_ctx, LANE), F32)
    ctx0 = jnp.zeros((n_ctx, LANE), F32)
    return (jnp.concatenate([cos, ctx1], axis=0), jnp.concatenate([sa, ctx0], axis=0),
            jnp.concatenate([sb, ctx0], axis=0))


def kernel(x, c, ctx, c_ctx, w_ada, b_ada, w_in, mla_q_norm, mla_w_uq, mla_kv_norm, mla_w_ukv, gqa_q_norm,
           gqa_k_norm, diff_lambda, diff_norm, w_br_mla, w_br_gqa, w_br_diff, w_o, ln1_g, ln1_b, w_router,
           b_router, moe_w_gate, moe_w_up, moe_w_down, shared_w_gate, shared_w_up, shared_w_down, ln2_g, ln2_b):
    B, s_lat, d = x.shape
    n_ctx = ctx.shape[1]
    assert B == 1 and d == D_MODEL
    s_all = s_lat + n_ctx

    xs = jnp.concatenate([x[0], ctx[0]], axis=0)
    cc = jnp.zeros((8, d), F32).at[0].set(c[0]).at[1].set(c_ctx)
    mod_all = _adaln(cc, w_ada, b_ada)

    tabs64 = _rope_tables(s_lat, n_ctx, MLA_ROPE)
    tabs128 = _rope_tables(s_lat, n_ctx, GQA_HEAD_DIM)
    wr_t = w_router.T
    br = b_router.reshape(N_EXPERTS, 1)
    row2 = lambda v: v.reshape(1, -1)

    o0 = MLA_Q_LORA + MLA_KV_LORA + MLA_ROPE
    o1 = o0 + GQA_HEADS * GQA_HEAD_DIM + 2 * GQA_KV_HEADS * GQA_HEAD_DIM
    o2 = o1 + 2 * (2 * DIFF_HEADS * DIFF_HEAD_DIM) + DIFF_HEADS * DIFF_V_DIM

    for l in range(DEPTH):
        last = l == DEPTH - 1
        mod = mod_all[l]
        w = w_in[l]
        wa = jnp.pad(w[:, :o0], ((0, 0), (0, LANE - MLA_ROPE))).astype(BF16)
        wb = w[:, o0:o1].astype(BF16)
        wc = w[:, o1:o2].astype(BF16)
        wd_gate = w[:, o2:].astype(BF16)
        wuq = jnp.pad(mla_w_uq[l].reshape(MLA_Q_LORA, MLA_HEADS, MLA_NOPE + MLA_ROPE),
                      ((0, 0), (0, 0), (0, MLA_QK_PAD - MLA_NOPE - MLA_ROPE))
                      ).reshape(MLA_Q_LORA, MLA_HEADS * MLA_QK_PAD).astype(BF16)
        ukv = mla_w_ukv[l].reshape(MLA_KV_LORA, MLA_HEADS, MLA_NOPE + MLA_V)
        wukv = jnp.concatenate([ukv[:, :, :MLA_NOPE].reshape(MLA_KV_LORA, -1),
                                ukv[:, :, MLA_NOPE:].reshape(MLA_KV_LORA, -1)], axis=1).astype(BF16)

        h, q_mla, k_mla, v_mla = _mla_proj(xs, mod, wa, row2(mla_q_norm[l]), row2(mla_kv_norm[l]),
                                           wuq, wukv, tabs64, s_lat)
        q_gqa, k_gqa, v_gqa = _gqa_proj(h, wb, row2(gqa_q_norm[l]), row2(gqa_k_norm[l]), tabs128)
        q_dif, k_dif, v_dif = _diff_proj(h, wc, tabs128)
        gates = _gate_proj(h, wd_gate)

        lam_init = 0.8 - 0.6 * math.exp(-0.3 * l)
        o_mla = _attention("mla", q_mla, k_mla, v_mla, s_lat, not last)
        o_gqa = _attention("gqa", q_gqa, k_gqa, v_gqa, s_lat, not last)
        o_dif = _attention("diff", q_dif, k_dif, v_dif, s_lat, not last,
                           extra=(diff_lambda[l], row2(diff_norm[l]), lam_init))

        rows = s_lat if last else s_all
        xs, h2 = _merge(o_mla, o_gqa, o_dif, gates, xs, mod, w_br_mla[l].astype(BF16), w_br_gqa[l].astype(BF16),
                        w_br_diff[l].astype(BF16), w_o[l].astype(BF16), row2(ln1_g[l]), row2(ln1_b[l]), rows, s_lat)

        cmb_t = _router(xs, mod, wr_t, br, rows, s_lat)
        cmb = jnp.concatenate([jnp.ones((rows, 1), F32), cmb_t.T], axis=1)
        wg = jnp.concatenate([shared_w_gate[l][None], moe_w_gate[l]], axis=0).astype(BF16)
        wu = jnp.concatenate([shared_w_up[l][None], moe_w_up[l]], axis=0).astype(BF16)
        wdn = jnp.concatenate([shared_w_down[l][None], moe_w_down[l]], axis=0).astype(BF16)
        xs = _moe(h2, cmb, wg, wu, wdn, xs, mod, row2(ln2_g[l]), row2(ln2_b[l]), rows, s_lat)

    return xs[None]
```

```python
import functools
import math

import jax
import jax.numpy as jnp
from jax import lax
from jax.experimental import pallas as pl
from jax.experimental.pallas import tpu as pltpu

F32 = jnp.float32
BF16 = jnp.bfloat16
HIGHEST = lax.Precision.HIGHEST

D_MODEL = 2048
DEPTH = 2
GRID_W = 64
ROPE_THETA = 10000.0
LN_EPS = 1e-5
RMS_EPS = 1e-6
MLA_HEADS = 8
MLA_Q_LORA = 512
MLA_KV_LORA = 512
MLA_NOPE = 128
MLA_ROPE = 64
MLA_V = 128
GQA_HEADS = 8
GQA_KV_HEADS = 2
GQA_HEAD_DIM = 128
DIFF_HEADS = 4
DIFF_HEAD_DIM = 128
DIFF_V_DIM = 2 * DIFF_HEAD_DIM
N_EXPERTS = 16
N_GROUPS = 4
EXPERTS_PER_GROUP = N_EXPERTS // N_GROUPS
MOE_D_FF = 512
ROUTED_SCALE = 1.0
DEEPNORM_ALPHA = (2 * DEPTH) ** 0.25

LANE = 128
MLA_QK_PAD = 2 * LANE
LOG2E = math.log2(math.e)
VMEM_LIMIT = 56 * 1024 * 1024


def _tile(rows, prefs):
    for t in prefs:
        if rows % t == 0:
            return t
    raise ValueError(f"no tile in {prefs} divides {rows}")


def _params(sem, vmem=VMEM_LIMIT):
    return pltpu.CompilerParams(dimension_semantics=sem, vmem_limit_bytes=vmem)


def _row_mod(mod_ref, k, row0, tm, s_lat):
    d = D_MODEL
    lat = mod_ref[0:1, k * d:(k + 1) * d]
    ctx = mod_ref[1:2, k * d:(k + 1) * d]
    rows = row0 + lax.broadcasted_iota(jnp.int32, (tm, 1), 0)
    return jnp.where(rows >= s_lat, ctx, lat)


def _rms(x, g):
    return x * lax.rsqrt(jnp.mean(x * x, axis=-1, keepdims=True) + RMS_EPS) * g


def _rope(t, cos, sa, sb, quarter):
    return t * cos + pltpu.roll(t, LANE - quarter, 1) * sa + pltpu.roll(t, quarter, 1) * sb


def _adaln_kernel(c_ref, w_ref, b_ref, o_ref):
    a = c_ref[...]
    a = a * jax.nn.sigmoid(a)
    o_ref[0] = jnp.dot(a, w_ref[0], precision=HIGHEST, preferred_element_type=F32) + b_ref[0]


def _adaln(cc, w_ada, b_ada):
    L, d, n = w_ada.shape
    tn = 1024
    return pl.pallas_call(
        _adaln_kernel,
        grid=(L, n // tn),
        in_specs=[pl.BlockSpec((8, d), lambda l, j: (0, 0)),
                  pl.BlockSpec((1, d, tn), lambda l, j: (l, 0, j)),
                  pl.BlockSpec((1, 1, tn), lambda l, j: (l, 0, j))],
        out_specs=pl.BlockSpec((1, 8, tn), lambda l, j: (l, 0, j)),
        out_shape=jax.ShapeDtypeStruct((L, 8, n), F32),
        compiler_params=_params(("arbitrary", "arbitrary")),
        name="adaln",
    )(cc, w_ada, b_ada.reshape(L, 1, n))


def _mla_proj_kernel(x_ref, mod_ref, wa_ref, qn_ref, kvn_ref, wuq_ref, wukv_ref, cos_ref, sa_ref, sb_ref,
                     h_ref, q_ref, k_ref, v_ref, *, tm, s_lat, q_scale):
    row0 = pl.program_id(0) * tm
    sh = _row_mod(mod_ref, 0, row0, tm, s_lat)
    sc = _row_mod(mod_ref, 1, row0, tm, s_lat)
    h = (x_ref[...] * (1.0 + sc) + sh).astype(BF16)
    h_ref[...] = h
    a = jnp.dot(h, wa_ref[...], preferred_element_type=F32)
    cqn = _rms(a[:, :MLA_Q_LORA], qn_ref[...]).astype(BF16)
    ckvn = _rms(a[:, MLA_Q_LORA:MLA_Q_LORA + MLA_KV_LORA], kvn_ref[...]).astype(BF16)
    kr = a[:, MLA_Q_LORA + MLA_KV_LORA:]
    q = jnp.dot(cqn, wuq_ref[...], preferred_element_type=F32)
    kv = jnp.dot(ckvn, wukv_ref[...], preferred_element_type=F32)
    cos, sa, sb = cos_ref[...], sa_ref[...], sb_ref[...]
    quarter = MLA_ROPE // 4
    kr2 = _rope(kr, cos, sa, sb, quarter).astype(BF16)
    for hh in range(MLA_HEADS):
        c0 = hh * MLA_QK_PAD
        q_ref[:, c0:c0 + LANE] = (q[:, c0:c0 + LANE] * q_scale).astype(BF16)
        q_ref[:, c0 + LANE:c0 + 2 * LANE] = (_rope(q[:, c0 + LANE:c0 + 2 * LANE], cos, sa, sb, quarter)
                                             * q_scale).astype(BF16)
        k_ref[:, c0:c0 + LANE] = kv[:, hh * LANE:(hh + 1) * LANE].astype(BF16)
        k_ref[:, c0 + LANE:c0 + 2 * LANE] = kr2
    v_ref[...] = kv[:, MLA_HEADS * MLA_NOPE:].astype(BF16)


def _mla_proj(xs, mod, wa, qn, kvn, wuq, wukv, tabs, s_lat):
    rows, d = xs.shape
    tm = _tile(rows, (640, 512, 256))
    cos, sa, sb = tabs
    row = lambda i: (i, 0)
    full = lambda i: (0, 0)
    qk_w = MLA_HEADS * MLA_QK_PAD
    v_w = MLA_HEADS * MLA_V
    q_scale = (MLA_NOPE + MLA_ROPE) ** -0.5 * LOG2E
    return pl.pallas_call(
        functools.partial(_mla_proj_kernel, tm=tm, s_lat=s_lat, q_scale=q_scale),
        grid=(rows // tm,),
        in_specs=[pl.BlockSpec((tm, d), row), pl.BlockSpec(mod.shape, full),
                  pl.BlockSpec(wa.shape, full), pl.BlockSpec(qn.shape, full), pl.BlockSpec(kvn.shape, full),
                  pl.BlockSpec(wuq.shape, full), pl.BlockSpec(wukv.shape, full),
                  pl.BlockSpec((tm, LANE), row), pl.BlockSpec((tm, LANE), row), pl.BlockSpec((tm, LANE), row)],
        out_specs=[pl.BlockSpec((tm, d), row), pl.BlockSpec((tm, qk_w), row),
                   pl.BlockSpec((tm, qk_w), row), pl.BlockSpec((tm, v_w), row)],
        out_shape=[jax.ShapeDtypeStruct((rows, d), BF16), jax.ShapeDtypeStruct((rows, qk_w), BF16),
                   jax.ShapeDtypeStruct((rows, qk_w), BF16), jax.ShapeDtypeStruct((rows, v_w), BF16)],
        compiler_params=_params(("parallel",)),
        name="mla_proj",
    )(xs, mod, wa, qn, kvn, wuq, wukv, cos, sa, sb)


def _gqa_proj_kernel(h_ref, w_ref, qn_ref, kn_ref, cos_ref, sa_ref, sb_ref, q_ref, k_ref, v_ref, *, q_scale):
    a = jnp.dot(h_ref[...], w_ref[...], preferred_element_type=F32)
    cos, sa, sb = cos_ref[...], sa_ref[...], sb_ref[...]
    quarter = GQA_HEAD_DIM // 4
    for hh in range(GQA_HEADS):
        x = _rms(a[:, hh * LANE:(hh + 1) * LANE], qn_ref[...])
        q_ref[:, hh * LANE:(hh + 1) * LANE] = (_rope(x, cos, sa, sb, quarter) * q_scale).astype(BF16)
    k0 = GQA_HEADS * GQA_HEAD_DIM
    for hh in range(GQA_KV_HEADS):
        x = _rms(a[:, k0 + hh * LANE:k0 + (hh + 1) * LANE], kn_ref[...])
        k_ref[:, hh * LANE:(hh + 1) * LANE] = _rope(x, cos, sa, sb, quarter).astype(BF16)
    v_ref[...] = a[:, k0 + GQA_KV_HEADS * GQA_HEAD_DIM:].astype(BF16)


def _gqa_proj(h, w, qn, kn, tabs):
    rows, d = h.shape
    tm = _tile(rows, (640, 512, 256))
    cos, sa, sb = tabs
    row = lambda i: (i, 0)
    full = lambda i: (0, 0)
    qw = GQA_HEADS * GQA_HEAD_DIM
    kw = GQA_KV_HEADS * GQA_HEAD_DIM
    return pl.pallas_call(
        functools.partial(_gqa_proj_kernel, q_scale=GQA_HEAD_DIM ** -0.5 * LOG2E),
        grid=(rows // tm,),
        in_specs=[pl.BlockSpec((tm, d), row), pl.BlockSpec(w.shape, full),
                  pl.BlockSpec(qn.shape, full), pl.BlockSpec(kn.shape, full),
                  pl.BlockSpec((tm, LANE), row), pl.BlockSpec((tm, LANE), row), pl.BlockSpec((tm, LANE), row)],
        out_specs=[pl.BlockSpec((tm, qw), row), pl.BlockSpec((tm, kw), row), pl.BlockSpec((tm, kw), row)],
        out_shape=[jax.ShapeDtypeStruct((rows, qw), BF16), jax.ShapeDtypeStruct((rows, kw), BF16),
                   jax.ShapeDtypeStruct((rows, kw), BF16)],
        compiler_params=_params(("parallel",)),
        name="gqa_proj",
    )(h, w, qn, kn, cos, sa, sb)


def _diff_proj_kernel(h_ref, w_ref, cos_ref, sa_ref, sb_ref, q_ref, k_ref, v_ref, *, q_scale):
    a = jnp.dot(h_ref[...], w_ref[...], preferred_element_type=F32)
    cos, sa, sb = cos_ref[...], sa_ref[...], sb_ref[...]
    quarter = DIFF_HEAD_DIM // 4
    n = 2 * DIFF_HEADS
    for hh in range(n):
        q_ref[:, hh * LANE:(hh + 1) * LANE] = (_rope(a[:, hh * LANE:(hh + 1) * LANE], cos, sa, sb, quarter)
                                               * q_scale).astype(BF16)
        k_ref[:, hh * LANE:(hh + 1) * LANE] = _rope(a[:, (n + hh) * LANE:(n + hh + 1) * LANE],
                                                    cos, sa, sb, quarter).astype(BF16)
    v_ref[...] = a[:, 2 * n * LANE:].astype(BF16)


def _diff_proj(h, w, tabs):
    rows, d = h.shape
    tm = _tile(rows, (640, 512, 256))
    cos, sa, sb = tabs
    row = lambda i: (i, 0)
    full = lambda i: (0, 0)
    ww = 2 * DIFF_HEADS * DIFF_HEAD_DIM
    vw = DIFF_HEADS * DIFF_V_DIM
    return pl.pallas_call(
        functools.partial(_diff_proj_kernel, q_scale=DIFF_HEAD_DIM ** -0.5 * LOG2E),
        grid=(rows // tm,),
        in_specs=[pl.BlockSpec((tm, d), row), pl.BlockSpec(w.shape, full),
                  pl.BlockSpec((tm, LANE), row), pl.BlockSpec((tm, LANE), row), pl.BlockSpec((tm, LANE), row)],
        out_specs=[pl.BlockSpec((tm, ww), row), pl.BlockSpec((tm, ww), row), pl.BlockSpec((tm, vw), row)],
        out_shape=[jax.ShapeDtypeStruct((rows, ww), BF16), jax.ShapeDtypeStruct((rows, ww), BF16),
                   jax.ShapeDtypeStruct((rows, vw), BF16)],
        compiler_params=_params(("parallel",)),
        name="diff_proj",
    )(h, w, cos, sa, sb)


def _gate_proj_kernel(h_ref, w_ref, o_ref):
    a = jnp.dot(h_ref[...], w_ref[...], preferred_element_type=F32)
    o_ref[...] = jax.nn.sigmoid(a).astype(BF16)


def _gate_proj(h, w):
    rows, d = h.shape
    n = w.shape[1]
    tm = _tile(rows, (640, 512, 256))
    tn = 1536
    return pl.pallas_call(
        _gate_proj_kernel,
        grid=(rows // tm, n // tn),
        in_specs=[pl.BlockSpec((tm, d), lambda i, j: (i, 0)), pl.BlockSpec((d, tn), lambda i, j: (0, j))],
        out_specs=pl.BlockSpec((tm, tn), lambda i, j: (i, j)),
        out_shape=jax.ShapeDtypeStruct((rows, n), BF16),
        compiler_params=_params(("parallel", "arbitrary")),
        name="gate_proj",
    )(h, w)


def _rows(start, size):
    return pl.ds(start if isinstance(start, int) else pl.multiple_of(start, size), size)


def _dot_nt(a, b):
    return lax.dot_general(a, b, (((1,), (1,)), ((), ())), preferred_element_type=F32)


def _flash(score, v_ref, n_sub, n_chunks, bkv, s_ref, mx_ref, m_ref, l_ref, acc_ref):
    m_ref[...] = jnp.full(m_ref.shape, -jnp.inf, F32)
    l_ref[...] = jnp.zeros(l_ref.shape, F32)
    acc_ref[...] = jnp.zeros(acc_ref.shape, F32)

    def item(k):
        if isinstance(k, int):
            return k // n_chunks, k % n_chunks
        t = lax.div(k, n_chunks)
        return t, k - t * n_chunks

    def issue(k, slot):
        t, c = item(k)
        score(t, c, slot)

    def absorb(k, slot):
        t, c = item(k)
        m_prev = m_ref[t]
        m_new = jnp.maximum(m_prev, mx_ref[slot])
        alpha = jnp.exp2(m_prev - m_new)
        p = jnp.exp2(s_ref[slot] - m_new)
        l_ref[t] = alpha * l_ref[t] + jnp.sum(p, axis=1, keepdims=True)
        acc_ref[t] = alpha * acc_ref[t] + jnp.dot(p.astype(BF16), v_ref[_rows(c * bkv, bkv), :],
                                                  preferred_element_type=F32)
        m_ref[t] = m_new

    n_items = n_sub * n_chunks
    issue(0, 0)

    def body(j, carry):
        k = 2 * j
        issue(k + 1, 1)
        absorb(k, 0)
        issue(k + 2, 0)
        absorb(k + 1, 1)
        return carry

    n_pairs = (n_items - 1) // 2
    if n_pairs:
        lax.fori_loop(0, n_pairs, body, 0)
    k = 2 * n_pairs
    if n_items % 2 == 1:
        absorb(k, 0)
    else:
        issue(k + 1, 1)
        absorb(k, 0)
        absorb(k + 1, 1)


def _score_into(s_ref, mx_ref, slot, row0, q, kc):
    s = _dot_nt(q, kc)
    rows = q.shape[0]
    s_ref[slot, row0:row0 + rows] = s
    mx_ref[slot, row0:row0 + rows] = jnp.max(s, axis=1, keepdims=True)


def _mla_attn_kernel(q_ref, k_ref, v_ref, *rest, n_sub, n_chunks, bkv):
    o_ref, s_ref, mx_ref, m_ref, l_ref, acc_ref = rest[-6:]
    m_rows = s_ref.shape[1]

    def score(t, c, slot):
        _score_into(s_ref, mx_ref, slot, 0, q_ref[_rows(t * m_rows, m_rows), :], k_ref[_rows(c * bkv, bkv), :])

    _flash(score, v_ref, n_sub, n_chunks, bkv, s_ref, mx_ref, m_ref, l_ref, acc_ref)
    for t in range(n_sub):
        o_ref[t * m_rows:(t + 1) * m_rows, :] = (acc_ref[t] / l_ref[t]).astype(BF16)


def _gqa_attn_kernel(q_ref, k_ref, v_ref, *rest, n_sub, n_chunks, bkv):
    o_ref, qs_ref, s_ref, mx_ref, m_ref, l_ref, acc_ref = rest[-7:]
    g = GQA_HEADS // GQA_KV_HEADS
    bqs = s_ref.shape[1] // g
    for t in range(n_sub):
        for j in range(g):
            qs_ref[t, j * bqs:(j + 1) * bqs] = q_ref[t * bqs:(t + 1) * bqs, j * LANE:(j + 1) * LANE]

    def score(t, c, slot):
        _score_into(s_ref, mx_ref, slot, 0, qs_ref[t], k_ref[_rows(c * bkv, bkv), :])

    _flash(score, v_ref, n_sub, n_chunks, bkv, s_ref, mx_ref, m_ref, l_ref, acc_ref)
    for t in range(n_sub):
        o = acc_ref[t] / l_ref[t]
        for j in range(g):
            o_ref[t * bqs:(t + 1) * bqs, j * LANE:(j + 1) * LANE] = o[j * bqs:(j + 1) * bqs].astype(BF16)


def _diff_attn_kernel(q_ref, k_ref, v_ref, lam_ref, dn_ref, *rest, n_sub, n_chunks, bkv, lam_init):
    o_ref, s_ref, mx_ref, m_ref, l_ref, acc_ref = rest[-6:]
    d = DIFF_HEAD_DIM
    bqs = s_ref.shape[1] // 2
    lp = lam_ref[...]
    lam = (jnp.exp(jnp.sum(lp[0:1] * lp[1:2], axis=1, keepdims=True))
           - jnp.exp(jnp.sum(lp[2:3] * lp[3:4], axis=1, keepdims=True)) + lam_init)

    def score(t, c, slot):
        qr, kr = _rows(t * bqs, bqs), _rows(c * bkv, bkv)
        _score_into(s_ref, mx_ref, slot, 0, q_ref[qr, 0:d], k_ref[kr, 0:d])
        _score_into(s_ref, mx_ref, slot, bqs, q_ref[qr, d:2 * d], k_ref[kr, d:2 * d])

    _flash(score, v_ref, n_sub, n_chunks, bkv, s_ref, mx_ref, m_ref, l_ref, acc_ref)
    for t in range(n_sub):
        o = acc_ref[t] / l_ref[t]
        o = o[:bqs] - lam * o[bqs:]
        o_ref[t * bqs:(t + 1) * bqs, :] = (_rms(o, dn_ref[...]) * (1.0 - lam_init)).astype(BF16)


def _attention(kind, q, k, v, s_lat, out_rows, extra=(), ctx_into=None):
    s_all = q.shape[0]
    n_ctx = s_all - s_lat
    assert s_lat % n_ctx == 0
    ctx = ctx_into is not None
    if kind == "mla":
        heads, qw, kw, vw, ow, stack = MLA_HEADS, MLA_QK_PAD, MLA_QK_PAD, MLA_V, MLA_V, 1
        bqs = n_ctx if ctx else _tile(s_lat, (1024, 512, 256))
        body = _mla_attn_kernel
    elif kind == "gqa":
        g = GQA_HEADS // GQA_KV_HEADS
        heads, qw, kw, vw, ow, stack = GQA_KV_HEADS, g * LANE, LANE, LANE, g * LANE, g
        bqs = n_ctx if ctx else 256
        body = _gqa_attn_kernel
    else:
        heads, qw, kw, vw, ow, stack = DIFF_HEADS, 2 * LANE, 2 * LANE, DIFF_V_DIM, DIFF_V_DIM, 2
        bqs = n_ctx if ctx else _tile(s_lat, (512, 256))
        body = functools.partial(_diff_attn_kernel, lam_init=extra[2])
        extra = extra[:2]
    m_rows = stack * bqs
    n_sub = 1 if ctx else _tile(s_lat // bqs, (4, 3, 2, 1))
    bq = n_sub * bqs
    if ctx:
        kv_rows, bkv, row_blk, nq = n_ctx, n_ctx, s_lat // n_ctx, 1
    else:
        kv_rows, bkv, row_blk, nq = s_all, _tile(s_all, (1280, 1024, 512, 256)), 0, s_lat // bq
    body = functools.partial(body, n_sub=n_sub, n_chunks=kv_rows // bkv, bkv=bkv)
    kv_blk = row_blk
    in_specs = [pl.BlockSpec((bq, qw), lambda h, i: (row_blk + i, h)),
                pl.BlockSpec((kv_rows, kw), lambda h, i: (kv_blk, h), pipeline_mode=pl.Buffered(1)),
                pl.BlockSpec((kv_rows, vw), lambda h, i: (kv_blk, h), pipeline_mode=pl.Buffered(1))]
    in_specs += [pl.BlockSpec(e.shape, lambda h, i: (0, 0)) for e in extra]
    args = [q, k, v, *extra]
    aliases = {}
    if ctx:
        in_specs.append(pl.BlockSpec(memory_space=pl.ANY))
        aliases = {len(args): 0}
        args.append(ctx_into)
    scratch = []
    if kind == "gqa":
        scratch.append(pltpu.VMEM((n_sub, m_rows, LANE), BF16))
    scratch += [pltpu.VMEM((2, m_rows, bkv), F32),
                pltpu.VMEM((2, m_rows, 1), F32), pltpu.VMEM((n_sub, m_rows, 1), F32),
                pltpu.VMEM((n_sub, m_rows, 1), F32), pltpu.VMEM((n_sub, m_rows, vw), F32)]
    return pl.pallas_call(
        body,
        grid=(heads, nq),
        in_specs=in_specs,
        out_specs=pl.BlockSpec((bq, ow), lambda h, i: (row_blk + i, h)),
        out_shape=jax.ShapeDtypeStruct((out_rows, heads * ow), BF16),
        scratch_shapes=scratch,
        input_output_aliases=aliases,
        compiler_params=_params(("parallel", "arbitrary")),
        name=kind + ("_attn_ctx" if ctx else "_attn"),
    )(*args)


def _layer_norm(z, g, b):
    mu = jnp.mean(z, axis=-1, keepdims=True)
    zc = z - mu
    var = jnp.mean(zc * zc, axis=-1, keepdims=True)
    return zc * lax.rsqrt(var + LN_EPS) * g + b


def _merge_kernel(om_ref, og_ref, od_ref, g_ref, x_ref, mod_ref, wbm_ref, wbg_ref, wbd_ref, wo_ref,
                  lng_ref, lnb_ref, xo_ref, h2_ref, *, tm, s_lat):
    d = D_MODEL
    row0 = pl.program_id(0) * tm
    merged = g_ref[:, 0:d].astype(F32) * jnp.dot(om_ref[...], wbm_ref[...], preferred_element_type=F32)
    merged += g_ref[:, d:2 * d].astype(F32) * jnp.dot(og_ref[...], wbg_ref[...], preferred_element_type=F32)
    merged += g_ref[:, 2 * d:3 * d].astype(F32) * jnp.dot(od_ref[...], wbd_ref[...], preferred_element_type=F32)
    y = jnp.dot(merged.astype(BF16), wo_ref[...], preferred_element_type=F32)
    z = DEEPNORM_ALPHA * x_ref[...] + _row_mod(mod_ref, 2, row0, tm, s_lat) * y
    xn = _layer_norm(z, lng_ref[...], lnb_ref[...])
    xo_ref[...] = xn
    h2 = xn * (1.0 + _row_mod(mod_ref, 4, row0, tm, s_lat)) + _row_mod(mod_ref, 3, row0, tm, s_lat)
    h2_ref[...] = h2.astype(BF16)


def _merge(om, og, od, gates, xs, mod, wbm, wbg, wbd, wo, lng, lnb, rows, s_lat):
    d = D_MODEL
    tm = 256
    row = lambda i: (i, 0)
    full = lambda i: (0, 0)
    resident = lambda a: pl.BlockSpec(a.shape, full, pipeline_mode=pl.Buffered(1))
    return pl.pallas_call(
        functools.partial(_merge_kernel, tm=tm, s_lat=s_lat),
        grid=(rows // tm,),
        in_specs=[pl.BlockSpec((tm, om.shape[1]), row), pl.BlockSpec((tm, og.shape[1]), row),
                  pl.BlockSpec((tm, od.shape[1]), row), pl.BlockSpec((tm, gates.shape[1]), row),
                  pl.BlockSpec((tm, d), row), pl.BlockSpec(mod.shape, full),
                  resident(wbm), resident(wbg), resident(wbd), resident(wo),
                  pl.BlockSpec(lng.shape, full), pl.BlockSpec(lnb.shape, full)],
        out_specs=[pl.BlockSpec((tm, d), row), pl.BlockSpec((tm, d), row)],
        out_shape=[jax.ShapeDtypeStruct((rows, d), F32), jax.ShapeDtypeStruct((rows, d), BF16)],
        compiler_params=_params(("parallel",)),
        name="merge",
    )(om, og, od, gates, xs, mod, wbm, wbg, wbd, wo, lng, lnb)


def _router_kernel(x_ref, mod_ref, wr_ref, br_ref, o_ref, *, tm, s_lat):
    row0 = pl.program_id(0) * tm
    h2 = x_ref[...] * (1.0 + _row_mod(mod_ref, 4, row0, tm, s_lat)) + _row_mod(mod_ref, 3, row0, tm, s_lat)
    logits = lax.dot_general(wr_ref[...], h2, (((1,), (1,)), ((), ())),
                             precision=HIGHEST, preferred_element_type=F32)
    scores = jax.nn.sigmoid(logits)
    biased = scores + br_ref[...]
    sc = [scores[e:e + 1, :] for e in range(N_EXPERTS)]
    bi = [biased[e:e + 1, :] for e in range(N_EXPERTS)]
    gs = []
    for g in range(N_GROUPS):
        a, b, c, dd = bi[4 * g:4 * g + 4]
        hi1, lo1, hi2, lo2 = jnp.maximum(a, b), jnp.minimum(a, b), jnp.maximum(c, dd), jnp.minimum(c, dd)
        gs.append(jnp.maximum(hi1, hi2) + jnp.maximum(jnp.minimum(hi1, hi2), jnp.maximum(lo1, lo2)))
    best = jnp.maximum(jnp.maximum(gs[0], gs[1]), jnp.maximum(gs[2], gs[3]))
    gsel = jnp.where(gs[0] == best, 0, jnp.where(gs[1] == best, 1, jnp.where(gs[2] == best, 2, 3)))
    pick = lambda vals, j: jnp.where(gsel == 0, vals[j], jnp.where(gsel == 1, vals[4 + j],
                                     jnp.where(gsel == 2, vals[8 + j], vals[12 + j])))
    xb = [pick(bi, j) for j in range(EXPERTS_PER_GROUP)]
    xs = [pick(sc, j) for j in range(EXPERTS_PER_GROUP)]
    w = []
    for i in range(EXPERTS_PER_GROUP):
        rank = jnp.zeros_like(gsel)
        for j in range(EXPERTS_PER_GROUP):
            if j == i:
                continue
            beats = (xb[j] >= xb[i]) if j < i else (xb[j] > xb[i])
            rank = rank + beats.astype(jnp.int32)
        w.append(jnp.where(rank < 2, xs[i], 0.0))
    inv = ROUTED_SCALE / (w[0] + w[1] + w[2] + w[3])
    for e in range(N_EXPERTS):
        o_ref[e:e + 1, :] = jnp.where(gsel == e // EXPERTS_PER_GROUP, w[e % EXPERTS_PER_GROUP] * inv, 0.0)


def _router(xs, mod, wr_t, br, rows, s_lat):
    d = D_MODEL
    tm = _tile(rows, (640, 512, 256))
    return pl.pallas_call(
        functools.partial(_router_kernel, tm=tm, s_lat=s_lat),
        grid=(rows // tm,),
        in_specs=[pl.BlockSpec((tm, d), lambda i: (i, 0)), pl.BlockSpec(mod.shape, lambda i: (0, 0)),
                  pl.BlockSpec(wr_t.shape, lambda i: (0, 0)), pl.BlockSpec(br.shape, lambda i: (0, 0))],
        out_specs=pl.BlockSpec((N_EXPERTS, tm), lambda i: (0, i)),
        out_shape=jax.ShapeDtypeStruct((N_EXPERTS, rows), F32),
        compiler_params=_params(("parallel",)),
        name="router",
    )(xs, mod, wr_t, br)


def _moe_kernel(h_ref, cmb_ref, wg_ref, wu_ref, wd_ref, x_ref, mod_ref, lng_ref, lnb_ref, o_ref, acc_ref,
                *, tm, s_lat, n_slots):
    e = pl.program_id(1)
    h = h_ref[...]
    a = jnp.dot(h, wg_ref[0], preferred_element_type=F32)
    b = jnp.dot(h, wu_ref[0], preferred_element_type=F32)
    t = (a * jax.nn.sigmoid(a) * b).astype(BF16)
    y = jnp.dot(t, wd_ref[0], preferred_element_type=F32)
    lane = lax.broadcasted_iota(jnp.int32, cmb_ref.shape, 1)
    c = jnp.sum(jnp.where(lane == e, cmb_ref[...], 0.0), axis=1, keepdims=True)

    @pl.when(e == 0)
    def _():
        acc_ref[...] = c * y

    @pl.when(e > 0)
    def _():
        acc_ref[...] += c * y

    @pl.when(e == n_slots - 1)
    def _():
        row0 = pl.program_id(0) * tm
        z = DEEPNORM_ALPHA * x_ref[...] + _row_mod(mod_ref, 5, row0, tm, s_lat) * acc_ref[...]
        o_ref[...] = _layer_norm(z, lng_ref[...], lnb_ref[...])


def _moe(h2, cmb, wg, wu, wd, xs, mod, lng, lnb, rows, s_lat):
    d = D_MODEL
    n_slots, _, f = wg.shape
    tm = _tile(rows, (640, 512, 256))
    row = lambda i, e: (i, 0)
    full = lambda i, e: (0, 0)
    return pl.pallas_call(
        functools.partial(_moe_kernel, tm=tm, s_lat=s_lat, n_slots=n_slots),
        grid=(rows // tm, n_slots),
        in_specs=[pl.BlockSpec((tm, d), row), pl.BlockSpec((tm, cmb.shape[1]), row),
                  pl.BlockSpec((1, d, f), lambda i, e: (e, 0, 0)), pl.BlockSpec((1, d, f), lambda i, e: (e, 0, 0)),
                  pl.BlockSpec((1, f, d), lambda i, e: (e, 0, 0)),
                  pl.BlockSpec((tm, d), row), pl.BlockSpec(mod.shape, full),
                  pl.BlockSpec(lng.shape, full), pl.BlockSpec(lnb.shape, full)],
        out_specs=pl.BlockSpec((tm, d), row),
        out_shape=jax.ShapeDtypeStruct((rows, d), F32),
        scratch_shapes=[pltpu.VMEM((tm, d), F32)],
        compiler_params=_params(("parallel", "arbitrary")),
        name="moe",
    )(h2, cmb, wg, wu, wd, xs, mod, lng, lnb)


def _rope_tables(s_lat, n_ctx, dim):
    quarter = dim // 4
    rows = s_lat // GRID_W
    row = jnp.repeat(jnp.arange(rows, dtype=F32), GRID_W)
    col = jnp.tile(jnp.arange(GRID_W, dtype=F32), rows)
    inv_freq = ROPE_THETA ** (-jnp.arange(quarter, dtype=F32) / quarter)
    ar, ac = row[:, None] * inv_freq, col[:, None] * inv_freq
    zero = jnp.zeros_like(ar)
    pad = jnp.zeros((s_lat, LANE - dim), F32)
    cos = jnp.concatenate([jnp.cos(ar), jnp.cos(ar), jnp.cos(ac), jnp.cos(ac), pad + 1.0], axis=1)
    sa = jnp.concatenate([-jnp.sin(ar), zero, -jnp.sin(ac), zero, pad], axis=1)
    sb = jnp.concatenate([zero, jnp.sin(ar), zero, jnp.sin(ac), pad], axis=1)
    ctx1 = jnp.ones((n_ctx, LANE), F32)
    ctx0 = jnp.zeros((n_ctx, LANE), F32)
    return (jnp.concatenate([cos, ctx1], axis=0), jnp.concatenate([sa, ctx0], axis=0),
            jnp.concatenate([sb, ctx0], axis=0))


def kernel(x, c, ctx, c_ctx, w_ada, b_ada, w_in, mla_q_norm, mla_w_uq, mla_kv_norm, mla_w_ukv, gqa_q_norm,
           gqa_k_norm, diff_lambda, diff_norm, w_br_mla, w_br_gqa, w_br_diff, w_o, ln1_g, ln1_b, w_router,
           b_router, moe_w_gate, moe_w_up, moe_w_down, shared_w_gate, shared_w_up, shared_w_down, ln2_g, ln2_b):
    B, s_lat, d = x.shape
    n_ctx = ctx.shape[1]
    assert B == 1 and d == D_MODEL
    s_all = s_lat + n_ctx

    xs = jnp.concatenate([x[0], ctx[0]], axis=0)
    cc = jnp.zeros((8, d), F32).at[0].set(c[0]).at[1].set(c_ctx)
    mod_all = _adaln(cc, w_ada, b_ada)

    tabs64 = _rope_tables(s_lat, n_ctx, MLA_ROPE)
    tabs128 = _rope_tables(s_lat, n_ctx, GQA_HEAD_DIM)
    wr_t = w_router.T
    br = b_router.reshape(N_EXPERTS, 1)
    row2 = lambda v: v.reshape(1, -1)

    o0 = MLA_Q_LORA + MLA_KV_LORA + MLA_ROPE
    o1 = o0 + GQA_HEADS * GQA_HEAD_DIM + 2 * GQA_KV_HEADS * GQA_HEAD_DIM
    o2 = o1 + 2 * (2 * DIFF_HEADS * DIFF_HEAD_DIM) + DIFF_HEADS * DIFF_V_DIM

    for l in range(DEPTH):
        last = l == DEPTH - 1
        mod = mod_all[l]
        w = w_in[l]
        wa = jnp.pad(w[:, :o0], ((0, 0), (0, LANE - MLA_ROPE))).astype(BF16)
        wb = w[:, o0:o1].astype(BF16)
        wc = w[:, o1:o2].astype(BF16)
        wd_gate = w[:, o2:].astype(BF16)
        wuq = jnp.pad(mla_w_uq[l].reshape(MLA_Q_LORA, MLA_HEADS, MLA_NOPE + MLA_ROPE),
                      ((0, 0), (0, 0), (0, MLA_QK_PAD - MLA_NOPE - MLA_ROPE))
                      ).reshape(MLA_Q_LORA, MLA_HEADS * MLA_QK_PAD).astype(BF16)
        ukv = mla_w_ukv[l].reshape(MLA_KV_LORA, MLA_HEADS, MLA_NOPE + MLA_V)
        wukv = jnp.concatenate([ukv[:, :, :MLA_NOPE].reshape(MLA_KV_LORA, -1),
                                ukv[:, :, MLA_NOPE:].reshape(MLA_KV_LORA, -1)], axis=1).astype(BF16)

        h, q_mla, k_mla, v_mla = _mla_proj(xs, mod, wa, row2(mla_q_norm[l]), row2(mla_kv_norm[l]),
                                           wuq, wukv, tabs64, s_lat)
        q_gqa, k_gqa, v_gqa = _gqa_proj(h, wb, row2(gqa_q_norm[l]), row2(gqa_k_norm[l]), tabs128)
        q_dif, k_dif, v_dif = _diff_proj(h, wc, tabs128)
        gates = _gate_proj(h, wd_gate)

        lam_init = 0.8 - 0.6 * math.exp(-0.3 * l)
        rows = s_lat if last else s_all
        dif_extra = (diff_lambda[l], row2(diff_norm[l]), lam_init)
        o_mla = _attention("mla", q_mla, k_mla, v_mla, s_lat, rows)
        o_gqa = _attention("gqa", q_gqa, k_gqa, v_gqa, s_lat, rows)
        o_dif = _attention("diff", q_dif, k_dif, v_dif, s_lat, rows, extra=dif_extra)
        if not last:
            o_mla = _attention("mla", q_mla, k_mla, v_mla, s_lat, rows, ctx_into=o_mla)
            o_gqa = _attention("gqa", q_gqa, k_gqa, v_gqa, s_lat, rows, ctx_into=o_gqa)
            o_dif = _attention("diff", q_dif, k_dif, v_dif, s_lat, rows, extra=dif_extra, ctx_into=o_dif)

        xs, h2 = _merge(o_mla, o_gqa, o_dif, gates, xs, mod, w_br_mla[l].astype(BF16), w_br_gqa[l].astype(BF16),
                        w_br_diff[l].astype(BF16), w_o[l].astype(BF16), row2(ln1_g[l]), row2(ln1_b[l]), rows, s_lat)

        cmb_t = _router(xs, mod, wr_t, br, rows, s_lat)
        cmb = jnp.concatenate([jnp.ones((rows, 1), F32), cmb_t.T], axis=1)
        wg = jnp.concatenate([shared_w_gate[l][None], moe_w_gate[l]], axis=0).astype(BF16)
        wu = jnp.concatenate([shared_w_up[l][None], moe_w_up[l]], axis=0).astype(BF16)
        wdn = jnp.concatenate([shared_w_down[l][None], moe_w_down[l]], axis=0).astype(BF16)
        xs = _moe(h2, cmb, wg, wu, wdn, xs, mod, row2(ln2_g[l]), row2(ln2_b[l]), rows, s_lat)

    return xs[None]
```

```python
import functools
import math

import jax
import jax.numpy as jnp
from jax import lax
from jax.experimental import pallas as pl
from jax.experimental.pallas import tpu as pltpu

F32 = jnp.float32
BF16 = jnp.bfloat16
HIGHEST = lax.Precision.HIGHEST

D_MODEL = 2048
DEPTH = 2
GRID_W = 64
ROPE_THETA = 10000.0
LN_EPS = 1e-5
RMS_EPS = 1e-6
MLA_HEADS = 8
MLA_Q_LORA = 512
MLA_KV_LORA = 512
MLA_NOPE = 128
MLA_ROPE = 64
MLA_V = 128
GQA_HEADS = 8
GQA_KV_HEADS = 2
GQA_HEAD_DIM = 128
DIFF_HEADS = 4
DIFF_HEAD_DIM = 128
DIFF_V_DIM = 2 * DIFF_HEAD_DIM
N_EXPERTS = 16
N_GROUPS = 4
EXPERTS_PER_GROUP = N_EXPERTS // N_GROUPS
MOE_D_FF = 512
ROUTED_SCALE = 1.0
DEEPNORM_ALPHA = (2 * DEPTH) ** 0.25

LANE = 128
MLA_QK_PAD = 2 * LANE
LOG2E = math.log2(math.e)
VMEM_LIMIT = 56 * 1024 * 1024


def _tile(rows, prefs):
    for t in prefs:
        if rows % t == 0:
            return t
    raise ValueError(f"no tile in {prefs} divides {rows}")


def _params(sem, vmem=VMEM_LIMIT, flags=None):
    return pltpu.CompilerParams(dimension_semantics=sem, vmem_limit_bytes=vmem, flags=flags)


def _row_mod(mod_ref, k, row0, tm, s_lat):
    d = D_MODEL
    lat = mod_ref[0:1, k * d:(k + 1) * d]
    ctx = mod_ref[1:2, k * d:(k + 1) * d]
    rows = row0 + lax.broadcasted_iota(jnp.int32, (tm, 1), 0)
    return jnp.where(rows >= s_lat, ctx, lat)


def _rms(x, g):
    return x * lax.rsqrt(jnp.mean(x * x, axis=-1, keepdims=True) + RMS_EPS) * g


def _rope(t, cos, sa, sb, quarter):
    return t * cos + pltpu.roll(t, LANE - quarter, 1) * sa + pltpu.roll(t, quarter, 1) * sb


def _adaln_kernel(c_ref, w_ref, b_ref, o_ref):
    a = c_ref[...]
    a = a * jax.nn.sigmoid(a)
    o_ref[0] = jnp.dot(a, w_ref[0], precision=HIGHEST, preferred_element_type=F32) + b_ref[0]


def _adaln(cc, w_ada, b_ada):
    L, d, n = w_ada.shape
    tn = 1024
    return pl.pallas_call(
        _adaln_kernel,
        grid=(L, n // tn),
        in_specs=[pl.BlockSpec((8, d), lambda l, j: (0, 0)),
                  pl.BlockSpec((1, d, tn), lambda l, j: (l, 0, j)),
                  pl.BlockSpec((1, 1, tn), lambda l, j: (l, 0, j))],
        out_specs=pl.BlockSpec((1, 8, tn), lambda l, j: (l, 0, j)),
        out_shape=jax.ShapeDtypeStruct((L, 8, n), F32),
        compiler_params=_params(("arbitrary", "arbitrary")),
        name="adaln",
    )(cc, w_ada, b_ada.reshape(L, 1, n))


def _mla_proj_kernel(x_ref, mod_ref, wa_ref, qn_ref, kvn_ref, wuq_ref, wukv_ref, cos_ref, sa_ref, sb_ref,
                     h_ref, q_ref, k_ref, v_ref, *, tm, s_lat, q_scale):
    row0 = pl.program_id(0) * tm
    sh = _row_mod(mod_ref, 0, row0, tm, s_lat)
    sc = _row_mod(mod_ref, 1, row0, tm, s_lat)
    h = (x_ref[...] * (1.0 + sc) + sh).astype(BF16)
    h_ref[...] = h
    a = jnp.dot(h, wa_ref[...], preferred_element_type=F32)
    cqn = _rms(a[:, :MLA_Q_LORA], qn_ref[...]).astype(BF16)
    ckvn = _rms(a[:, MLA_Q_LORA:MLA_Q_LORA + MLA_KV_LORA], kvn_ref[...]).astype(BF16)
    kr = a[:, MLA_Q_LORA + MLA_KV_LORA:]
    q = jnp.dot(cqn, wuq_ref[...], preferred_element_type=F32)
    kv = jnp.dot(ckvn, wukv_ref[...], preferred_element_type=F32)
    cos, sa, sb = cos_ref[...], sa_ref[...], sb_ref[...]
    quarter = MLA_ROPE // 4
    kr2 = _rope(kr, cos, sa, sb, quarter).astype(BF16)
    for hh in range(MLA_HEADS):
        c0 = hh * MLA_QK_PAD
        q_ref[:, c0:c0 + LANE] = (q[:, c0:c0 + LANE] * q_scale).astype(BF16)
        q_ref[:, c0 + LANE:c0 + 2 * LANE] = (_rope(q[:, c0 + LANE:c0 + 2 * LANE], cos, sa, sb, quarter)
                                             * q_scale).astype(BF16)
        k_ref[:, c0:c0 + LANE] = kv[:, hh * LANE:(hh + 1) * LANE].astype(BF16)
        k_ref[:, c0 + LANE:c0 + 2 * LANE] = kr2
    v_ref[...] = kv[:, MLA_HEADS * MLA_NOPE:].astype(BF16)


def _mla_proj(xs, mod, wa, qn, kvn, wuq, wukv, tabs, s_lat):
    rows, d = xs.shape
    tm = _tile(rows, (640, 512, 256))
    cos, sa, sb = tabs
    row = lambda i: (i, 0)
    full = lambda i: (0, 0)
    qk_w = MLA_HEADS * MLA_QK_PAD
    v_w = MLA_HEADS * MLA_V
    q_scale = (MLA_NOPE + MLA_ROPE) ** -0.5 * LOG2E
    return pl.pallas_call(
        functools.partial(_mla_proj_kernel, tm=tm, s_lat=s_lat, q_scale=q_scale),
        grid=(rows // tm,),
        in_specs=[pl.BlockSpec((tm, d), row), pl.BlockSpec(mod.shape, full),
                  pl.BlockSpec(wa.shape, full), pl.BlockSpec(qn.shape, full), pl.BlockSpec(kvn.shape, full),
                  pl.BlockSpec(wuq.shape, full), pl.BlockSpec(wukv.shape, full),
                  pl.BlockSpec((tm, LANE), row), pl.BlockSpec((tm, LANE), row), pl.BlockSpec((tm, LANE), row)],
        out_specs=[pl.BlockSpec((tm, d), row), pl.BlockSpec((tm, qk_w), row),
                   pl.BlockSpec((tm, qk_w), row), pl.BlockSpec((tm, v_w), row)],
        out_shape=[jax.ShapeDtypeStruct((rows, d), BF16), jax.ShapeDtypeStruct((rows, qk_w), BF16),
                   jax.ShapeDtypeStruct((rows, qk_w), BF16), jax.ShapeDtypeStruct((rows, v_w), BF16)],
        compiler_params=_params(("parallel",)),
        name="mla_proj",
    )(xs, mod, wa, qn, kvn, wuq, wukv, cos, sa, sb)


def _gqa_proj_kernel(h_ref, w_ref, qn_ref, kn_ref, cos_ref, sa_ref, sb_ref, q_ref, k_ref, v_ref, *, q_scale):
    a = jnp.dot(h_ref[...], w_ref[...], preferred_element_type=F32)
    cos, sa, sb = cos_ref[...], sa_ref[...], sb_ref[...]
    quarter = GQA_HEAD_DIM // 4
    for hh in range(GQA_HEADS):
        x = _rms(a[:, hh * LANE:(hh + 1) * LANE], qn_ref[...])
        q_ref[:, hh * LANE:(hh + 1) * LANE] = (_rope(x, cos, sa, sb, quarter) * q_scale).astype(BF16)
    k0 = GQA_HEADS * GQA_HEAD_DIM
    for hh in range(GQA_KV_HEADS):
        x = _rms(a[:, k0 + hh * LANE:k0 + (hh + 1) * LANE], kn_ref[...])
        k_ref[:, hh * LANE:(hh + 1) * LANE] = _rope(x, cos, sa, sb, quarter).astype(BF16)
    v_ref[...] = a[:, k0 + GQA_KV_HEADS * GQA_HEAD_DIM:].astype(BF16)


def _gqa_proj(h, w, qn, kn, tabs):
    rows, d = h.shape
    tm = _tile(rows, (640, 512, 256))
    cos, sa, sb = tabs
    row = lambda i: (i, 0)
    full = lambda i: (0, 0)
    qw = GQA_HEADS * GQA_HEAD_DIM
    kw = GQA_KV_HEADS * GQA_HEAD_DIM
    return pl.pallas_call(
        functools.partial(_gqa_proj_kernel, q_scale=GQA_HEAD_DIM ** -0.5 * LOG2E),
        grid=(rows // tm,),
        in_specs=[pl.BlockSpec((tm, d), row), pl.BlockSpec(w.shape, full),
                  pl.BlockSpec(qn.shape, full), pl.BlockSpec(kn.shape, full),
                  pl.BlockSpec((tm, LANE), row), pl.BlockSpec((tm, LANE), row), pl.BlockSpec((tm, LANE), row)],
        out_specs=[pl.BlockSpec((tm, qw), row), pl.BlockSpec((tm, kw), row), pl.BlockSpec((tm, kw), row)],
        out_shape=[jax.ShapeDtypeStruct((rows, qw), BF16), jax.ShapeDtypeStruct((rows, kw), BF16),
                   jax.ShapeDtypeStruct((rows, kw), BF16)],
        compiler_params=_params(("parallel",)),
        name="gqa_proj",
    )(h, w, qn, kn, cos, sa, sb)


def _diff_proj_kernel(h_ref, w_ref, cos_ref, sa_ref, sb_ref, q_ref, k_ref, v_ref, *, q_scale):
    a = jnp.dot(h_ref[...], w_ref[...], preferred_element_type=F32)
    cos, sa, sb = cos_ref[...], sa_ref[...], sb_ref[...]
    quarter = DIFF_HEAD_DIM // 4
    n = 2 * DIFF_HEADS
    for hh in range(n):
        q_ref[:, hh * LANE:(hh + 1) * LANE] = (_rope(a[:, hh * LANE:(hh + 1) * LANE], cos, sa, sb, quarter)
                                               * q_scale).astype(BF16)
        k_ref[:, hh * LANE:(hh + 1) * LANE] = _rope(a[:, (n + hh) * LANE:(n + hh + 1) * LANE],
                                                    cos, sa, sb, quarter).astype(BF16)
    v_ref[...] = a[:, 2 * n * LANE:].astype(BF16)


def _diff_proj(h, w, tabs):
    rows, d = h.shape
    tm = _tile(rows, (640, 512, 256))
    cos, sa, sb = tabs
    row = lambda i: (i, 0)
    full = lambda i: (0, 0)
    ww = 2 * DIFF_HEADS * DIFF_HEAD_DIM
    vw = DIFF_HEADS * DIFF_V_DIM
    return pl.pallas_call(
        functools.partial(_diff_proj_kernel, q_scale=DIFF_HEAD_DIM ** -0.5 * LOG2E),
        grid=(rows // tm,),
        in_specs=[pl.BlockSpec((tm, d), row), pl.BlockSpec(w.shape, full),
                  pl.BlockSpec((tm, LANE), row), pl.BlockSpec((tm, LANE), row), pl.BlockSpec((tm, LANE), row)],
        out_specs=[pl.BlockSpec((tm, ww), row), pl.BlockSpec((tm, ww), row), pl.BlockSpec((tm, vw), row)],
        out_shape=[jax.ShapeDtypeStruct((rows, ww), BF16), jax.ShapeDtypeStruct((rows, ww), BF16),
                   jax.ShapeDtypeStruct((rows, vw), BF16)],
        compiler_params=_params(("parallel",)),
        name="diff_proj",
    )(h, w, cos, sa, sb)


def _gate_proj_kernel(h_ref, w_ref, o_ref):
    a = jnp.dot(h_ref[...], w_ref[...], preferred_element_type=F32)
    o_ref[...] = jax.nn.sigmoid(a).astype(BF16)


def _gate_proj(h, w):
    rows, d = h.shape
    n = w.shape[1]
    tm = _tile(rows, (640, 512, 256))
    tn = 1536
    return pl.pallas_call(
        _gate_proj_kernel,
        grid=(rows // tm, n // tn),
        in_specs=[pl.BlockSpec((tm, d), lambda i, j: (i, 0)), pl.BlockSpec((d, tn), lambda i, j: (0, j))],
        out_specs=pl.BlockSpec((tm, tn), lambda i, j: (i, j)),
        out_shape=jax.ShapeDtypeStruct((rows, n), BF16),
        compiler_params=_params(("parallel", "arbitrary")),
        name="gate_proj",
    )(h, w)


def _rows(start, size):
    return pl.ds(start if isinstance(start, int) else pl.multiple_of(start, size), size)


def _dot_nt(a, b):
    return lax.dot_general(a, b, (((1,), (1,)), ((), ())), preferred_element_type=F32)


def _flash(score, v_ref, n_sub, n_chunks, bkv, s_ref, mx_ref, m_ref, l_ref, acc_ref):
    m_ref[...] = jnp.full(m_ref.shape, -jnp.inf, F32)
    l_ref[...] = jnp.zeros(l_ref.shape, F32)
    acc_ref[...] = jnp.zeros(acc_ref.shape, F32)

    def item(k):
        if isinstance(k, int):
            return k // n_chunks, k % n_chunks
        t = lax.div(k, n_chunks)
        return t, k - t * n_chunks

    def issue(k, slot):
        t, c = item(k)
        score(t, c, slot)

    def absorb(k, slot):
        t, c = item(k)
        m_prev = m_ref[t]
        m_new = jnp.maximum(m_prev, mx_ref[slot])
        alpha = jnp.exp2(m_prev - m_new)
        p = jnp.exp2(s_ref[slot] - m_new)
        l_ref[t] = alpha * l_ref[t] + jnp.sum(p, axis=1, keepdims=True)
        acc_ref[t] = alpha * acc_ref[t] + jnp.dot(p.astype(BF16), v_ref[_rows(c * bkv, bkv), :],
                                                  preferred_element_type=F32)
        m_ref[t] = m_new

    n_items = n_sub * n_chunks
    issue(0, 0)

    def body(j, carry):
        k = 2 * j
        issue(k + 1, 1)
        absorb(k, 0)
        issue(k + 2, 0)
        absorb(k + 1, 1)
        return carry

    n_pairs = (n_items - 1) // 2
    if n_pairs:
        lax.fori_loop(0, n_pairs, body, 0)
    k = 2 * n_pairs
    if n_items % 2 == 1:
        absorb(k, 0)
    else:
        issue(k + 1, 1)
        absorb(k, 0)
        absorb(k + 1, 1)


def _score_into(s_ref, mx_ref, slot, row0, q, kc):
    s = _dot_nt(q, kc)
    rows = q.shape[0]
    s_ref[slot, row0:row0 + rows] = s
    mx_ref[slot, row0:row0 + rows] = jnp.max(s, axis=1, keepdims=True)


def _mla_attn_kernel(q_ref, k_ref, v_ref, *rest, n_sub, n_chunks, bkv):
    o_ref, s_ref, mx_ref, m_ref, l_ref, acc_ref = rest[-6:]
    m_rows = s_ref.shape[1]

    def score(t, c, slot):
        _score_into(s_ref, mx_ref, slot, 0, q_ref[_rows(t * m_rows, m_rows), :], k_ref[_rows(c * bkv, bkv), :])

    _flash(score, v_ref, n_sub, n_chunks, bkv, s_ref, mx_ref, m_ref, l_ref, acc_ref)
    for t in range(n_sub):
        o_ref[t * m_rows:(t + 1) * m_rows, :] = (acc_ref[t] / l_ref[t]).astype(BF16)


def _gqa_attn_kernel(q_ref, k_ref, v_ref, *rest, n_sub, n_chunks, bkv):
    o_ref, qs_ref, s_ref, mx_ref, m_ref, l_ref, acc_ref = rest[-7:]
    g = GQA_HEADS // GQA_KV_HEADS
    bqs = s_ref.shape[1] // g
    for t in range(n_sub):
        for j in range(g):
            qs_ref[t, j * bqs:(j + 1) * bqs] = q_ref[t * bqs:(t + 1) * bqs, j * LANE:(j + 1) * LANE]

    def score(t, c, slot):
        _score_into(s_ref, mx_ref, slot, 0, qs_ref[t], k_ref[_rows(c * bkv, bkv), :])

    _flash(score, v_ref, n_sub, n_chunks, bkv, s_ref, mx_ref, m_ref, l_ref, acc_ref)
    for t in range(n_sub):
        o = acc_ref[t] / l_ref[t]
        for j in range(g):
            o_ref[t * bqs:(t + 1) * bqs, j * LANE:(j + 1) * LANE] = o[j * bqs:(j + 1) * bqs].astype(BF16)


def _diff_attn_kernel(q_ref, k_ref, v_ref, lam_ref, dn_ref, *rest, n_sub, n_chunks, bkv, lam_init):
    o_ref, s_ref, mx_ref, m_ref, l_ref, acc_ref = rest[-6:]
    d = DIFF_HEAD_DIM
    bqs = s_ref.shape[1] // 2
    lp = lam_ref[...]
    lam = (jnp.exp(jnp.sum(lp[0:1] * lp[1:2], axis=1, keepdims=True))
           - jnp.exp(jnp.sum(lp[2:3] * lp[3:4], axis=1, keepdims=True)) + lam_init)

    def score(t, c, slot):
        qr, kr = _rows(t * bqs, bqs), _rows(c * bkv, bkv)
        _score_into(s_ref, mx_ref, slot, 0, q_ref[qr, 0:d], k_ref[kr, 0:d])
        _score_into(s_ref, mx_ref, slot, bqs, q_ref[qr, d:2 * d], k_ref[kr, d:2 * d])

    _flash(score, v_ref, n_sub, n_chunks, bkv, s_ref, mx_ref, m_ref, l_ref, acc_ref)
    for t in range(n_sub):
        o = acc_ref[t] / l_ref[t]
        o = o[:bqs] - lam * o[bqs:]
        o_ref[t * bqs:(t + 1) * bqs, :] = (_rms(o, dn_ref[...]) * (1.0 - lam_init)).astype(BF16)


def _attention(kind, q, k, v, s_lat, out_rows, extra=(), ctx=False, into=None):
    s_all = q.shape[0]
    n_ctx = s_all - s_lat
    assert s_lat % n_ctx == 0
    if kind == "mla":
        heads, qw, kw, vw, ow, stack = MLA_HEADS, MLA_QK_PAD, MLA_QK_PAD, MLA_V, MLA_V, 1
        bqs = n_ctx if ctx else _tile(s_lat, (1024, 512, 256))
        body = _mla_attn_kernel
    elif kind == "gqa":
        g = GQA_HEADS // GQA_KV_HEADS
        heads, qw, kw, vw, ow, stack = GQA_KV_HEADS, g * LANE, LANE, LANE, g * LANE, g
        bqs = n_ctx if ctx else 256
        body = _gqa_attn_kernel
    else:
        heads, qw, kw, vw, ow, stack = DIFF_HEADS, 2 * LANE, 2 * LANE, DIFF_V_DIM, DIFF_V_DIM, 2
        bqs = n_ctx if ctx else _tile(s_lat, (512, 256))
        body = functools.partial(_diff_attn_kernel, lam_init=extra[2])
        extra = extra[:2]
    m_rows = stack * bqs
    n_sub = 1 if ctx else _tile(s_lat // bqs, (4, 3, 2, 1))
    bq = n_sub * bqs
    if ctx:
        kv_rows, bkv, row_blk, nq = n_ctx, n_ctx, s_lat // n_ctx, 1
    else:
        kv_rows, bkv, row_blk, nq = s_all, _tile(s_all, (1280, 1024, 512, 256)), 0, s_lat // bq
    body = functools.partial(body, n_sub=n_sub, n_chunks=kv_rows // bkv, bkv=bkv)
    kv_blk = row_blk
    in_specs = [pl.BlockSpec((bq, qw), lambda h, i: (row_blk + i, h)),
                pl.BlockSpec((kv_rows, kw), lambda h, i: (kv_blk, h), pipeline_mode=pl.Buffered(1)),
                pl.BlockSpec((kv_rows, vw), lambda h, i: (kv_blk, h), pipeline_mode=pl.Buffered(1))]
    in_specs += [pl.BlockSpec(e.shape, lambda h, i: (0, 0)) for e in extra]
    args = [q, k, v, *extra]
    aliases = {}
    if into is not None:
        in_specs.append(pl.BlockSpec(memory_space=pl.ANY))
        aliases = {len(args): 0}
        args.append(into)
    scratch = []
    if kind == "gqa":
        scratch.append(pltpu.VMEM((n_sub, m_rows, LANE), BF16))
    scratch += [pltpu.VMEM((2, m_rows, bkv), F32),
                pltpu.VMEM((2, m_rows, 1), F32), pltpu.VMEM((n_sub, m_rows, 1), F32),
                pltpu.VMEM((n_sub, m_rows, 1), F32), pltpu.VMEM((n_sub, m_rows, vw), F32)]
    return pl.pallas_call(
        body,
        grid=(heads, nq),
        in_specs=in_specs,
        out_specs=pl.BlockSpec((bq, ow), lambda h, i: (row_blk + i, h)),
        out_shape=jax.ShapeDtypeStruct((out_rows, heads * ow), BF16),
        scratch_shapes=scratch,
        input_output_aliases=aliases,
        compiler_params=_params(("parallel", "arbitrary")),
        name=kind + ("_attn_ctx" if ctx else "_attn"),
    )(*args)


def _layer_norm(z, g, b):
    mu = jnp.mean(z, axis=-1, keepdims=True)
    zc = z - mu
    var = jnp.mean(zc * zc, axis=-1, keepdims=True)
    return zc * lax.rsqrt(var + LN_EPS) * g + b


def _merge_kernel(om_ref, og_ref, od_ref, g_ref, x_ref, mod_ref, wbm_ref, wbg_ref, wbd_ref, wo_ref,
                  lng_ref, lnb_ref, xo_ref, h2_ref, *, tm, s_lat):
    d = D_MODEL
    row0 = pl.program_id(0) * tm
    merged = g_ref[:, 0:d].astype(F32) * jnp.dot(om_ref[...], wbm_ref[...], preferred_element_type=F32)
    merged += g_ref[:, d:2 * d].astype(F32) * jnp.dot(og_ref[...], wbg_ref[...], preferred_element_type=F32)
    merged += g_ref[:, 2 * d:3 * d].astype(F32) * jnp.dot(od_ref[...], wbd_ref[...], preferred_element_type=F32)
    y = jnp.dot(merged.astype(BF16), wo_ref[...], preferred_element_type=F32)
    z = DEEPNORM_ALPHA * x_ref[...] + _row_mod(mod_ref, 2, row0, tm, s_lat) * y
    xn = _layer_norm(z, lng_ref[...], lnb_ref[...])
    xo_ref[...] = xn
    h2 = xn * (1.0 + _row_mod(mod_ref, 4, row0, tm, s_lat)) + _row_mod(mod_ref, 3, row0, tm, s_lat)
    h2_ref[...] = _pack_pairs(h2)


def _merge(om, og, od, gates, xs, mod, wbm, wbg, wbd, wo, lng, lnb, rows, s_lat):
    d = D_MODEL
    tm = 256
    row = lambda i: (i, 0)
    full = lambda i: (0, 0)
    resident = lambda a: pl.BlockSpec(a.shape, full, pipeline_mode=pl.Buffered(1))
    return pl.pallas_call(
        functools.partial(_merge_kernel, tm=tm, s_lat=s_lat),
        grid=(rows // tm,),
        in_specs=[pl.BlockSpec((tm, om.shape[1]), row), pl.BlockSpec((tm, og.shape[1]), row),
                  pl.BlockSpec((tm, od.shape[1]), row), pl.BlockSpec((tm, gates.shape[1]), row),
                  pl.BlockSpec((tm, d), row), pl.BlockSpec(mod.shape, full),
                  resident(wbm), resident(wbg), resident(wbd), resident(wo),
                  pl.BlockSpec(lng.shape, full), pl.BlockSpec(lnb.shape, full)],
        out_specs=[pl.BlockSpec((tm, d), row), pl.BlockSpec((tm, d // 2), row)],
        out_shape=[jax.ShapeDtypeStruct((rows, d), F32), jax.ShapeDtypeStruct((rows, d // 2), jnp.uint32)],
        compiler_params=_params(("parallel",)),
        name="merge",
    )(om, og, od, gates, xs, mod, wbm, wbg, wbd, wo, lng, lnb)


def _router_kernel(x_ref, mod_ref, wr_ref, br_ref, ri_ref, rw_ref, cnt_ref, run_ref, *, tm, s_lat):
    @pl.when(pl.program_id(0) == 0)
    def _():
        run_ref[...] = jnp.zeros(run_ref.shape, F32)

    row0 = pl.program_id(0) * tm
    h2 = x_ref[...] * (1.0 + _row_mod(mod_ref, 4, row0, tm, s_lat)) + _row_mod(mod_ref, 3, row0, tm, s_lat)
    logits = lax.dot_general(wr_ref[...], h2, (((1,), (1,)), ((), ())),
                             precision=HIGHEST, preferred_element_type=F32)
    scores = jax.nn.sigmoid(logits)
    biased = scores + br_ref[...]
    sc = [scores[e:e + 1, :] for e in range(N_EXPERTS)]
    bi = [biased[e:e + 1, :] for e in range(N_EXPERTS)]
    gs = []
    for g in range(N_GROUPS):
        a, b, c, dd = bi[4 * g:4 * g + 4]
        hi1, lo1, hi2, lo2 = jnp.maximum(a, b), jnp.minimum(a, b), jnp.maximum(c, dd), jnp.minimum(c, dd)
        gs.append(jnp.maximum(hi1, hi2) + jnp.maximum(jnp.minimum(hi1, hi2), jnp.maximum(lo1, lo2)))
    best = jnp.maximum(jnp.maximum(gs[0], gs[1]), jnp.maximum(gs[2], gs[3]))
    gsel = jnp.where(gs[0] == best, 0, jnp.where(gs[1] == best, 1, jnp.where(gs[2] == best, 2, 3)))
    pick = lambda vals, j: jnp.where(gsel == 0, vals[j], jnp.where(gsel == 1, vals[4 + j],
                                     jnp.where(gsel == 2, vals[8 + j], vals[12 + j])))
    xb = [pick(bi, j) for j in range(EXPERTS_PER_GROUP)]
    xs = [pick(sc, j) for j in range(EXPERTS_PER_GROUP)]
    sel, w = [], []
    for i in range(EXPERTS_PER_GROUP):
        rank = jnp.zeros_like(gsel)
        for j in range(EXPERTS_PER_GROUP):
            if j == i:
                continue
            beats = (xb[j] >= xb[i]) if j < i else (xb[j] > xb[i])
            rank = rank + beats.astype(jnp.int32)
        sel.append(rank < 2)
        w.append(jnp.where(rank < 2, xs[i], 0.0))
    inv = ROUTED_SCALE / (w[0] + w[1] + w[2] + w[3])
    ia = jnp.where(sel[0], 0, jnp.where(sel[1], 1, jnp.where(sel[2], 2, 3)))
    ib = jnp.where(sel[3], 3, jnp.where(sel[2], 2, jnp.where(sel[1], 1, 0)))
    local = lambda idx: jnp.where(idx == 0, w[0], jnp.where(idx == 1, w[1], jnp.where(idx == 2, w[2], w[3])))
    ea = gsel * EXPERTS_PER_GROUP + ia
    eb = gsel * EXPERTS_PER_GROUP + ib
    eidx = lax.broadcasted_iota(jnp.int32, (N_EXPERTS, tm), 0)
    is_a, is_b = eidx == ea, eidx == eb
    onehot = jnp.where(is_a | is_b, 1.0, 0.0)
    earlier = jnp.where(lax.broadcasted_iota(jnp.int32, (tm, tm), 0) < lax.broadcasted_iota(jnp.int32, (tm, tm), 1),
                        1.0, 0.0).astype(BF16)
    pos = run_ref[...] + jnp.dot(onehot.astype(BF16), earlier, preferred_element_type=F32)
    ri_ref[0:1, :] = ea
    ri_ref[1:2, :] = eb
    ri_ref[2:3, :] = jnp.sum(jnp.where(is_a, pos, 0.0), axis=0, keepdims=True).astype(jnp.int32)
    ri_ref[3:4, :] = jnp.sum(jnp.where(is_b, pos, 0.0), axis=0, keepdims=True).astype(jnp.int32)
    rw_ref[0:1, :] = local(ia) * inv
    rw_ref[1:2, :] = local(ib) * inv
    run = run_ref[...] + jnp.sum(onehot, axis=1, keepdims=True)
    run_ref[...] = run
    cnt_ref[...] = jnp.broadcast_to(run, cnt_ref.shape)


def _router(xs, mod, wr_t, br, rows, s_lat):
    d = D_MODEL
    tm = _tile(rows, (640, 512, 256))
    return pl.pallas_call(
        functools.partial(_router_kernel, tm=tm, s_lat=s_lat),
        grid=(rows // tm,),
        in_specs=[pl.BlockSpec((tm, d), lambda i: (i, 0)), pl.BlockSpec(mod.shape, lambda i: (0, 0)),
                  pl.BlockSpec(wr_t.shape, lambda i: (0, 0)), pl.BlockSpec(br.shape, lambda i: (0, 0))],
        out_specs=[pl.BlockSpec((4, tm), lambda i: (0, i)), pl.BlockSpec((2, tm), lambda i: (0, i)),
                   pl.BlockSpec((N_EXPERTS, LANE), lambda i: (0, 0))],
        out_shape=[jax.ShapeDtypeStruct((4, rows), jnp.int32), jax.ShapeDtypeStruct((2, rows), F32),
                   jax.ShapeDtypeStruct((N_EXPERTS, LANE), F32)],
        scratch_shapes=[pltpu.VMEM((N_EXPERTS, 1), F32)],
        compiler_params=_params(("arbitrary",)),
        name="router",
    )(xs, mod, wr_t, br)


MOE_TILE = 512
HALF = D_MODEL // 2


def _pack_pairs(h):
    hi = pltpu.bitcast(h[:, :HALF].astype(BF16).astype(F32), jnp.uint32)
    lo = pltpu.bitcast(h[:, HALF:].astype(BF16).astype(F32), jnp.uint32)
    return (hi & jnp.uint32(0xFFFF0000)) | (lo >> jnp.uint32(16))


def _unpack_pairs(w):
    hi = pltpu.bitcast(w & jnp.uint32(0xFFFF0000), F32)
    lo = pltpu.bitcast(w << jnp.uint32(16), F32)
    return jnp.concatenate([hi, lo], axis=1).astype(BF16)


def _swiglu(x, wg, wu, wd):
    a = jnp.dot(x, wg, preferred_element_type=F32)
    b = jnp.dot(x, wu, preferred_element_type=F32)
    return jnp.dot((a * jax.nn.sigmoid(a) * b).astype(BF16), wd, preferred_element_type=F32)


def _sort_copy(dest_ref, h_ref, o_ref, sem, t, j):
    return pltpu.make_async_copy(h_ref.at[pl.ds(t, 1)], o_ref.at[pl.ds(dest_ref[j, t], 1)], sem)


def _moe_sort_kernel(dest_ref, h_ref, init_ref, o_ref, sem, *, tg):
    del init_ref
    base = pl.program_id(0) * tg

    def start(r, carry):
        for j in range(2):
            _sort_copy(dest_ref, h_ref, o_ref, sem, base + r, j).start()
        return carry

    def wait(r, carry):
        for j in range(2):
            _sort_copy(dest_ref, h_ref, o_ref, sem, base + r, j).wait()
        return carry

    lax.fori_loop(0, tg, start, 0)
    lax.fori_loop(0, tg, wait, 0)


def _moe_sort(dest, h2p, n_sorted):
    rows, half = h2p.shape
    tg = _tile(rows, (1280, 1024, 512, 256))
    return pl.pallas_call(
        functools.partial(_moe_sort_kernel, tg=tg),
        grid_spec=pltpu.PrefetchScalarGridSpec(
            num_scalar_prefetch=1, grid=(rows // tg,),
            in_specs=[pl.BlockSpec(memory_space=pl.ANY), pl.BlockSpec(memory_space=pl.ANY)],
            out_specs=pl.BlockSpec(memory_space=pl.ANY),
            scratch_shapes=[pltpu.SemaphoreType.DMA]),
        out_shape=jax.ShapeDtypeStruct((n_sorted, half), jnp.uint32),
        input_output_aliases={2: 0},
        compiler_params=_params(("arbitrary",)),
        name="moe_sort",
    )(dest, h2p, jnp.zeros((n_sorted, half), jnp.uint32))


def _moe_expert_kernel(te_ref, na_ref, x_ref, wg_ref, wu_ref, wd_ref, y_ref):
    del te_ref
    live = pl.program_id(0) < na_ref[0]

    @pl.when(live)
    def _():
        y_ref[...] = _swiglu(_unpack_pairs(x_ref[...]), wg_ref[0].astype(BF16), wu_ref[0].astype(BF16),
                             wd_ref[0].astype(BF16))

    @pl.when(jnp.logical_not(live))
    def _():
        y_ref[...] = jnp.zeros(y_ref.shape, F32)


def _moe_experts(tile_expert, n_active, xsorted, wg, wu, wd):
    n_sorted, half = xsorted.shape
    _, d, f = wg.shape
    tme = MOE_TILE
    wsel = lambda i, te, na: (te[i], 0, 0)
    return pl.pallas_call(
        _moe_expert_kernel,
        grid_spec=pltpu.PrefetchScalarGridSpec(
            num_scalar_prefetch=2, grid=(n_sorted // tme,),
            in_specs=[pl.BlockSpec((tme, half), lambda i, te, na: (i, 0)),
                      pl.BlockSpec((1, d, f), wsel), pl.BlockSpec((1, d, f), wsel), pl.BlockSpec((1, f, d), wsel)],
            out_specs=pl.BlockSpec((tme, d), lambda i, te, na: (i, 0))),
        out_shape=jax.ShapeDtypeStruct((n_sorted, d), F32),
        compiler_params=_params(("arbitrary",)),
        name="moe_experts",
    )(tile_expert, n_active, xsorted, wg, wu, wd)


def _moe_final_kernel(dest_ref, hp_ref, rw_ref, y_ref, sg_ref, su_ref, sd_ref, x_ref, mod_ref, lng_ref, lnb_ref,
                      o_ref, ya_ref, yb_ref, sem, *, tm, s_lat):
    base = pl.program_id(0) * tm
    bufs = (ya_ref, yb_ref)

    def copy(r, j):
        return pltpu.make_async_copy(y_ref.at[pl.ds(dest_ref[j, base + r], 1)], bufs[j].at[pl.ds(r, 1)], sem.at[j])

    def start(r, carry):
        for j in range(2):
            copy(r, j).start()
        return carry

    def wait(r, carry):
        for j in range(2):
            copy(r, j).wait()
        return carry

    lax.fori_loop(0, tm, start, 0)
    shared = _swiglu(_unpack_pairs(hp_ref[...]), sg_ref[...], su_ref[...], sd_ref[...])
    lax.fori_loop(0, tm, wait, 0)
    y = shared + rw_ref[:, 0:1] * ya_ref[...] + rw_ref[:, 1:2] * yb_ref[...]
    z = DEEPNORM_ALPHA * x_ref[...] + _row_mod(mod_ref, 5, base, tm, s_lat) * y
    o_ref[...] = _layer_norm(z, lng_ref[...], lnb_ref[...])


def _moe_final(dest, h2p, rw, ysorted, sg, su, sd, xs, mod, lng, lnb, rows, s_lat):
    d = D_MODEL
    tm = 256
    row = lambda i, dref: (i, 0)
    full = lambda i, dref: (0, 0)
    return pl.pallas_call(
        functools.partial(_moe_final_kernel, tm=tm, s_lat=s_lat),
        grid_spec=pltpu.PrefetchScalarGridSpec(
            num_scalar_prefetch=1, grid=(rows // tm,),
            in_specs=[pl.BlockSpec((tm, HALF), row), pl.BlockSpec((tm, 2), row), pl.BlockSpec(memory_space=pl.ANY),
                      pl.BlockSpec(sg.shape, full), pl.BlockSpec(su.shape, full), pl.BlockSpec(sd.shape, full),
                      pl.BlockSpec((tm, d), row), pl.BlockSpec(mod.shape, full),
                      pl.BlockSpec(lng.shape, full), pl.BlockSpec(lnb.shape, full)],
            out_specs=pl.BlockSpec((tm, d), row),
            scratch_shapes=[pltpu.VMEM((tm, d), F32), pltpu.VMEM((tm, d), F32), pltpu.SemaphoreType.DMA((2,))]),
        out_shape=jax.ShapeDtypeStruct((rows, d), F32),
        compiler_params=_params(("arbitrary",)),
        name="moe_final",
    )(dest, h2p, rw, ysorted, sg, su, sd, xs, mod, lng, lnb)


def _moe_plan(ri, cnt, rows):
    tme = MOE_TILE
    n_sorted = 2 * rows + N_EXPERTS * tme
    counts = cnt[:, 0].astype(jnp.int32)
    padded = (counts + tme - 1) // tme * tme
    ends = jnp.cumsum(padded)
    offs = ends - padded
    dest = jnp.stack([offs[ri[0]] + ri[2], offs[ri[1]] + ri[3]])
    tile_expert = jnp.minimum(jnp.searchsorted(ends, jnp.arange(n_sorted // tme, dtype=jnp.int32) * tme,
                                               side="right"), N_EXPERTS - 1).astype(jnp.int32)
    n_active = (ends[-1:] // tme).astype(jnp.int32)
    return dest, tile_expert, n_active, n_sorted


def _moe(xs, h2p, mod, wr_t, br, wg, wu, wd, sg, su, sd, lng, lnb, rows, s_lat):
    ri, rw, cnt = _router(xs, mod, wr_t, br, rows, s_lat)
    dest, tile_expert, n_active, n_sorted = _moe_plan(ri, cnt, rows)
    xsorted = _moe_sort(dest, h2p, n_sorted)
    ysorted = _moe_experts(tile_expert, n_active, xsorted, wg, wu, wd)
    return _moe_final(dest, h2p, rw.T, ysorted, sg, su, sd, xs, mod, lng, lnb, rows, s_lat)


def _rope_tables(s_lat, n_ctx, dim):
    quarter = dim // 4
    rows = s_lat // GRID_W
    inv_freq = ROPE_THETA ** (-jnp.arange(quarter, dtype=F32) / quarter)
    ang_r = jnp.arange(rows, dtype=F32)[:, None] * inv_freq
    ang_c = jnp.arange(GRID_W, dtype=F32)[:, None] * inv_freq
    per_row = lambda t: jnp.repeat(t, GRID_W, axis=0)
    per_col = lambda t: jnp.tile(t, (rows, 1))
    cos_r, sin_r = per_row(jnp.cos(ang_r)), per_row(jnp.sin(ang_r))
    cos_c, sin_c = per_col(jnp.cos(ang_c)), per_col(jnp.sin(ang_c))
    zero = jnp.zeros_like(cos_r)
    pad = jnp.zeros((s_lat, LANE - dim), F32)
    cos = jnp.concatenate([cos_r, cos_r, cos_c, cos_c, pad + 1.0], axis=1)
    sa = jnp.concatenate([-sin_r, zero, -sin_c, zero, pad], axis=1)
    sb = jnp.concatenate([zero, sin_r, zero, sin_c, pad], axis=1)
    ctx1 = jnp.ones((n_ctx, LANE), F32)
    ctx0 = jnp.zeros((n_ctx, LANE), F32)
    return (jnp.concatenate([cos, ctx1], axis=0), jnp.concatenate([sa, ctx0], axis=0),
            jnp.concatenate([sb, ctx0], axis=0))


def kernel(x, c, ctx, c_ctx, w_ada, b_ada, w_in, mla_q_norm, mla_w_uq, mla_kv_norm, mla_w_ukv, gqa_q_norm,
           gqa_k_norm, diff_lambda, diff_norm, w_br_mla, w_br_gqa, w_br_diff, w_o, ln1_g, ln1_b, w_router,
           b_router, moe_w_gate, moe_w_up, moe_w_down, shared_w_gate, shared_w_up, shared_w_down, ln2_g, ln2_b):
    B, s_lat, d = x.shape
    n_ctx = ctx.shape[1]
    assert B == 1 and d == D_MODEL
    s_all = s_lat + n_ctx

    xs = jnp.concatenate([x[0], ctx[0]], axis=0)
    cc = jnp.zeros((8, d), F32).at[0].set(c[0]).at[1].set(c_ctx)
    mod_all = _adaln(cc, w_ada, b_ada)

    tabs64 = _rope_tables(s_lat, n_ctx, MLA_ROPE)
    tabs128 = _rope_tables(s_lat, n_ctx, GQA_HEAD_DIM)
    wr_t = w_router.T
    br = b_router.reshape(N_EXPERTS, 1)
    row2 = lambda v: v.reshape(1, -1)

    o0 = MLA_Q_LORA + MLA_KV_LORA + MLA_ROPE
    o1 = o0 + GQA_HEADS * GQA_HEAD_DIM + 2 * GQA_KV_HEADS * GQA_HEAD_DIM
    o2 = o1 + 2 * (2 * DIFF_HEADS * DIFF_HEAD_DIM) + DIFF_HEADS * DIFF_V_DIM

    for l in range(DEPTH):
        last = l == DEPTH - 1
        mod = mod_all[l]
        w = w_in[l]
        wa = jnp.pad(w[:, :o0], ((0, 0), (0, LANE - MLA_ROPE))).astype(BF16)
        wb = w[:, o0:o1].astype(BF16)
        wc = w[:, o1:o2].astype(BF16)
        wd_gate = w[:, o2:].astype(BF16)
        wuq = jnp.pad(mla_w_uq[l].reshape(MLA_Q_LORA, MLA_HEADS, MLA_NOPE + MLA_ROPE),
                      ((0, 0), (0, 0), (0, MLA_QK_PAD - MLA_NOPE - MLA_ROPE))
                      ).reshape(MLA_Q_LORA, MLA_HEADS * MLA_QK_PAD).astype(BF16)
        ukv = mla_w_ukv[l].reshape(MLA_KV_LORA, MLA_HEADS, MLA_NOPE + MLA_V)
        wukv = jnp.concatenate([ukv[:, :, :MLA_NOPE].reshape(MLA_KV_LORA, -1),
                                ukv[:, :, MLA_NOPE:].reshape(MLA_KV_LORA, -1)], axis=1).astype(BF16)

        h, q_mla, k_mla, v_mla = _mla_proj(xs, mod, wa, row2(mla_q_norm[l]), row2(mla_kv_norm[l]),
                                           wuq, wukv, tabs64, s_lat)
        q_gqa, k_gqa, v_gqa = _gqa_proj(h, wb, row2(gqa_q_norm[l]), row2(gqa_k_norm[l]), tabs128)
        q_dif, k_dif, v_dif = _diff_proj(h, wc, tabs128)
        gates = _gate_proj(h, wd_gate)

        lam_init = 0.8 - 0.6 * math.exp(-0.3 * l)
        rows = s_lat if last else s_all
        dif_extra = (diff_lambda[l], row2(diff_norm[l]), lam_init)
        fresh = lambda width: None if last else jnp.zeros((rows, width), BF16)
        o_mla = _attention("mla", q_mla, k_mla, v_mla, s_lat, rows, into=fresh(MLA_HEADS * MLA_V))
        o_gqa = _attention("gqa", q_gqa, k_gqa, v_gqa, s_lat, rows, into=fresh(GQA_HEADS * GQA_HEAD_DIM))
        o_dif = _attention("diff", q_dif, k_dif, v_dif, s_lat, rows, extra=dif_extra,
                           into=fresh(DIFF_HEADS * DIFF_V_DIM))
        if not last:
            o_mla = _attention("mla", q_mla, k_mla, v_mla, s_lat, rows, ctx=True, into=o_mla)
            o_gqa = _attention("gqa", q_gqa, k_gqa, v_gqa, s_lat, rows, ctx=True, into=o_gqa)
            o_dif = _attention("diff", q_dif, k_dif, v_dif, s_lat, rows, extra=dif_extra, ctx=True, into=o_dif)

        xs, h2p = _merge(o_mla, o_gqa, o_dif, gates, xs, mod, w_br_mla[l].astype(BF16), w_br_gqa[l].astype(BF16),
                         w_br_diff[l].astype(BF16), w_o[l].astype(BF16), row2(ln1_g[l]), row2(ln1_b[l]), rows, s_lat)

        xs = _moe(xs, h2p, mod, wr_t, br, moe_w_gate[l], moe_w_up[l], moe_w_down[l],
                  shared_w_gate[l].astype(BF16), shared_w_up[l].astype(BF16), shared_w_down[l].astype(BF16),
                  row2(ln2_g[l]), row2(ln2_b[l]), rows, s_lat)

    return xs[None]
```

```python
import functools
import math

import jax
import jax.numpy as jnp
from jax import lax
from jax.experimental import pallas as pl
from jax.experimental.pallas import tpu as pltpu

F32 = jnp.float32
BF16 = jnp.bfloat16
HIGHEST = lax.Precision.HIGHEST

D_MODEL = 2048
DEPTH = 2
GRID_W = 64
ROPE_THETA = 10000.0
LN_EPS = 1e-5
RMS_EPS = 1e-6
MLA_HEADS = 8
MLA_Q_LORA = 512
MLA_KV_LORA = 512
MLA_NOPE = 128
MLA_ROPE = 64
MLA_V = 128
GQA_HEADS = 8
GQA_KV_HEADS = 2
GQA_HEAD_DIM = 128
DIFF_HEADS = 4
DIFF_HEAD_DIM = 128
DIFF_V_DIM = 2 * DIFF_HEAD_DIM
N_EXPERTS = 16
N_GROUPS = 4
EXPERTS_PER_GROUP = N_EXPERTS // N_GROUPS
MOE_D_FF = 512
ROUTED_SCALE = 1.0
DEEPNORM_ALPHA = (2 * DEPTH) ** 0.25

LANE = 128
MLA_QK_PAD = 2 * LANE
LOG2E = math.log2(math.e)
VMEM_LIMIT = 56 * 1024 * 1024


def _tile(rows, prefs):
    for t in prefs:
        if rows % t == 0:
            return t
    raise ValueError(f"no tile in {prefs} divides {rows}")


def _params(sem, vmem=VMEM_LIMIT, flags=None):
    return pltpu.CompilerParams(dimension_semantics=sem, vmem_limit_bytes=vmem, flags=flags)


def _row_mod(mod_ref, k, row0, tm, s_lat):
    d = D_MODEL
    lat = mod_ref[0:1, k * d:(k + 1) * d]
    ctx = mod_ref[1:2, k * d:(k + 1) * d]
    rows = row0 + lax.broadcasted_iota(jnp.int32, (tm, 1), 0)
    return jnp.where(rows >= s_lat, ctx, lat)


def _rms(x, g):
    return x * lax.rsqrt(jnp.mean(x * x, axis=-1, keepdims=True) + RMS_EPS) * g


def _rope(t, cos, sa, sb, quarter):
    return t * cos + pltpu.roll(t, LANE - quarter, 1) * sa + pltpu.roll(t, quarter, 1) * sb


def _adaln_kernel(c_ref, w_ref, b_ref, o_ref):
    a = c_ref[...]
    a = a * jax.nn.sigmoid(a)
    o_ref[0] = jnp.dot(a, w_ref[0], precision=HIGHEST, preferred_element_type=F32) + b_ref[0]


def _adaln(cc, w_ada, b_ada):
    L, d, n = w_ada.shape
    tn = 1024
    return pl.pallas_call(
        _adaln_kernel,
        grid=(L, n // tn),
        in_specs=[pl.BlockSpec((8, d), lambda l, j: (0, 0)),
                  pl.BlockSpec((1, d, tn), lambda l, j: (l, 0, j)),
                  pl.BlockSpec((1, 1, tn), lambda l, j: (l, 0, j))],
        out_specs=pl.BlockSpec((1, 8, tn), lambda l, j: (l, 0, j)),
        out_shape=jax.ShapeDtypeStruct((L, 8, n), F32),
        compiler_params=_params(("arbitrary", "arbitrary")),
        name="adaln",
    )(cc, w_ada, b_ada.reshape(L, 1, n))


def _mla_proj_kernel(x_ref, mod_ref, wa_ref, qn_ref, kvn_ref, wuq_ref, wukv_ref, cos_ref, sa_ref, sb_ref,
                     h_ref, q_ref, k_ref, v_ref, *, tm, s_lat, q_scale):
    row0 = pl.program_id(0) * tm
    sh = _row_mod(mod_ref, 0, row0, tm, s_lat)
    sc = _row_mod(mod_ref, 1, row0, tm, s_lat)
    h = (x_ref[...] * (1.0 + sc) + sh).astype(BF16)
    h_ref[...] = h
    a = jnp.dot(h, wa_ref[...], preferred_element_type=F32)
    cqn = _rms(a[:, :MLA_Q_LORA], qn_ref[...]).astype(BF16)
    ckvn = _rms(a[:, MLA_Q_LORA:MLA_Q_LORA + MLA_KV_LORA], kvn_ref[...]).astype(BF16)
    kr = a[:, MLA_Q_LORA + MLA_KV_LORA:]
    q = jnp.dot(cqn, wuq_ref[...], preferred_element_type=F32)
    kv = jnp.dot(ckvn, wukv_ref[...], preferred_element_type=F32)
    cos, sa, sb = cos_ref[...], sa_ref[...], sb_ref[...]
    quarter = MLA_ROPE // 4
    kr2 = _rope(kr, cos, sa, sb, quarter).astype(BF16)
    for hh in range(MLA_HEADS):
        c0 = hh * MLA_QK_PAD
        q_ref[:, c0:c0 + LANE] = (q[:, c0:c0 + LANE] * q_scale).astype(BF16)
        q_ref[:, c0 + LANE:c0 + 2 * LANE] = (_rope(q[:, c0 + LANE:c0 + 2 * LANE], cos, sa, sb, quarter)
                                             * q_scale).astype(BF16)
        k_ref[:, c0:c0 + LANE] = kv[:, hh * LANE:(hh + 1) * LANE].astype(BF16)
        k_ref[:, c0 + LANE:c0 + 2 * LANE] = kr2
    v_ref[...] = kv[:, MLA_HEADS * MLA_NOPE:].astype(BF16)


def _mla_proj(xs, mod, wa, qn, kvn, wuq, wukv, tabs, s_lat):
    rows, d = xs.shape
    tm = _tile(rows, (640, 512, 256))
    cos, sa, sb = tabs
    row = lambda i: (i, 0)
    full = lambda i: (0, 0)
    qk_w = MLA_HEADS * MLA_QK_PAD
    v_w = MLA_HEADS * MLA_V
    q_scale = (MLA_NOPE + MLA_ROPE) ** -0.5 * LOG2E
    return pl.pallas_call(
        functools.partial(_mla_proj_kernel, tm=tm, s_lat=s_lat, q_scale=q_scale),
        grid=(rows // tm,),
        in_specs=[pl.BlockSpec((tm, d), row), pl.BlockSpec(mod.shape, full),
                  pl.BlockSpec(wa.shape, full), pl.BlockSpec(qn.shape, full), pl.BlockSpec(kvn.shape, full),
                  pl.BlockSpec(wuq.shape, full), pl.BlockSpec(wukv.shape, full),
                  pl.BlockSpec((tm, LANE), row), pl.BlockSpec((tm, LANE), row), pl.BlockSpec((tm, LANE), row)],
        out_specs=[pl.BlockSpec((tm, d), row), pl.BlockSpec((tm, qk_w), row),
                   pl.BlockSpec((tm, qk_w), row), pl.BlockSpec((tm, v_w), row)],
        out_shape=[jax.ShapeDtypeStruct((rows, d), BF16), jax.ShapeDtypeStruct((rows, qk_w), BF16),
                   jax.ShapeDtypeStruct((rows, qk_w), BF16), jax.ShapeDtypeStruct((rows, v_w), BF16)],
        compiler_params=_params(("parallel",)),
        name="mla_proj",
    )(xs, mod, wa, qn, kvn, wuq, wukv, cos, sa, sb)


def _gqa_proj_kernel(h_ref, w_ref, qn_ref, kn_ref, cos_ref, sa_ref, sb_ref, q_ref, k_ref, v_ref, *, q_scale):
    a = jnp.dot(h_ref[...], w_ref[...], preferred_element_type=F32)
    cos, sa, sb = cos_ref[...], sa_ref[...], sb_ref[...]
    quarter = GQA_HEAD_DIM // 4
    for hh in range(GQA_HEADS):
        x = _rms(a[:, hh * LANE:(hh + 1) * LANE], qn_ref[...])
        q_ref[:, hh * LANE:(hh + 1) * LANE] = (_rope(x, cos, sa, sb, quarter) * q_scale).astype(BF16)
    k0 = GQA_HEADS * GQA_HEAD_DIM
    for hh in range(GQA_KV_HEADS):
        x = _rms(a[:, k0 + hh * LANE:k0 + (hh + 1) * LANE], kn_ref[...])
        k_ref[:, hh * LANE:(hh + 1) * LANE] = _rope(x, cos, sa, sb, quarter).astype(BF16)
    v_ref[...] = a[:, k0 + GQA_KV_HEADS * GQA_HEAD_DIM:].astype(BF16)


def _gqa_proj(h, w, qn, kn, tabs):
    rows, d = h.shape
    tm = _tile(rows, (640, 512, 256))
    cos, sa, sb = tabs
    row = lambda i: (i, 0)
    full = lambda i: (0, 0)
    qw = GQA_HEADS * GQA_HEAD_DIM
    kw = GQA_KV_HEADS * GQA_HEAD_DIM
    return pl.pallas_call(
        functools.partial(_gqa_proj_kernel, q_scale=GQA_HEAD_DIM ** -0.5 * LOG2E),
        grid=(rows // tm,),
        in_specs=[pl.BlockSpec((tm, d), row), pl.BlockSpec(w.shape, full),
                  pl.BlockSpec(qn.shape, full), pl.BlockSpec(kn.shape, full),
                  pl.BlockSpec((tm, LANE), row), pl.BlockSpec((tm, LANE), row), pl.BlockSpec((tm, LANE), row)],
        out_specs=[pl.BlockSpec((tm, qw), row), pl.BlockSpec((tm, kw), row), pl.BlockSpec((tm, kw), row)],
        out_shape=[jax.ShapeDtypeStruct((rows, qw), BF16), jax.ShapeDtypeStruct((rows, kw), BF16),
                   jax.ShapeDtypeStruct((rows, kw), BF16)],
        compiler_params=_params(("parallel",)),
        name="gqa_proj",
    )(h, w, qn, kn, cos, sa, sb)


def _diff_proj_kernel(h_ref, w_ref, cos_ref, sa_ref, sb_ref, q_ref, k_ref, v_ref, *, q_scale):
    a = jnp.dot(h_ref[...], w_ref[...], preferred_element_type=F32)
    cos, sa, sb = cos_ref[...], sa_ref[...], sb_ref[...]
    quarter = DIFF_HEAD_DIM // 4
    n = 2 * DIFF_HEADS
    for hh in range(n):
        q_ref[:, hh * LANE:(hh + 1) * LANE] = (_rope(a[:, hh * LANE:(hh + 1) * LANE], cos, sa, sb, quarter)
                                               * q_scale).astype(BF16)
        k_ref[:, hh * LANE:(hh + 1) * LANE] = _rope(a[:, (n + hh) * LANE:(n + hh + 1) * LANE],
                                                    cos, sa, sb, quarter).astype(BF16)
    v_ref[...] = a[:, 2 * n * LANE:].astype(BF16)


def _diff_proj(h, w, tabs):
    rows, d = h.shape
    tm = _tile(rows, (640, 512, 256))
    cos, sa, sb = tabs
    row = lambda i: (i, 0)
    full = lambda i: (0, 0)
    ww = 2 * DIFF_HEADS * DIFF_HEAD_DIM
    vw = DIFF_HEADS * DIFF_V_DIM
    return pl.pallas_call(
        functools.partial(_diff_proj_kernel, q_scale=DIFF_HEAD_DIM ** -0.5 * LOG2E),
        grid=(rows // tm,),
        in_specs=[pl.BlockSpec((tm, d), row), pl.BlockSpec(w.shape, full),
                  pl.BlockSpec((tm, LANE), row), pl.BlockSpec((tm, LANE), row), pl.BlockSpec((tm, LANE), row)],
        out_specs=[pl.BlockSpec((tm, ww), row), pl.BlockSpec((tm, ww), row), pl.BlockSpec((tm, vw), row)],
        out_shape=[jax.ShapeDtypeStruct((rows, ww), BF16), jax.ShapeDtypeStruct((rows, ww), BF16),
                   jax.ShapeDtypeStruct((rows, vw), BF16)],
        compiler_params=_params(("parallel",)),
        name="diff_proj",
    )(h, w, cos, sa, sb)


def _gate_proj_kernel(h_ref, w_ref, o_ref):
    a = jnp.dot(h_ref[...], w_ref[...], preferred_element_type=F32)
    o_ref[...] = jax.nn.sigmoid(a).astype(BF16)


def _gate_proj(h, w):
    rows, d = h.shape
    n = w.shape[1]
    tm = _tile(rows, (640, 512, 256))
    tn = 1536
    return pl.pallas_call(
        _gate_proj_kernel,
        grid=(rows // tm, n // tn),
        in_specs=[pl.BlockSpec((tm, d), lambda i, j: (i, 0)), pl.BlockSpec((d, tn), lambda i, j: (0, j))],
        out_specs=pl.BlockSpec((tm, tn), lambda i, j: (i, j)),
        out_shape=jax.ShapeDtypeStruct((rows, n), BF16),
        compiler_params=_params(("parallel", "arbitrary")),
        name="gate_proj",
    )(h, w)


def _rows(start, size):
    return pl.ds(start if isinstance(start, int) else pl.multiple_of(start, size), size)


def _dot_nt(a, b):
    return lax.dot_general(a, b, (((1,), (1,)), ((), ())), preferred_element_type=F32)


def _flash(score, v_ref, n_sub, n_chunks, bkv, s_ref, mx_ref, m_ref, l_ref, acc_ref):
    m_ref[...] = jnp.full(m_ref.shape, -jnp.inf, F32)
    l_ref[...] = jnp.zeros(l_ref.shape, F32)
    acc_ref[...] = jnp.zeros(acc_ref.shape, F32)

    def item(k):
        if isinstance(k, int):
            return k // n_chunks, k % n_chunks
        t = lax.div(k, n_chunks)
        return t, k - t * n_chunks

    def issue(k, slot):
        t, c = item(k)
        score(t, c, slot)

    def absorb(k, slot):
        t, c = item(k)
        m_prev = m_ref[t]
        m_new = jnp.maximum(m_prev, mx_ref[slot])
        alpha = jnp.exp2(m_prev - m_new)
        p = jnp.exp2(s_ref[slot] - m_new)
        l_ref[t] = alpha * l_ref[t] + jnp.sum(p, axis=1, keepdims=True)
        acc_ref[t] = alpha * acc_ref[t] + jnp.dot(p.astype(BF16), v_ref[_rows(c * bkv, bkv), :],
                                                  preferred_element_type=F32)
        m_ref[t] = m_new

    n_items = n_sub * n_chunks
    issue(0, 0)

    def body(j, carry):
        k = 2 * j
        issue(k + 1, 1)
        absorb(k, 0)
        issue(k + 2, 0)
        absorb(k + 1, 1)
        return carry

    n_pairs = (n_items - 1) // 2
    if n_pairs:
        lax.fori_loop(0, n_pairs, body, 0)
    k = 2 * n_pairs
    if n_items % 2 == 1:
        absorb(k, 0)
    else:
        issue(k + 1, 1)
        absorb(k, 0)
        absorb(k + 1, 1)


def _score_into(s_ref, mx_ref, slot, row0, q, kc):
    s = _dot_nt(q, kc)
    rows = q.shape[0]
    s_ref[slot, row0:row0 + rows] = s
    mx_ref[slot, row0:row0 + rows] = jnp.max(s, axis=1, keepdims=True)


def _mla_attn_kernel(q_ref, k_ref, v_ref, *rest, n_sub, n_chunks, bkv):
    o_ref, s_ref, mx_ref, m_ref, l_ref, acc_ref = rest[-6:]
    m_rows = s_ref.shape[1]

    def score(t, c, slot):
        _score_into(s_ref, mx_ref, slot, 0, q_ref[_rows(t * m_rows, m_rows), :], k_ref[_rows(c * bkv, bkv), :])

    _flash(score, v_ref, n_sub, n_chunks, bkv, s_ref, mx_ref, m_ref, l_ref, acc_ref)
    for t in range(n_sub):
        o_ref[t * m_rows:(t + 1) * m_rows, :] = (acc_ref[t] / l_ref[t]).astype(BF16)


def _gqa_attn_kernel(q_ref, k_ref, v_ref, *rest, n_sub, n_chunks, bkv):
    o_ref, qs_ref, s_ref, mx_ref, m_ref, l_ref, acc_ref = rest[-7:]
    g = GQA_HEADS // GQA_KV_HEADS
    bqs = s_ref.shape[1] // g
    for t in range(n_sub):
        for j in range(g):
            qs_ref[t, j * bqs:(j + 1) * bqs] = q_ref[t * bqs:(t + 1) * bqs, j * LANE:(j + 1) * LANE]

    def score(t, c, slot):
        _score_into(s_ref, mx_ref, slot, 0, qs_ref[t], k_ref[_rows(c * bkv, bkv), :])

    _flash(score, v_ref, n_sub, n_chunks, bkv, s_ref, mx_ref, m_ref, l_ref, acc_ref)
    for t in range(n_sub):
        o = acc_ref[t] / l_ref[t]
        for j in range(g):
            o_ref[t * bqs:(t + 1) * bqs, j * LANE:(j + 1) * LANE] = o[j * bqs:(j + 1) * bqs].astype(BF16)


def _diff_attn_kernel(q_ref, k_ref, v_ref, lam_ref, dn_ref, *rest, n_sub, n_chunks, bkv, lam_init):
    o_ref, s_ref, mx_ref, m_ref, l_ref, acc_ref = rest[-6:]
    d = DIFF_HEAD_DIM
    bqs = s_ref.shape[1] // 2
    lp = lam_ref[...]
    lam = (jnp.exp(jnp.sum(lp[0:1] * lp[1:2], axis=1, keepdims=True))
           - jnp.exp(jnp.sum(lp[2:3] * lp[3:4], axis=1, keepdims=True)) + lam_init)

    def score(t, c, slot):
        qr, kr = _rows(t * bqs, bqs), _rows(c * bkv, bkv)
        _score_into(s_ref, mx_ref, slot, 0, q_ref[qr, 0:d], k_ref[kr, 0:d])
        _score_into(s_ref, mx_ref, slot, bqs, q_ref[qr, d:2 * d], k_ref[kr, d:2 * d])

    _flash(score, v_ref, n_sub, n_chunks, bkv, s_ref, mx_ref, m_ref, l_ref, acc_ref)
    for t in range(n_sub):
        o = acc_ref[t] / l_ref[t]
        o = o[:bqs] - lam * o[bqs:]
        o_ref[t * bqs:(t + 1) * bqs, :] = (_rms(o, dn_ref[...]) * (1.0 - lam_init)).astype(BF16)


def _attention(kind, q, k, v, s_lat, out_rows, extra=(), ctx=False, into=None):
    s_all = q.shape[0]
    n_ctx = s_all - s_lat
    assert s_lat % n_ctx == 0
    if kind == "mla":
        heads, qw, kw, vw, ow, stack = MLA_HEADS, MLA_QK_PAD, MLA_QK_PAD, MLA_V, MLA_V, 1
        bqs = n_ctx if ctx else _tile(s_lat, (1024, 512, 256))
        body = _mla_attn_kernel
    elif kind == "gqa":
        g = GQA_HEADS // GQA_KV_HEADS
        heads, qw, kw, vw, ow, stack = GQA_KV_HEADS, g * LANE, LANE, LANE, g * LANE, g
        bqs = n_ctx if ctx else 256
        body = _gqa_attn_kernel
    else:
        heads, qw, kw, vw, ow, stack = DIFF_HEADS, 2 * LANE, 2 * LANE, DIFF_V_DIM, DIFF_V_DIM, 2
        bqs = n_ctx if ctx else _tile(s_lat, (512, 256))
        body = functools.partial(_diff_attn_kernel, lam_init=extra[2])
        extra = extra[:2]
    m_rows = stack * bqs
    n_sub = 1 if ctx else _tile(s_lat // bqs, (4, 3, 2, 1))
    bq = n_sub * bqs
    if ctx:
        kv_rows, bkv, row_blk, nq = n_ctx, n_ctx, s_lat // n_ctx, 1
    else:
        kv_rows, bkv, row_blk, nq = s_all, _tile(s_all, (1280, 1024, 512, 256)), 0, s_lat // bq
    body = functools.partial(body, n_sub=n_sub, n_chunks=kv_rows // bkv, bkv=bkv)
    kv_blk = row_blk
    in_specs = [pl.BlockSpec((bq, qw), lambda h, i: (row_blk + i, h)),
                pl.BlockSpec((kv_rows, kw), lambda h, i: (kv_blk, h), pipeline_mode=pl.Buffered(1)),
                pl.BlockSpec((kv_rows, vw), lambda h, i: (kv_blk, h), pipeline_mode=pl.Buffered(1))]
    in_specs += [pl.BlockSpec(e.shape, lambda h, i: (0, 0)) for e in extra]
    args = [q, k, v, *extra]
    aliases = {}
    if into is not None:
        in_specs.append(pl.BlockSpec(memory_space=pl.ANY))
        aliases = {len(args): 0}
        args.append(into)
    scratch = []
    if kind == "gqa":
        scratch.append(pltpu.VMEM((n_sub, m_rows, LANE), BF16))
    scratch += [pltpu.VMEM((2, m_rows, bkv), F32),
                pltpu.VMEM((2, m_rows, 1), F32), pltpu.VMEM((n_sub, m_rows, 1), F32),
                pltpu.VMEM((n_sub, m_rows, 1), F32), pltpu.VMEM((n_sub, m_rows, vw), F32)]
    return pl.pallas_call(
        body,
        grid=(heads, nq),
        in_specs=in_specs,
        out_specs=pl.BlockSpec((bq, ow), lambda h, i: (row_blk + i, h)),
        out_shape=jax.ShapeDtypeStruct((out_rows, heads * ow), BF16),
        scratch_shapes=scratch,
        input_output_aliases=aliases,
        compiler_params=_params(("parallel", "arbitrary")),
        name=kind + ("_attn_ctx" if ctx else "_attn"),
    )(*args)


def _layer_norm(z, g, b):
    mu = jnp.mean(z, axis=-1, keepdims=True)
    zc = z - mu
    var = jnp.mean(zc * zc, axis=-1, keepdims=True)
    return zc * lax.rsqrt(var + LN_EPS) * g + b


def _merge_kernel(om_ref, og_ref, od_ref, g_ref, x_ref, mod_ref, wbm_ref, wbg_ref, wbd_ref, wo_ref,
                  lng_ref, lnb_ref, xo_ref, h2_ref, *, tm, s_lat):
    d = D_MODEL
    row0 = pl.program_id(0) * tm
    merged = g_ref[:, 0:d].astype(F32) * jnp.dot(om_ref[...], wbm_ref[...], preferred_element_type=F32)
    merged += g_ref[:, d:2 * d].astype(F32) * jnp.dot(og_ref[...], wbg_ref[...], preferred_element_type=F32)
    merged += g_ref[:, 2 * d:3 * d].astype(F32) * jnp.dot(od_ref[...], wbd_ref[...], preferred_element_type=F32)
    y = jnp.dot(merged.astype(BF16), wo_ref[...], preferred_element_type=F32)
    z = DEEPNORM_ALPHA * x_ref[...] + _row_mod(mod_ref, 2, row0, tm, s_lat) * y
    xn = _layer_norm(z, lng_ref[...], lnb_ref[...])
    xo_ref[...] = xn
    h2 = xn * (1.0 + _row_mod(mod_ref, 4, row0, tm, s_lat)) + _row_mod(mod_ref, 3, row0, tm, s_lat)
    h2_ref[...] = _pack_pairs(h2)


def _merge(om, og, od, gates, xs, mod, wbm, wbg, wbd, wo, lng, lnb, rows, s_lat):
    d = D_MODEL
    tm = 256
    row = lambda i: (i, 0)
    full = lambda i: (0, 0)
    resident = lambda a: pl.BlockSpec(a.shape, full, pipeline_mode=pl.Buffered(1))
    return pl.pallas_call(
        functools.partial(_merge_kernel, tm=tm, s_lat=s_lat),
        grid=(rows // tm,),
        in_specs=[pl.BlockSpec((tm, om.shape[1]), row), pl.BlockSpec((tm, og.shape[1]), row),
                  pl.BlockSpec((tm, od.shape[1]), row), pl.BlockSpec((tm, gates.shape[1]), row),
                  pl.BlockSpec((tm, d), row), pl.BlockSpec(mod.shape, full),
                  resident(wbm), resident(wbg), resident(wbd), resident(wo),
                  pl.BlockSpec(lng.shape, full), pl.BlockSpec(lnb.shape, full)],
        out_specs=[pl.BlockSpec((tm, d), row), pl.BlockSpec((tm, d // 2), row)],
        out_shape=[jax.ShapeDtypeStruct((rows, d), F32), jax.ShapeDtypeStruct((rows, d // 2), jnp.uint32)],
        compiler_params=_params(("parallel",)),
        name="merge",
    )(om, og, od, gates, xs, mod, wbm, wbg, wbd, wo, lng, lnb)


def _router_kernel(x_ref, mod_ref, wr_ref, br_ref, ri_ref, rw_ref, cnt_ref, run_ref, *, tm, s_lat):
    @pl.when(pl.program_id(0) == 0)
    def _():
        run_ref[...] = jnp.zeros(run_ref.shape, F32)

    row0 = pl.program_id(0) * tm
    h2 = x_ref[...] * (1.0 + _row_mod(mod_ref, 4, row0, tm, s_lat)) + _row_mod(mod_ref, 3, row0, tm, s_lat)
    logits = lax.dot_general(wr_ref[...], h2, (((1,), (1,)), ((), ())),
                             precision=HIGHEST, preferred_element_type=F32)
    scores = jax.nn.sigmoid(logits)
    biased = scores + br_ref[...]
    sc = [scores[e:e + 1, :] for e in range(N_EXPERTS)]
    bi = [biased[e:e + 1, :] for e in range(N_EXPERTS)]
    gs = []
    for g in range(N_GROUPS):
        a, b, c, dd = bi[4 * g:4 * g + 4]
        hi1, lo1, hi2, lo2 = jnp.maximum(a, b), jnp.minimum(a, b), jnp.maximum(c, dd), jnp.minimum(c, dd)
        gs.append(jnp.maximum(hi1, hi2) + jnp.maximum(jnp.minimum(hi1, hi2), jnp.maximum(lo1, lo2)))
    best = jnp.maximum(jnp.maximum(gs[0], gs[1]), jnp.maximum(gs[2], gs[3]))
    gsel = jnp.where(gs[0] == best, 0, jnp.where(gs[1] == best, 1, jnp.where(gs[2] == best, 2, 3)))
    pick = lambda vals, j: jnp.where(gsel == 0, vals[j], jnp.where(gsel == 1, vals[4 + j],
                                     jnp.where(gsel == 2, vals[8 + j], vals[12 + j])))
    xb = [pick(bi, j) for j in range(EXPERTS_PER_GROUP)]
    xs = [pick(sc, j) for j in range(EXPERTS_PER_GROUP)]
    sel, w = [], []
    for i in range(EXPERTS_PER_GROUP):
        rank = jnp.zeros_like(gsel)
        for j in range(EXPERTS_PER_GROUP):
            if j == i:
                continue
            beats = (xb[j] >= xb[i]) if j < i else (xb[j] > xb[i])
            rank = rank + beats.astype(jnp.int32)
        sel.append(rank < 2)
        w.append(jnp.where(rank < 2, xs[i], 0.0))
    inv = ROUTED_SCALE / (w[0] + w[1] + w[2] + w[3])
    ia = jnp.where(sel[0], 0, jnp.where(sel[1], 1, jnp.where(sel[2], 2, 3)))
    ib = jnp.where(sel[3], 3, jnp.where(sel[2], 2, jnp.where(sel[1], 1, 0)))
    local = lambda idx: jnp.where(idx == 0, w[0], jnp.where(idx == 1, w[1], jnp.where(idx == 2, w[2], w[3])))
    ea = gsel * EXPERTS_PER_GROUP + ia
    eb = gsel * EXPERTS_PER_GROUP + ib
    eidx = lax.broadcasted_iota(jnp.int32, (N_EXPERTS, tm), 0)
    is_a, is_b = eidx == ea, eidx == eb
    onehot = jnp.where(is_a | is_b, 1.0, 0.0)
    earlier = jnp.where(lax.broadcasted_iota(jnp.int32, (tm, tm), 0) < lax.broadcasted_iota(jnp.int32, (tm, tm), 1),
                        1.0, 0.0).astype(BF16)
    pos = run_ref[...] + jnp.dot(onehot.astype(BF16), earlier, preferred_element_type=F32)
    ri_ref[0:1, :] = ea
    ri_ref[1:2, :] = eb
    ri_ref[2:3, :] = jnp.sum(jnp.where(is_a, pos, 0.0), axis=0, keepdims=True).astype(jnp.int32)
    ri_ref[3:4, :] = jnp.sum(jnp.where(is_b, pos, 0.0), axis=0, keepdims=True).astype(jnp.int32)
    rw_ref[0:1, :] = local(ia) * inv
    rw_ref[1:2, :] = local(ib) * inv
    run = run_ref[...] + jnp.sum(onehot, axis=1, keepdims=True)
    run_ref[...] = run
    cnt_ref[...] = jnp.broadcast_to(run, cnt_ref.shape)


def _router(xs, mod, wr_t, br, rows, s_lat):
    d = D_MODEL
    tm = _tile(rows, (640, 512, 256))
    return pl.pallas_call(
        functools.partial(_router_kernel, tm=tm, s_lat=s_lat),
        grid=(rows // tm,),
        in_specs=[pl.BlockSpec((tm, d), lambda i: (i, 0)), pl.BlockSpec(mod.shape, lambda i: (0, 0)),
                  pl.BlockSpec(wr_t.shape, lambda i: (0, 0)), pl.BlockSpec(br.shape, lambda i: (0, 0))],
        out_specs=[pl.BlockSpec((4, tm), lambda i: (0, i)), pl.BlockSpec((2, tm), lambda i: (0, i)),
                   pl.BlockSpec((N_EXPERTS, LANE), lambda i: (0, 0))],
        out_shape=[jax.ShapeDtypeStruct((4, rows), jnp.int32), jax.ShapeDtypeStruct((2, rows), F32),
                   jax.ShapeDtypeStruct((N_EXPERTS, LANE), F32)],
        scratch_shapes=[pltpu.VMEM((N_EXPERTS, 1), F32)],
        compiler_params=_params(("arbitrary",)),
        name="router",
    )(xs, mod, wr_t, br)


MOE_TILE = 512
HALF = D_MODEL // 2


def _pack_pairs(h):
    hi = pltpu.bitcast(h[:, :HALF].astype(BF16).astype(F32), jnp.uint32)
    lo = pltpu.bitcast(h[:, HALF:].astype(BF16).astype(F32), jnp.uint32)
    return (hi & jnp.uint32(0xFFFF0000)) | (lo >> jnp.uint32(16))


def _unpack_pairs(w):
    hi = pltpu.bitcast(w & jnp.uint32(0xFFFF0000), F32)
    lo = pltpu.bitcast(w << jnp.uint32(16), F32)
    return jnp.concatenate([hi, lo], axis=1).astype(BF16)


def _swiglu(x, wg, wu, wd):
    a = jnp.dot(x, wg, preferred_element_type=F32)
    b = jnp.dot(x, wu, preferred_element_type=F32)
    return jnp.dot((a * jax.nn.sigmoid(a) * b).astype(BF16), wd, preferred_element_type=F32)


def _moe_sort_kernel(dest_ref, h_ref, init_ref, o_ref, sem, *, tg):
    del init_ref
    base = pl.program_id(0) * tg

    def copy(r, j):
        return pltpu.make_async_copy(h_ref.at[pl.ds(r, 1)], o_ref.at[pl.ds(dest_ref[j, base + r], 1)], sem)

    def start(r, carry):
        for j in range(2):
            copy(r, j).start()
        return carry

    def wait(r, carry):
        for j in range(2):
            copy(r, j).wait()
        return carry

    lax.fori_loop(0, tg, start, 0)
    lax.fori_loop(0, tg, wait, 0)


def _moe_sort(dest, h2p, n_sorted):
    rows, half = h2p.shape
    tg = _tile(rows, (1280, 1024, 512, 256))
    return pl.pallas_call(
        functools.partial(_moe_sort_kernel, tg=tg),
        grid_spec=pltpu.PrefetchScalarGridSpec(
            num_scalar_prefetch=1, grid=(rows // tg,),
            in_specs=[pl.BlockSpec((tg, half), lambda i, dref: (i, 0)), pl.BlockSpec(memory_space=pl.ANY)],
            out_specs=pl.BlockSpec(memory_space=pl.ANY),
            scratch_shapes=[pltpu.SemaphoreType.DMA]),
        out_shape=jax.ShapeDtypeStruct((n_sorted, half), jnp.uint32),
        input_output_aliases={2: 0},
        compiler_params=_params(("arbitrary",)),
        name="moe_sort",
    )(dest, h2p, jnp.zeros((n_sorted, half), jnp.uint32))


def _moe_expert_kernel(te_ref, na_ref, x_ref, wg_ref, wu_ref, wd_ref, y_ref):
    del te_ref
    live = pl.program_id(0) < na_ref[0]

    @pl.when(live)
    def _():
        y_ref[...] = _swiglu(_unpack_pairs(x_ref[...]), wg_ref[0].astype(BF16), wu_ref[0].astype(BF16),
                             wd_ref[0].astype(BF16))

    @pl.when(jnp.logical_not(live))
    def _():
        y_ref[...] = jnp.zeros(y_ref.shape, F32)


def _moe_experts(tile_expert, n_active, xsorted, wg, wu, wd):
    n_sorted, half = xsorted.shape
    _, d, f = wg.shape
    tme = MOE_TILE
    wsel = lambda i, te, na: (te[i], 0, 0)
    return pl.pallas_call(
        _moe_expert_kernel,
        grid_spec=pltpu.PrefetchScalarGridSpec(
            num_scalar_prefetch=2, grid=(n_sorted // tme,),
            in_specs=[pl.BlockSpec((tme, half), lambda i, te, na: (i, 0)),
                      pl.BlockSpec((1, d, f), wsel), pl.BlockSpec((1, d, f), wsel), pl.BlockSpec((1, f, d), wsel)],
            out_specs=pl.BlockSpec((tme, d), lambda i, te, na: (i, 0))),
        out_shape=jax.ShapeDtypeStruct((n_sorted, d), F32),
        compiler_params=_params(("arbitrary",)),
        name="moe_experts",
    )(tile_expert, n_active, xsorted, wg, wu, wd)


def _moe_final_kernel(dest_ref, hp_ref, rw_ref, y_ref, sg_ref, su_ref, sd_ref, x_ref, mod_ref, lng_ref, lnb_ref,
                      o_ref, ya_ref, yb_ref, sem, *, tm, s_lat):
    base = pl.program_id(0) * tm
    bufs = (ya_ref, yb_ref)

    def copy(r, j):
        return pltpu.make_async_copy(y_ref.at[pl.ds(dest_ref[j, base + r], 1)], bufs[j].at[pl.ds(r, 1)], sem.at[j])

    def start(r, carry):
        for j in range(2):
            copy(r, j).start()
        return carry

    def wait(r, carry):
        for j in range(2):
            copy(r, j).wait()
        return carry

    lax.fori_loop(0, tm, start, 0)
    shared = _swiglu(_unpack_pairs(hp_ref[...]), sg_ref[...], su_ref[...], sd_ref[...])
    lax.fori_loop(0, tm, wait, 0)
    y = shared + rw_ref[:, 0:1] * ya_ref[...] + rw_ref[:, 1:2] * yb_ref[...]
    z = DEEPNORM_ALPHA * x_ref[...] + _row_mod(mod_ref, 5, base, tm, s_lat) * y
    o_ref[...] = _layer_norm(z, lng_ref[...], lnb_ref[...])


def _moe_final(dest, h2p, rw, ysorted, sg, su, sd, xs, mod, lng, lnb, rows, s_lat):
    d = D_MODEL
    tm = 256
    row = lambda i, dref: (i, 0)
    full = lambda i, dref: (0, 0)
    return pl.pallas_call(
        functools.partial(_moe_final_kernel, tm=tm, s_lat=s_lat),
        grid_spec=pltpu.PrefetchScalarGridSpec(
            num_scalar_prefetch=1, grid=(rows // tm,),
            in_specs=[pl.BlockSpec((tm, HALF), row), pl.BlockSpec((tm, 2), row), pl.BlockSpec(memory_space=pl.ANY),
                      pl.BlockSpec(sg.shape, full), pl.BlockSpec(su.shape, full), pl.BlockSpec(sd.shape, full),
                      pl.BlockSpec((tm, d), row), pl.BlockSpec(mod.shape, full),
                      pl.BlockSpec(lng.shape, full), pl.BlockSpec(lnb.shape, full)],
            out_specs=pl.BlockSpec((tm, d), row),
            scratch_shapes=[pltpu.VMEM((tm, d), F32), pltpu.VMEM((tm, d), F32), pltpu.SemaphoreType.DMA((2,))]),
        out_shape=jax.ShapeDtypeStruct((rows, d), F32),
        compiler_params=_params(("arbitrary",)),
        name="moe_final",
    )(dest, h2p, rw, ysorted, sg, su, sd, xs, mod, lng, lnb)


def _moe_plan(ri, cnt, rows):
    tme = MOE_TILE
    n_sorted = 2 * rows + N_EXPERTS * tme
    counts = cnt[:, 0].astype(jnp.int32)
    padded = (counts + tme - 1) // tme * tme
    ends = jnp.cumsum(padded)
    offs = ends - padded
    dest = jnp.stack([offs[ri[0]] + ri[2], offs[ri[1]] + ri[3]])
    tile_start = jnp.arange(n_sorted // tme, dtype=jnp.int32) * tme
    tile_expert = jnp.minimum(jnp.sum((ends[None, :] <= tile_start[:, None]).astype(jnp.int32), axis=1),
                              N_EXPERTS - 1)
    n_active = (ends[-1:] // tme).astype(jnp.int32)
    return dest, tile_expert, n_active, n_sorted


def _moe(xs, h2p, mod, wr_t, br, wg, wu, wd, sg, su, sd, lng, lnb, rows, s_lat):
    ri, rw, cnt = _router(xs, mod, wr_t, br, rows, s_lat)
    dest, tile_expert, n_active, n_sorted = _moe_plan(ri, cnt, rows)
    xsorted = _moe_sort(dest, h2p, n_sorted)
    ysorted = _moe_experts(tile_expert, n_active, xsorted, wg, wu, wd)
    return _moe_final(dest, h2p, rw.T, ysorted, sg, su, sd, xs, mod, lng, lnb, rows, s_lat)


def _rope_tables(s_lat, n_ctx, dim):
    quarter = dim // 4
    rows = s_lat // GRID_W
    inv_freq = ROPE_THETA ** (-jnp.arange(quarter, dtype=F32) / quarter)
    ang_r = jnp.arange(rows, dtype=F32)[:, None] * inv_freq
    ang_c = jnp.arange(GRID_W, dtype=F32)[:, None] * inv_freq
    per_row = lambda t: jnp.repeat(t, GRID_W, axis=0)
    per_col = lambda t: jnp.tile(t, (rows, 1))
    cos_r, sin_r = per_row(jnp.cos(ang_r)), per_row(jnp.sin(ang_r))
    cos_c, sin_c = per_col(jnp.cos(ang_c)), per_col(jnp.sin(ang_c))
    zero = jnp.zeros_like(cos_r)
    pad = jnp.zeros((s_lat, LANE - dim), F32)
    cos = jnp.concatenate([cos_r, cos_r, cos_c, cos_c, pad + 1.0], axis=1)
    sa = jnp.concatenate([-sin_r, zero, -sin_c, zero, pad], axis=1)
    sb = jnp.concatenate([zero, sin_r, zero, sin_c, pad], axis=1)
    ctx1 = jnp.ones((n_ctx, LANE), F32)
    ctx0 = jnp.zeros((n_ctx, LANE), F32)
    return (jnp.concatenate([cos, ctx1], axis=0), jnp.concatenate([sa, ctx0], axis=0),
            jnp.concatenate([sb, ctx0], axis=0))


def kernel(x, c, ctx, c_ctx, w_ada, b_ada, w_in, mla_q_norm, mla_w_uq, mla_kv_norm, mla_w_ukv, gqa_q_norm,
           gqa_k_norm, diff_lambda, diff_norm, w_br_mla, w_br_gqa, w_br_diff, w_o, ln1_g, ln1_b, w_router,
           b_router, moe_w_gate, moe_w_up, moe_w_down, shared_w_gate, shared_w_up, shared_w_down, ln2_g, ln2_b):
    B, s_lat, d = x.shape
    n_ctx = ctx.shape[1]
    assert B == 1 and d == D_MODEL
    s_all = s_lat + n_ctx

    xs = jnp.concatenate([x[0], ctx[0]], axis=0)
    cc = jnp.zeros((8, d), F32).at[0].set(c[0]).at[1].set(c_ctx)
    mod_all = _adaln(cc, w_ada, b_ada)

    tabs64 = _rope_tables(s_lat, n_ctx, MLA_ROPE)
    tabs128 = _rope_tables(s_lat, n_ctx, GQA_HEAD_DIM)
    wr_t = w_router.T
    br = b_router.reshape(N_EXPERTS, 1)
    row2 = lambda v: v.reshape(1, -1)

    o0 = MLA_Q_LORA + MLA_KV_LORA + MLA_ROPE
    o1 = o0 + GQA_HEADS * GQA_HEAD_DIM + 2 * GQA_KV_HEADS * GQA_HEAD_DIM
    o2 = o1 + 2 * (2 * DIFF_HEADS * DIFF_HEAD_DIM) + DIFF_HEADS * DIFF_V_DIM

    for l in range(DEPTH):
        last = l == DEPTH - 1
        mod = mod_all[l]
        w = w_in[l]
        wa = jnp.pad(w[:, :o0], ((0, 0), (0, LANE - MLA_ROPE))).astype(BF16)
        wb = w[:, o0:o1].astype(BF16)
        wc = w[:, o1:o2].astype(BF16)
        wd_gate = w[:, o2:].astype(BF16)
        wuq = jnp.pad(mla_w_uq[l].reshape(MLA_Q_LORA, MLA_HEADS, MLA_NOPE + MLA_ROPE),
                      ((0, 0), (0, 0), (0, MLA_QK_PAD - MLA_NOPE - MLA_ROPE))
                      ).reshape(MLA_Q_LORA, MLA_HEADS * MLA_QK_PAD).astype(BF16)
        ukv = mla_w_ukv[l].reshape(MLA_KV_LORA, MLA_HEADS, MLA_NOPE + MLA_V)
        wukv = jnp.concatenate([ukv[:, :, :MLA_NOPE].reshape(MLA_KV_LORA, -1),
                                ukv[:, :, MLA_NOPE:].reshape(MLA_KV_LORA, -1)], axis=1).astype(BF16)

        h, q_mla, k_mla, v_mla = _mla_proj(xs, mod, wa, row2(mla_q_norm[l]), row2(mla_kv_norm[l]),
                                           wuq, wukv, tabs64, s_lat)
        q_gqa, k_gqa, v_gqa = _gqa_proj(h, wb, row2(gqa_q_norm[l]), row2(gqa_k_norm[l]), tabs128)
        q_dif, k_dif, v_dif = _diff_proj(h, wc, tabs128)
        gates = _gate_proj(h, wd_gate)

        lam_init = 0.8 - 0.6 * math.exp(-0.3 * l)
        rows = s_lat if last else s_all
        dif_extra = (diff_lambda[l], row2(diff_norm[l]), lam_init)
        fresh = lambda width: None if last else jnp.zeros((rows, width), BF16)
        o_mla = _attention("mla", q_mla, k_mla, v_mla, s_lat, rows, into=fresh(MLA_HEADS * MLA_V))
        o_gqa = _attention("gqa", q_gqa, k_gqa, v_gqa, s_lat, rows, into=fresh(GQA_HEADS * GQA_HEAD_DIM))
        o_dif = _attention("diff", q_dif, k_dif, v_dif, s_lat, rows, extra=dif_extra,
                           into=fresh(DIFF_HEADS * DIFF_V_DIM))
        if not last:
            o_mla = _attention("mla", q_mla, k_mla, v_mla, s_lat, rows, ctx=True, into=o_mla)
            o_gqa = _attention("gqa", q_gqa, k_gqa, v_gqa, s_lat, rows, ctx=True, into=o_gqa)
            o_dif = _attention("diff", q_dif, k_dif, v_dif, s_lat, rows, extra=dif_extra, ctx=True, into=o_dif)

        xs, h2p = _merge(o_mla, o_gqa, o_dif, gates, xs, mod, w_br_mla[l].astype(BF16), w_br_gqa[l].astype(BF16),
                         w_br_diff[l].astype(BF16), w_o[l].astype(BF16), row2(ln1_g[l]), row2(ln1_b[l]), rows, s_lat)

        xs = _moe(xs, h2p, mod, wr_t, br, moe_w_gate[l], moe_w_up[l], moe_w_down[l],
                  shared_w_gate[l].astype(BF16), shared_w_up[l].astype(BF16), shared_w_down[l].astype(BF16),
                  row2(ln2_g[l]), row2(ln2_b[l]), rows, s_lat)

    return xs[None]
```

```python
import functools
import math

import jax
import jax.numpy as jnp
from jax import lax
from jax.experimental import pallas as pl
from jax.experimental.pallas import tpu as pltpu

F32 = jnp.float32
BF16 = jnp.bfloat16
HIGHEST = lax.Precision.HIGHEST

D_MODEL = 2048
DEPTH = 2
GRID_W = 64
ROPE_THETA = 10000.0
LN_EPS = 1e-5
RMS_EPS = 1e-6
MLA_HEADS = 8
MLA_Q_LORA = 512
MLA_KV_LORA = 512
MLA_NOPE = 128
MLA_ROPE = 64
MLA_V = 128
GQA_HEADS = 8
GQA_KV_HEADS = 2
GQA_HEAD_DIM = 128
DIFF_HEADS = 4
DIFF_HEAD_DIM = 128
DIFF_V_DIM = 2 * DIFF_HEAD_DIM
N_EXPERTS = 16
N_GROUPS = 4
EXPERTS_PER_GROUP = N_EXPERTS // N_GROUPS
MOE_D_FF = 512
ROUTED_SCALE = 1.0
DEEPNORM_ALPHA = (2 * DEPTH) ** 0.25

LANE = 128
MLA_QK_PAD = 2 * LANE
LOG2E = math.log2(math.e)
VMEM_LIMIT = 56 * 1024 * 1024


def _tile(rows, prefs):
    for t in prefs:
        if rows % t == 0:
            return t
    raise ValueError(f"no tile in {prefs} divides {rows}")


def _params(sem, vmem=VMEM_LIMIT, flags=None):
    return pltpu.CompilerParams(dimension_semantics=sem, vmem_limit_bytes=vmem, flags=flags)


def _row_mod(mod_ref, k, row0, tm, s_lat):
    d = D_MODEL
    lat = mod_ref[0:1, k * d:(k + 1) * d]
    ctx = mod_ref[1:2, k * d:(k + 1) * d]
    rows = row0 + lax.broadcasted_iota(jnp.int32, (tm, 1), 0)
    return jnp.where(rows >= s_lat, ctx, lat)


def _rms(x, g):
    return x * lax.rsqrt(jnp.mean(x * x, axis=-1, keepdims=True) + RMS_EPS) * g


def _rope(t, cos, sa, sb, quarter):
    return t * cos + pltpu.roll(t, LANE - quarter, 1) * sa + pltpu.roll(t, quarter, 1) * sb


def _adaln_kernel(c_ref, w_ref, b_ref, o_ref):
    a = c_ref[...]
    a = a * jax.nn.sigmoid(a)
    o_ref[0] = jnp.dot(a, w_ref[0], precision=HIGHEST, preferred_element_type=F32) + b_ref[0]


def _adaln(cc, w_ada, b_ada):
    L, d, n = w_ada.shape
    tn = 1024
    return pl.pallas_call(
        _adaln_kernel,
        grid=(L, n // tn),
        in_specs=[pl.BlockSpec((8, d), lambda l, j: (0, 0)),
                  pl.BlockSpec((1, d, tn), lambda l, j: (l, 0, j)),
                  pl.BlockSpec((1, 1, tn), lambda l, j: (l, 0, j))],
        out_specs=pl.BlockSpec((1, 8, tn), lambda l, j: (l, 0, j)),
        out_shape=jax.ShapeDtypeStruct((L, 8, n), F32),
        compiler_params=_params(("arbitrary", "arbitrary")),
        name="adaln",
    )(cc, w_ada, b_ada.reshape(L, 1, n))


def _mla_proj_kernel(x_ref, mod_ref, wa_ref, qn_ref, kvn_ref, wuq_ref, wukv_ref, cos_ref, sa_ref, sb_ref,
                     h_ref, q_ref, k_ref, v_ref, *, tm, s_lat, q_scale):
    row0 = pl.program_id(0) * tm
    sh = _row_mod(mod_ref, 0, row0, tm, s_lat)
    sc = _row_mod(mod_ref, 1, row0, tm, s_lat)
    h = (x_ref[...] * (1.0 + sc) + sh).astype(BF16)
    h_ref[...] = h
    a = jnp.dot(h, wa_ref[...], preferred_element_type=F32)
    cqn = _rms(a[:, :MLA_Q_LORA], qn_ref[...]).astype(BF16)
    ckvn = _rms(a[:, MLA_Q_LORA:MLA_Q_LORA + MLA_KV_LORA], kvn_ref[...]).astype(BF16)
    kr = a[:, MLA_Q_LORA + MLA_KV_LORA:]
    q = jnp.dot(cqn, wuq_ref[...], preferred_element_type=F32)
    kv = jnp.dot(ckvn, wukv_ref[...], preferred_element_type=F32)
    cos, sa, sb = cos_ref[...], sa_ref[...], sb_ref[...]
    quarter = MLA_ROPE // 4
    kr2 = _rope(kr, cos, sa, sb, quarter).astype(BF16)
    for hh in range(MLA_HEADS):
        c0 = hh * MLA_QK_PAD
        q_ref[:, c0:c0 + LANE] = (q[:, c0:c0 + LANE] * q_scale).astype(BF16)
        q_ref[:, c0 + LANE:c0 + 2 * LANE] = (_rope(q[:, c0 + LANE:c0 + 2 * LANE], cos, sa, sb, quarter)
                                             * q_scale).astype(BF16)
        k_ref[:, c0:c0 + LANE] = kv[:, hh * LANE:(hh + 1) * LANE].astype(BF16)
        k_ref[:, c0 + LANE:c0 + 2 * LANE] = kr2
    v_ref[...] = kv[:, MLA_HEADS * MLA_NOPE:].astype(BF16)


def _mla_proj(xs, mod, wa, qn, kvn, wuq, wukv, tabs, s_lat):
    rows, d = xs.shape
    tm = _tile(rows, (640, 512, 256))
    cos, sa, sb = tabs
    row = lambda i: (i, 0)
    full = lambda i: (0, 0)
    qk_w = MLA_HEADS * MLA_QK_PAD
    v_w = MLA_HEADS * MLA_V
    q_scale = (MLA_NOPE + MLA_ROPE) ** -0.5 * LOG2E
    return pl.pallas_call(
        functools.partial(_mla_proj_kernel, tm=tm, s_lat=s_lat, q_scale=q_scale),
        grid=(rows // tm,),
        in_specs=[pl.BlockSpec((tm, d), row), pl.BlockSpec(mod.shape, full),
                  pl.BlockSpec(wa.shape, full), pl.BlockSpec(qn.shape, full), pl.BlockSpec(kvn.shape, full),
                  pl.BlockSpec(wuq.shape, full), pl.BlockSpec(wukv.shape, full),
                  pl.BlockSpec((tm, LANE), row), pl.BlockSpec((tm, LANE), row), pl.BlockSpec((tm, LANE), row)],
        out_specs=[pl.BlockSpec((tm, d), row), pl.BlockSpec((tm, qk_w), row),
                   pl.BlockSpec((tm, qk_w), row), pl.BlockSpec((tm, v_w), row)],
        out_shape=[jax.ShapeDtypeStruct((rows, d), BF16), jax.ShapeDtypeStruct((rows, qk_w), BF16),
                   jax.ShapeDtypeStruct((rows, qk_w), BF16), jax.ShapeDtypeStruct((rows, v_w), BF16)],
        compiler_params=_params(("parallel",)),
        name="mla_proj",
    )(xs, mod, wa, qn, kvn, wuq, wukv, cos, sa, sb)


def _gqa_proj_kernel(h_ref, w_ref, qn_ref, kn_ref, cos_ref, sa_ref, sb_ref, q_ref, k_ref, v_ref, *, q_scale):
    a = jnp.dot(h_ref[...], w_ref[...], preferred_element_type=F32)
    cos, sa, sb = cos_ref[...], sa_ref[...], sb_ref[...]
    quarter = GQA_HEAD_DIM // 4
    for hh in range(GQA_HEADS):
        x = _rms(a[:, hh * LANE:(hh + 1) * LANE], qn_ref[...])
        q_ref[:, hh * LANE:(hh + 1) * LANE] = (_rope(x, cos, sa, sb, quarter) * q_scale).astype(BF16)
    k0 = GQA_HEADS * GQA_HEAD_DIM
    for hh in range(GQA_KV_HEADS):
        x = _rms(a[:, k0 + hh * LANE:k0 + (hh + 1) * LANE], kn_ref[...])
        k_ref[:, hh * LANE:(hh + 1) * LANE] = _rope(x, cos, sa, sb, quarter).astype(BF16)
    v_ref[...] = a[:, k0 + GQA_KV_HEADS * GQA_HEAD_DIM:].astype(BF16)


def _gqa_proj(h, w, qn, kn, tabs):
    rows, d = h.shape
    tm = _tile(rows, (640, 512, 256))
    cos, sa, sb = tabs
    row = lambda i: (i, 0)
    full = lambda i: (0, 0)
    qw = GQA_HEADS * GQA_HEAD_DIM
    kw = GQA_KV_HEADS * GQA_HEAD_DIM
    return pl.pallas_call(
        functools.partial(_gqa_proj_kernel, q_scale=GQA_HEAD_DIM ** -0.5 * LOG2E),
        grid=(rows // tm,),
        in_specs=[pl.BlockSpec((tm, d), row), pl.BlockSpec(w.shape, full),
                  pl.BlockSpec(qn.shape, full), pl.BlockSpec(kn.shape, full),
                  pl.BlockSpec((tm, LANE), row), pl.BlockSpec((tm, LANE), row), pl.BlockSpec((tm, LANE), row)],
        out_specs=[pl.BlockSpec((tm, qw), row), pl.BlockSpec((tm, kw), row), pl.BlockSpec((tm, kw), row)],
        out_shape=[jax.ShapeDtypeStruct((rows, qw), BF16), jax.ShapeDtypeStruct((rows, kw), BF16),
                   jax.ShapeDtypeStruct((rows, kw), BF16)],
        compiler_params=_params(("parallel",)),
        name="gqa_proj",
    )(h, w, qn, kn, cos, sa, sb)


def _diff_proj_kernel(h_ref, w_ref, cos_ref, sa_ref, sb_ref, q_ref, k_ref, v_ref, *, q_scale):
    a = jnp.dot(h_ref[...], w_ref[...], preferred_element_type=F32)
    cos, sa, sb = cos_ref[...], sa_ref[...], sb_ref[...]
    quarter = DIFF_HEAD_DIM // 4
    n = 2 * DIFF_HEADS
    for hh in range(n):
        q_ref[:, hh * LANE:(hh + 1) * LANE] = (_rope(a[:, hh * LANE:(hh + 1) * LANE], cos, sa, sb, quarter)
                                               * q_scale).astype(BF16)
        k_ref[:, hh * LANE:(hh + 1) * LANE] = _rope(a[:, (n + hh) * LANE:(n + hh + 1) * LANE],
                                                    cos, sa, sb, quarter).astype(BF16)
    v_ref[...] = a[:, 2 * n * LANE:].astype(BF16)


def _diff_proj(h, w, tabs):
    rows, d = h.shape
    tm = _tile(rows, (640, 512, 256))
    cos, sa, sb = tabs
    row = lambda i: (i, 0)
    full = lambda i: (0, 0)
    ww = 2 * DIFF_HEADS * DIFF_HEAD_DIM
    vw = DIFF_HEADS * DIFF_V_DIM
    return pl.pallas_call(
        functools.partial(_diff_proj_kernel, q_scale=DIFF_HEAD_DIM ** -0.5 * LOG2E),
        grid=(rows // tm,),
        in_specs=[pl.BlockSpec((tm, d), row), pl.BlockSpec(w.shape, full),
                  pl.BlockSpec((tm, LANE), row), pl.BlockSpec((tm, LANE), row), pl.BlockSpec((tm, LANE), row)],
        out_specs=[pl.BlockSpec((tm, ww), row), pl.BlockSpec((tm, ww), row), pl.BlockSpec((tm, vw), row)],
        out_shape=[jax.ShapeDtypeStruct((rows, ww), BF16), jax.ShapeDtypeStruct((rows, ww), BF16),
                   jax.ShapeDtypeStruct((rows, vw), BF16)],
        compiler_params=_params(("parallel",)),
        name="diff_proj",
    )(h, w, cos, sa, sb)


def _gate_proj_kernel(h_ref, w_ref, o_ref):
    a = jnp.dot(h_ref[...], w_ref[...], preferred_element_type=F32)
    o_ref[...] = jax.nn.sigmoid(a).astype(BF16)


def _gate_proj(h, w):
    rows, d = h.shape
    n = w.shape[1]
    tm = _tile(rows, (640, 512, 256))
    tn = 1536
    return pl.pallas_call(
        _gate_proj_kernel,
        grid=(rows // tm, n // tn),
        in_specs=[pl.BlockSpec((tm, d), lambda i, j: (i, 0)), pl.BlockSpec((d, tn), lambda i, j: (0, j))],
        out_specs=pl.BlockSpec((tm, tn), lambda i, j: (i, j)),
        out_shape=jax.ShapeDtypeStruct((rows, n), BF16),
        compiler_params=_params(("parallel", "arbitrary")),
        name="gate_proj",
    )(h, w)


ITEMS_PER_TRIP = 2
MXU_DEPTH = 256


def _rows(start, size):
    return pl.ds(start if isinstance(start, int) else pl.multiple_of(start, size), size)


def _dot_nt(a, b):
    return lax.dot_general(a, b, (((1,), (1,)), ((), ())), preferred_element_type=F32)


def _flash(score, v_ref, n_sub, n_chunks, bkv, s_ref, mx_ref, m_ref, l_ref, acc_ref, block_cols=False):
    m_ref[...] = jnp.full(m_ref.shape, -jnp.inf, F32)
    l_ref[...] = jnp.zeros(l_ref.shape, F32)
    acc_ref[...] = jnp.zeros(acc_ref.shape, F32)

    def item(k):
        if isinstance(k, int):
            return k // n_chunks, k % n_chunks
        t = lax.div(k, n_chunks)
        return t, k - t * n_chunks

    def issue(k, slot):
        t, c = item(k)
        score(t, c, slot)

    def absorb(k, slot):
        t, c = item(k)
        m_prev = m_ref[t]
        m_new = jnp.maximum(m_prev, mx_ref[slot])
        alpha = jnp.exp2(m_prev - m_new)
        kt = MXU_DEPTH if (block_cols and bkv % MXU_DEPTH == 0) else bkv
        start = c * bkv
        lsum = jnp.zeros(m_prev.shape, F32)
        acc = alpha * acc_ref[t]
        for j in range(bkv // kt):
            p = jnp.exp2(s_ref[slot, :, j * kt:(j + 1) * kt] - m_new)
            lsum = lsum + jnp.sum(p, axis=1, keepdims=True)
            acc = acc + jnp.dot(p.astype(BF16), v_ref[_rows(start + j * kt, kt), :], preferred_element_type=F32)
        l_ref[t] = alpha * l_ref[t] + lsum
        acc_ref[t] = acc
        m_ref[t] = m_new

    n_items = n_sub * n_chunks
    issue(0, 0)

    def body(j, carry):
        k = ITEMS_PER_TRIP * j
        for i in range(ITEMS_PER_TRIP):
            issue(k + i + 1, (i + 1) % 2)
            absorb(k + i, i % 2)
        return carry

    n_trips = (n_items - 1) // ITEMS_PER_TRIP
    if n_trips:
        lax.fori_loop(0, n_trips, body, 0)
    for k in range(ITEMS_PER_TRIP * n_trips, n_items):
        if k + 1 < n_items:
            issue(k + 1, (k + 1) % 2)
        absorb(k, k % 2)


def _score_into(s_ref, mx_ref, slot, row0, q, kc):
    s = _dot_nt(q, kc)
    rows = q.shape[0]
    s_ref[slot, row0:row0 + rows] = s
    mx_ref[slot, row0:row0 + rows] = jnp.max(s, axis=1, keepdims=True)


def _mla_attn_kernel(q_ref, k_ref, v_ref, *rest, n_sub, n_chunks, bkv):
    o_ref, s_ref, mx_ref, m_ref, l_ref, acc_ref = rest[-6:]
    m_rows = s_ref.shape[1]

    def score(t, c, slot):
        _score_into(s_ref, mx_ref, slot, 0, q_ref[_rows(t * m_rows, m_rows), :], k_ref[_rows(c * bkv, bkv), :])

    _flash(score, v_ref, n_sub, n_chunks, bkv, s_ref, mx_ref, m_ref, l_ref, acc_ref)
    for t in range(n_sub):
        o_ref[t * m_rows:(t + 1) * m_rows, :] = (acc_ref[t] / l_ref[t]).astype(BF16)


def _gqa_attn_kernel(q_ref, k_ref, v_ref, *rest, n_sub, n_chunks, bkv):
    o_ref, qs_ref, s_ref, mx_ref, m_ref, l_ref, acc_ref = rest[-7:]
    g = GQA_HEADS // GQA_KV_HEADS
    bqs = s_ref.shape[1] // g
    for t in range(n_sub):
        for j in range(g):
            qs_ref[t, j * bqs:(j + 1) * bqs] = q_ref[t * bqs:(t + 1) * bqs, j * LANE:(j + 1) * LANE]

    def score(t, c, slot):
        _score_into(s_ref, mx_ref, slot, 0, qs_ref[t], k_ref[_rows(c * bkv, bkv), :])

    _flash(score, v_ref, n_sub, n_chunks, bkv, s_ref, mx_ref, m_ref, l_ref, acc_ref)
    for t in range(n_sub):
        o = acc_ref[t] / l_ref[t]
        for j in range(g):
            o_ref[t * bqs:(t + 1) * bqs, j * LANE:(j + 1) * LANE] = o[j * bqs:(j + 1) * bqs].astype(BF16)


def _diff_attn_kernel(q_ref, k_ref, v_ref, lam_ref, dn_ref, *rest, n_sub, n_chunks, bkv, lam_init):
    o_ref, s_ref, mx_ref, m_ref, l_ref, acc_ref = rest[-6:]
    d = DIFF_HEAD_DIM
    bqs = s_ref.shape[1] // 2
    lp = lam_ref[...]
    lam = (jnp.exp(jnp.sum(lp[0:1] * lp[1:2], axis=1, keepdims=True))
           - jnp.exp(jnp.sum(lp[2:3] * lp[3:4], axis=1, keepdims=True)) + lam_init)

    def score(t, c, slot):
        qr, kr = _rows(t * bqs, bqs), _rows(c * bkv, bkv)
        _score_into(s_ref, mx_ref, slot, 0, q_ref[qr, 0:d], k_ref[kr, 0:d])
        _score_into(s_ref, mx_ref, slot, bqs, q_ref[qr, d:2 * d], k_ref[kr, d:2 * d])

    _flash(score, v_ref, n_sub, n_chunks, bkv, s_ref, mx_ref, m_ref, l_ref, acc_ref, block_cols=True)
    for t in range(n_sub):
        o = acc_ref[t] / l_ref[t]
        o = o[:bqs] - lam * o[bqs:]
        o_ref[t * bqs:(t + 1) * bqs, :] = (_rms(o, dn_ref[...]) * (1.0 - lam_init)).astype(BF16)


def _attention(kind, q, k, v, s_lat, out_rows, extra=(), ctx=False, into=None):
    s_all = q.shape[0]
    n_ctx = s_all - s_lat
    assert s_lat % n_ctx == 0
    if kind == "mla":
        heads, qw, kw, vw, ow, stack = MLA_HEADS, MLA_QK_PAD, MLA_QK_PAD, MLA_V, MLA_V, 1
        bqs = n_ctx if ctx else _tile(s_lat, (1024, 512, 256))
        body = _mla_attn_kernel
    elif kind == "gqa":
        g = GQA_HEADS // GQA_KV_HEADS
        heads, qw, kw, vw, ow, stack = GQA_KV_HEADS, g * LANE, LANE, LANE, g * LANE, g
        bqs = n_ctx if ctx else 256
        body = _gqa_attn_kernel
    else:
        heads, qw, kw, vw, ow, stack = DIFF_HEADS, 2 * LANE, 2 * LANE, DIFF_V_DIM, DIFF_V_DIM, 2
        bqs = n_ctx if ctx else _tile(s_lat, (512, 256))
        body = functools.partial(_diff_attn_kernel, lam_init=extra[2])
        extra = extra[:2]
    m_rows = stack * bqs
    n_sub = 1 if ctx else _tile(s_lat // bqs, (4, 3, 2, 1))
    bq = n_sub * bqs
    if ctx:
        kv_rows, bkv, row_blk, nq = n_ctx, n_ctx, s_lat // n_ctx, 1
    else:
        kv_rows, bkv, row_blk, nq = s_all, _tile(s_all, (1280, 1024, 512, 256)), 0, s_lat // bq
    body = functools.partial(body, n_sub=n_sub, n_chunks=kv_rows // bkv, bkv=bkv)
    kv_blk = row_blk
    in_specs = [pl.BlockSpec((bq, qw), lambda h, i: (row_blk + i, h)),
                pl.BlockSpec((kv_rows, kw), lambda h, i: (kv_blk, h), pipeline_mode=pl.Buffered(1)),
                pl.BlockSpec((kv_rows, vw), lambda h, i: (kv_blk, h), pipeline_mode=pl.Buffered(1))]
    in_specs += [pl.BlockSpec(e.shape, lambda h, i: (0, 0)) for e in extra]
    args = [q, k, v, *extra]
    aliases = {}
    if into is not None:
        in_specs.append(pl.BlockSpec(memory_space=pl.ANY))
        aliases = {len(args): 0}
        args.append(into)
    scratch = []
    if kind == "gqa":
        scratch.append(pltpu.VMEM((n_sub, m_rows, LANE), BF16))
    scratch += [pltpu.VMEM((2, m_rows, bkv), F32),
                pltpu.VMEM((2, m_rows, 1), F32), pltpu.VMEM((n_sub, m_rows, 1), F32),
                pltpu.VMEM((n_sub, m_rows, 1), F32), pltpu.VMEM((n_sub, m_rows, vw), F32)]
    return pl.pallas_call(
        body,
        grid=(heads, nq),
        in_specs=in_specs,
        out_specs=pl.BlockSpec((bq, ow), lambda h, i: (row_blk + i, h)),
        out_shape=jax.ShapeDtypeStruct((out_rows, heads * ow), BF16),
        scratch_shapes=scratch,
        input_output_aliases=aliases,
        compiler_params=_params(("parallel", "arbitrary")),
        name=kind + ("_attn_ctx" if ctx else "_attn"),
    )(*args)


def _layer_norm(z, g, b):
    mu = jnp.mean(z, axis=-1, keepdims=True)
    zc = z - mu
    var = jnp.mean(zc * zc, axis=-1, keepdims=True)
    return zc * lax.rsqrt(var + LN_EPS) * g + b


def _merge_kernel(om_ref, og_ref, od_ref, g_ref, x_ref, mod_ref, wbm_ref, wbg_ref, wbd_ref, wo_ref,
                  lng_ref, lnb_ref, xo_ref, h2_ref, *, tm, s_lat):
    d = D_MODEL
    row0 = pl.program_id(0) * tm
    merged = g_ref[:, 0:d].astype(F32) * jnp.dot(om_ref[...], wbm_ref[...], preferred_element_type=F32)
    merged += g_ref[:, d:2 * d].astype(F32) * jnp.dot(og_ref[...], wbg_ref[...], preferred_element_type=F32)
    merged += g_ref[:, 2 * d:3 * d].astype(F32) * jnp.dot(od_ref[...], wbd_ref[...], preferred_element_type=F32)
    y = jnp.dot(merged.astype(BF16), wo_ref[...], preferred_element_type=F32)
    z = DEEPNORM_ALPHA * x_ref[...] + _row_mod(mod_ref, 2, row0, tm, s_lat) * y
    xn = _layer_norm(z, lng_ref[...], lnb_ref[...])
    xo_ref[...] = xn
    h2 = xn * (1.0 + _row_mod(mod_ref, 4, row0, tm, s_lat)) + _row_mod(mod_ref, 3, row0, tm, s_lat)
    h2_ref[...] = _pack_pairs(h2)


def _merge(om, og, od, gates, xs, mod, wbm, wbg, wbd, wo, lng, lnb, rows, s_lat):
    d = D_MODEL
    tm = 256
    row = lambda i: (i, 0)
    full = lambda i: (0, 0)
    resident = lambda a: pl.BlockSpec(a.shape, full, pipeline_mode=pl.Buffered(1))
    return pl.pallas_call(
        functools.partial(_merge_kernel, tm=tm, s_lat=s_lat),
        grid=(rows // tm,),
        in_specs=[pl.BlockSpec((tm, om.shape[1]), row), pl.BlockSpec((tm, og.shape[1]), row),
                  pl.BlockSpec((tm, od.shape[1]), row), pl.BlockSpec((tm, gates.shape[1]), row),
                  pl.BlockSpec((tm, d), row), pl.BlockSpec(mod.shape, full),
                  resident(wbm), resident(wbg), resident(wbd), resident(wo),
                  pl.BlockSpec(lng.shape, full), pl.BlockSpec(lnb.shape, full)],
        out_specs=[pl.BlockSpec((tm, d), row), pl.BlockSpec((tm, d // 2), row)],
        out_shape=[jax.ShapeDtypeStruct((rows, d), F32), jax.ShapeDtypeStruct((rows, d // 2), jnp.uint32)],
        compiler_params=_params(("parallel",)),
        name="merge",
    )(om, og, od, gates, xs, mod, wbm, wbg, wbd, wo, lng, lnb)


def _router_kernel(x_ref, mod_ref, wr_ref, br_ref, ri_ref, rw_ref, cnt_ref, run_ref, *, tm, s_lat):
    @pl.when(pl.program_id(0) == 0)
    def _():
        run_ref[...] = jnp.zeros(run_ref.shape, F32)

    row0 = pl.program_id(0) * tm
    h2 = x_ref[...] * (1.0 + _row_mod(mod_ref, 4, row0, tm, s_lat)) + _row_mod(mod_ref, 3, row0, tm, s_lat)
    logits = lax.dot_general(wr_ref[...], h2, (((1,), (1,)), ((), ())),
                             precision=HIGHEST, preferred_element_type=F32)
    scores = jax.nn.sigmoid(logits)
    biased = scores + br_ref[...]
    sc = [scores[e:e + 1, :] for e in range(N_EXPERTS)]
    bi = [biased[e:e + 1, :] for e in range(N_EXPERTS)]
    gs = []
    for g in range(N_GROUPS):
        a, b, c, dd = bi[4 * g:4 * g + 4]
        hi1, lo1, hi2, lo2 = jnp.maximum(a, b), jnp.minimum(a, b), jnp.maximum(c, dd), jnp.minimum(c, dd)
        gs.append(jnp.maximum(hi1, hi2) + jnp.maximum(jnp.minimum(hi1, hi2), jnp.maximum(lo1, lo2)))
    best = jnp.maximum(jnp.maximum(gs[0], gs[1]), jnp.maximum(gs[2], gs[3]))
    gsel = jnp.where(gs[0] == best, 0, jnp.where(gs[1] == best, 1, jnp.where(gs[2] == best, 2, 3)))
    pick = lambda vals, j: jnp.where(gsel == 0, vals[j], jnp.where(gsel == 1, vals[4 + j],
                                     jnp.where(gsel == 2, vals[8 + j], vals[12 + j])))
    xb = [pick(bi, j) for j in range(EXPERTS_PER_GROUP)]
    xs = [pick(sc, j) for j in range(EXPERTS_PER_GROUP)]
    sel, w = [], []
    for i in range(EXPERTS_PER_GROUP):
        rank = jnp.zeros_like(gsel)
        for j in range(EXPERTS_PER_GROUP):
            if j == i:
                continue
            beats = (xb[j] >= xb[i]) if j < i else (xb[j] > xb[i])
            rank = rank + beats.astype(jnp.int32)
        sel.append(rank < 2)
        w.append(jnp.where(rank < 2, xs[i], 0.0))
    inv = ROUTED_SCALE / (w[0] + w[1] + w[2] + w[3])
    ia = jnp.where(sel[0], 0, jnp.where(sel[1], 1, jnp.where(sel[2], 2, 3)))
    ib = jnp.where(sel[3], 3, jnp.where(sel[2], 2, jnp.where(sel[1], 1, 0)))
    local = lambda idx: jnp.where(idx == 0, w[0], jnp.where(idx == 1, w[1], jnp.where(idx == 2, w[2], w[3])))
    ea = gsel * EXPERTS_PER_GROUP + ia
    eb = gsel * EXPERTS_PER_GROUP + ib
    eidx = lax.broadcasted_iota(jnp.int32, (N_EXPERTS, tm), 0)
    is_a, is_b = eidx == ea, eidx == eb
    onehot = jnp.where(is_a | is_b, 1.0, 0.0)
    earlier = jnp.where(lax.broadcasted_iota(jnp.int32, (tm, tm), 0) < lax.broadcasted_iota(jnp.int32, (tm, tm), 1),
                        1.0, 0.0).astype(BF16)
    pos = run_ref[...] + jnp.dot(onehot.astype(BF16), earlier, preferred_element_type=F32)
    ri_ref[0:1, :] = ea
    ri_ref[1:2, :] = eb
    ri_ref[2:3, :] = jnp.sum(jnp.where(is_a, pos, 0.0), axis=0, keepdims=True).astype(jnp.int32)
    ri_ref[3:4, :] = jnp.sum(jnp.where(is_b, pos, 0.0), axis=0, keepdims=True).astype(jnp.int32)
    rw_ref[0:1, :] = local(ia) * inv
    rw_ref[1:2, :] = local(ib) * inv
    run = run_ref[...] + jnp.sum(onehot, axis=1, keepdims=True)
    run_ref[...] = run
    cnt_ref[...] = jnp.broadcast_to(run, cnt_ref.shape)


def _router(xs, mod, wr_t, br, rows, s_lat):
    d = D_MODEL
    tm = _tile(rows, (640, 512, 256))
    return pl.pallas_call(
        functools.partial(_router_kernel, tm=tm, s_lat=s_lat),
        grid=(rows // tm,),
        in_specs=[pl.BlockSpec((tm, d), lambda i: (i, 0)), pl.BlockSpec(mod.shape, lambda i: (0, 0)),
                  pl.BlockSpec(wr_t.shape, lambda i: (0, 0)), pl.BlockSpec(br.shape, lambda i: (0, 0))],
        out_specs=[pl.BlockSpec((4, tm), lambda i: (0, i)), pl.BlockSpec((2, tm), lambda i: (0, i)),
                   pl.BlockSpec((N_EXPERTS, LANE), lambda i: (0, 0))],
        out_shape=[jax.ShapeDtypeStruct((4, rows), jnp.int32), jax.ShapeDtypeStruct((2, rows), F32),
                   jax.ShapeDtypeStruct((N_EXPERTS, LANE), F32)],
        scratch_shapes=[pltpu.VMEM((N_EXPERTS, 1), F32)],
        compiler_params=_params(("arbitrary",)),
        name="router",
    )(xs, mod, wr_t, br)


MOE_TILE = 512
HALF = D_MODEL // 2


def _pack_pairs(h):
    hi = pltpu.bitcast(h[:, :HALF].astype(BF16).astype(F32), jnp.uint32)
    lo = pltpu.bitcast(h[:, HALF:].astype(BF16).astype(F32), jnp.uint32)
    return (hi & jnp.uint32(0xFFFF0000)) | (lo >> jnp.uint32(16))


def _unpack_pairs(w):
    hi = pltpu.bitcast(w & jnp.uint32(0xFFFF0000), F32)
    lo = pltpu.bitcast(w << jnp.uint32(16), F32)
    return jnp.concatenate([hi, lo], axis=1).astype(BF16)


def _swiglu(x, wg, wu, wd):
    a = jnp.dot(x, wg, preferred_element_type=F32)
    b = jnp.dot(x, wu, preferred_element_type=F32)
    return jnp.dot((a * jax.nn.sigmoid(a) * b).astype(BF16), wd, preferred_element_type=F32)


def _moe_sort_kernel(dest_ref, h_ref, init_ref, o_ref, sem, *, tg):
    del init_ref
    base = pl.program_id(0) * tg

    def copy(r, j):
        return pltpu.make_async_copy(h_ref.at[pl.ds(r, 1)], o_ref.at[pl.ds(dest_ref[j, base + r], 1)], sem)

    def start(r, carry):
        for j in range(2):
            copy(r, j).start()
        return carry

    def wait(r, carry):
        for j in range(2):
            copy(r, j).wait()
        return carry

    lax.fori_loop(0, tg, start, 0)
    lax.fori_loop(0, tg, wait, 0)


def _moe_sort(dest, h2p, n_sorted):
    rows, half = h2p.shape
    tg = _tile(rows, (1280, 1024, 512, 256))
    return pl.pallas_call(
        functools.partial(_moe_sort_kernel, tg=tg),
        grid_spec=pltpu.PrefetchScalarGridSpec(
            num_scalar_prefetch=1, grid=(rows // tg,),
            in_specs=[pl.BlockSpec((tg, half), lambda i, dref: (i, 0)), pl.BlockSpec(memory_space=pl.ANY)],
            out_specs=pl.BlockSpec(memory_space=pl.ANY),
            scratch_shapes=[pltpu.SemaphoreType.DMA]),
        out_shape=jax.ShapeDtypeStruct((n_sorted, half), jnp.uint32),
        input_output_aliases={2: 0},
        compiler_params=_params(("arbitrary",)),
        name="moe_sort",
    )(dest, h2p, jnp.zeros((n_sorted, half), jnp.uint32))


def _moe_expert_kernel(te_ref, na_ref, x_ref, wg_ref, wu_ref, wd_ref, y_ref):
    del te_ref
    live = pl.program_id(0) < na_ref[0]

    @pl.when(live)
    def _():
        y_ref[...] = _swiglu(_unpack_pairs(x_ref[...]), wg_ref[0].astype(BF16), wu_ref[0].astype(BF16),
                             wd_ref[0].astype(BF16))

    @pl.when(jnp.logical_not(live))
    def _():
        y_ref[...] = jnp.zeros(y_ref.shape, F32)


def _moe_experts(tile_expert, n_active, xsorted, wg, wu, wd):
    n_sorted, half = xsorted.shape
    _, d, f = wg.shape
    tme = MOE_TILE
    wsel = lambda i, te, na: (te[i], 0, 0)
    return pl.pallas_call(
        _moe_expert_kernel,
        grid_spec=pltpu.PrefetchScalarGridSpec(
            num_scalar_prefetch=2, grid=(n_sorted // tme,),
            in_specs=[pl.BlockSpec((tme, half), lambda i, te, na: (i, 0)),
                      pl.BlockSpec((1, d, f), wsel), pl.BlockSpec((1, d, f), wsel), pl.BlockSpec((1, f, d), wsel)],
            out_specs=pl.BlockSpec((tme, d), lambda i, te, na: (i, 0))),
        out_shape=jax.ShapeDtypeStruct((n_sorted, d), F32),
        compiler_params=_params(("arbitrary",)),
        name="moe_experts",
    )(tile_expert, n_active, xsorted, wg, wu, wd)


def _moe_final_kernel(dest_ref, hp_ref, rw_ref, y_ref, sg_ref, su_ref, sd_ref, x_ref, mod_ref, lng_ref, lnb_ref,
                      o_ref, ya_ref, yb_ref, sem, *, tm, s_lat):
    base = pl.program_id(0) * tm
    bufs = (ya_ref, yb_ref)

    def copy(r, j):
        return pltpu.make_async_copy(y_ref.at[pl.ds(dest_ref[j, base + r], 1)], bufs[j].at[pl.ds(r, 1)], sem.at[j])

    def start(r, carry):
        for j in range(2):
            copy(r, j).start()
        return carry

    def wait(r, carry):
        for j in range(2):
            copy(r, j).wait()
        return carry

    lax.fori_loop(0, tm, start, 0)
    shared = _swiglu(_unpack_pairs(hp_ref[...]), sg_ref[...], su_ref[...], sd_ref[...])
    lax.fori_loop(0, tm, wait, 0)
    y = shared + rw_ref[:, 0:1] * ya_ref[...] + rw_ref[:, 1:2] * yb_ref[...]
    z = DEEPNORM_ALPHA * x_ref[...] + _row_mod(mod_ref, 5, base, tm, s_lat) * y
    o_ref[...] = _layer_norm(z, lng_ref[...], lnb_ref[...])


def _moe_final(dest, h2p, rw, ysorted, sg, su, sd, xs, mod, lng, lnb, rows, s_lat):
    d = D_MODEL
    tm = 256
    row = lambda i, dref: (i, 0)
    full = lambda i, dref: (0, 0)
    return pl.pallas_call(
        functools.partial(_moe_final_kernel, tm=tm, s_lat=s_lat),
        grid_spec=pltpu.PrefetchScalarGridSpec(
            num_scalar_prefetch=1, grid=(rows // tm,),
            in_specs=[pl.BlockSpec((tm, HALF), row), pl.BlockSpec((tm, 2), row), pl.BlockSpec(memory_space=pl.ANY),
                      pl.BlockSpec(sg.shape, full), pl.BlockSpec(su.shape, full), pl.BlockSpec(sd.shape, full),
                      pl.BlockSpec((tm, d), row), pl.BlockSpec(mod.shape, full),
                      pl.BlockSpec(lng.shape, full), pl.BlockSpec(lnb.shape, full)],
            out_specs=pl.BlockSpec((tm, d), row),
            scratch_shapes=[pltpu.VMEM((tm, d), F32), pltpu.VMEM((tm, d), F32), pltpu.SemaphoreType.DMA((2,))]),
        out_shape=jax.ShapeDtypeStruct((rows, d), F32),
        compiler_params=_params(("arbitrary",)),
        name="moe_final",
    )(dest, h2p, rw, ysorted, sg, su, sd, xs, mod, lng, lnb)


def _moe_plan(ri, cnt, rows):
    tme = MOE_TILE
    n_sorted = 2 * rows + N_EXPERTS * tme
    counts = cnt[:, 0].astype(jnp.int32)
    padded = (counts + tme - 1) // tme * tme
    ends = jnp.cumsum(padded)
    offs = ends - padded
    dest = jnp.stack([offs[ri[0]] + ri[2], offs[ri[1]] + ri[3]])
    tile_start = jnp.arange(n_sorted // tme, dtype=jnp.int32) * tme
    tile_expert = jnp.minimum(jnp.sum((ends[None, :] <= tile_start[:, None]).astype(jnp.int32), axis=1),
                              N_EXPERTS - 1)
    n_active = (ends[-1:] // tme).astype(jnp.int32)
    return dest, tile_expert, n_active, n_sorted


def _moe(xs, h2p, mod, wr_t, br, wg, wu, wd, sg, su, sd, lng, lnb, rows, s_lat):
    ri, rw, cnt = _router(xs, mod, wr_t, br, rows, s_lat)
    dest, tile_expert, n_active, n_sorted = _moe_plan(ri, cnt, rows)
    xsorted = _moe_sort(dest, h2p, n_sorted)
    ysorted = _moe_experts(tile_expert, n_active, xsorted, wg, wu, wd)
    return _moe_final(dest, h2p, rw.T, ysorted, sg, su, sd, xs, mod, lng, lnb, rows, s_lat)


def _rope_tables(s_lat, n_ctx, dim):
    quarter = dim // 4
    rows = s_lat // GRID_W
    inv_freq = ROPE_THETA ** (-jnp.arange(quarter, dtype=F32) / quarter)
    ang_r = jnp.arange(rows, dtype=F32)[:, None] * inv_freq
    ang_c = jnp.arange(GRID_W, dtype=F32)[:, None] * inv_freq
    per_row = lambda t: jnp.repeat(t, GRID_W, axis=0)
    per_col = lambda t: jnp.tile(t, (rows, 1))
    cos_r, sin_r = per_row(jnp.cos(ang_r)), per_row(jnp.sin(ang_r))
    cos_c, sin_c = per_col(jnp.cos(ang_c)), per_col(jnp.sin(ang_c))
    zero = jnp.zeros_like(cos_r)
    pad = jnp.zeros((s_lat, LANE - dim), F32)
    cos = jnp.concatenate([cos_r, cos_r, cos_c, cos_c, pad + 1.0], axis=1)
    sa = jnp.concatenate([-sin_r, zero, -sin_c, zero, pad], axis=1)
    sb = jnp.concatenate([zero, sin_r, zero, sin_c, pad], axis=1)
    ctx1 = jnp.ones((n_ctx, LANE), F32)
    ctx0 = jnp.zeros((n_ctx, LANE), F32)
    return (jnp.concatenate([cos, ctx1], axis=0), jnp.concatenate([sa, ctx0], axis=0),
            jnp.concatenate([sb, ctx0], axis=0))


def kernel(x, c, ctx, c_ctx, w_ada, b_ada, w_in, mla_q_norm, mla_w_uq, mla_kv_norm, mla_w_ukv, gqa_q_norm,
           gqa_k_norm, diff_lambda, diff_norm, w_br_mla, w_br_gqa, w_br_diff, w_o, ln1_g, ln1_b, w_router,
           b_router, moe_w_gate, moe_w_up, moe_w_down, shared_w_gate, shared_w_up, shared_w_down, ln2_g, ln2_b):
    B, s_lat, d = x.shape
    n_ctx = ctx.shape[1]
    assert B == 1 and d == D_MODEL
    s_all = s_lat + n_ctx

    xs = jnp.concatenate([x[0], ctx[0]], axis=0)
    cc = jnp.zeros((8, d), F32).at[0].set(c[0]).at[1].set(c_ctx)
    mod_all = _adaln(cc, w_ada, b_ada)

    tabs64 = _rope_tables(s_lat, n_ctx, MLA_ROPE)
    tabs128 = _rope_tables(s_lat, n_ctx, GQA_HEAD_DIM)
    wr_t = w_router.T
    br = b_router.reshape(N_EXPERTS, 1)
    row2 = lambda v: v.reshape(1, -1)

    o0 = MLA_Q_LORA + MLA_KV_LORA + MLA_ROPE
    o1 = o0 + GQA_HEADS * GQA_HEAD_DIM + 2 * GQA_KV_HEADS * GQA_HEAD_DIM
    o2 = o1 + 2 * (2 * DIFF_HEADS * DIFF_HEAD_DIM) + DIFF_HEADS * DIFF_V_DIM

    for l in range(DEPTH):
        last = l == DEPTH - 1
        mod = mod_all[l]
        w = w_in[l]
        wa = jnp.pad(w[:, :o0], ((0, 0), (0, LANE - MLA_ROPE))).astype(BF16)
        wb = w[:, o0:o1].astype(BF16)
        wc = w[:, o1:o2].astype(BF16)
        wd_gate = w[:, o2:].astype(BF16)
        wuq = jnp.pad(mla_w_uq[l].reshape(MLA_Q_LORA, MLA_HEADS, MLA_NOPE + MLA_ROPE),
                      ((0, 0), (0, 0), (0, MLA_QK_PAD - MLA_NOPE - MLA_ROPE))
                      ).reshape(MLA_Q_LORA, MLA_HEADS * MLA_QK_PAD).astype(BF16)
        ukv = mla_w_ukv[l].reshape(MLA_KV_LORA, MLA_HEADS, MLA_NOPE + MLA_V)
        wukv = jnp.concatenate([ukv[:, :, :MLA_NOPE].reshape(MLA_KV_LORA, -1),
                                ukv[:, :, MLA_NOPE:].reshape(MLA_KV_LORA, -1)], axis=1).astype(BF16)

        h, q_mla, k_mla, v_mla = _mla_proj(xs, mod, wa, row2(mla_q_norm[l]), row2(mla_kv_norm[l]),
                                           wuq, wukv, tabs64, s_lat)
        q_gqa, k_gqa, v_gqa = _gqa_proj(h, wb, row2(gqa_q_norm[l]), row2(gqa_k_norm[l]), tabs128)
        q_dif, k_dif, v_dif = _diff_proj(h, wc, tabs128)
        gates = _gate_proj(h, wd_gate)

        lam_init = 0.8 - 0.6 * math.exp(-0.3 * l)
        rows = s_lat if last else s_all
        dif_extra = (diff_lambda[l], row2(diff_norm[l]), lam_init)
        fresh = lambda width: None if last else jnp.zeros((rows, width), BF16)
        o_mla = _attention("mla", q_mla, k_mla, v_mla, s_lat, rows, into=fresh(MLA_HEADS * MLA_V))
        o_gqa = _attention("gqa", q_gqa, k_gqa, v_gqa, s_lat, rows, into=fresh(GQA_HEADS * GQA_HEAD_DIM))
        o_dif = _attention("diff", q_dif, k_dif, v_dif, s_lat, rows, extra=dif_extra,
                           into=fresh(DIFF_HEADS * DIFF_V_DIM))
        if not last:
            o_mla = _attention("mla", q_mla, k_mla, v_mla, s_lat, rows, ctx=True, into=o_mla)
            o_gqa = _attention("gqa", q_gqa, k_gqa, v_gqa, s_lat, rows, ctx=True, into=o_gqa)
            o_dif = _attention("diff", q_dif, k_dif, v_dif, s_lat, rows, extra=dif_extra, ctx=True, into=o_dif)

        xs, h2p = _merge(o_mla, o_gqa, o_dif, gates, xs, mod, w_br_mla[l].astype(BF16), w_br_gqa[l].astype(BF16),
                         w_br_diff[l].astype(BF16), w_o[l].astype(BF16), row2(ln1_g[l]), row2(ln1_b[l]), rows, s_lat)

        xs = _moe(xs, h2p, mod, wr_t, br, moe_w_gate[l], moe_w_up[l], moe_w_down[l],
                  shared_w_gate[l].astype(BF16), shared_w_up[l].astype(BF16), shared_w_down[l].astype(BF16),
                  row2(ln2_g[l]), row2(ln2_b[l]), rows, s_lat)

    return xs[None]
```

```python
import functools
import math

import jax
import jax.numpy as jnp
from jax import lax
from jax.experimental import pallas as pl
from jax.experimental.pallas import tpu as pltpu

F32 = jnp.float32
BF16 = jnp.bfloat16
HIGHEST = lax.Precision.HIGHEST

D_MODEL = 2048
DEPTH = 2
GRID_W = 64
ROPE_THETA = 10000.0
LN_EPS = 1e-5
RMS_EPS = 1e-6
MLA_HEADS = 8
MLA_Q_LORA = 512
MLA_KV_LORA = 512
MLA_NOPE = 128
MLA_ROPE = 64
MLA_V = 128
GQA_HEADS = 8
GQA_KV_HEADS = 2
GQA_HEAD_DIM = 128
DIFF_HEADS = 4
DIFF_HEAD_DIM = 128
DIFF_V_DIM = 2 * DIFF_HEAD_DIM
N_EXPERTS = 16
N_GROUPS = 4
EXPERTS_PER_GROUP = N_EXPERTS // N_GROUPS
MOE_D_FF = 512
ROUTED_SCALE = 1.0
DEEPNORM_ALPHA = (2 * DEPTH) ** 0.25

LANE = 128
MLA_QK_PAD = 2 * LANE
LOG2E = math.log2(math.e)
VMEM_LIMIT = 56 * 1024 * 1024


def _tile(rows, prefs):
    for t in prefs:
        if rows % t == 0:
            return t
    raise ValueError(f"no tile in {prefs} divides {rows}")


def _params(sem, vmem=VMEM_LIMIT, flags=None):
    return pltpu.CompilerParams(dimension_semantics=sem, vmem_limit_bytes=vmem, flags=flags)


def _row_mod(mod_ref, k, row0, tm, s_lat):
    d = D_MODEL
    lat = mod_ref[0:1, k * d:(k + 1) * d]
    ctx = mod_ref[1:2, k * d:(k + 1) * d]
    rows = row0 + lax.broadcasted_iota(jnp.int32, (tm, 1), 0)
    return jnp.where(rows >= s_lat, ctx, lat)


def _rms(x, g):
    return x * lax.rsqrt(jnp.mean(x * x, axis=-1, keepdims=True) + RMS_EPS) * g


def _rope(t, cos, sa, sb, quarter):
    return t * cos + pltpu.roll(t, LANE - quarter, 1) * sa + pltpu.roll(t, quarter, 1) * sb


def _adaln_kernel(c_ref, w_ref, b_ref, o_ref):
    a = c_ref[...]
    a = a * jax.nn.sigmoid(a)
    o_ref[0] = jnp.dot(a, w_ref[0], precision=HIGHEST, preferred_element_type=F32) + b_ref[0]


def _adaln(cc, w_ada, b_ada):
    L, d, n = w_ada.shape
    tn = 1024
    return pl.pallas_call(
        _adaln_kernel,
        grid=(L, n // tn),
        in_specs=[pl.BlockSpec((8, d), lambda l, j: (0, 0)),
                  pl.BlockSpec((1, d, tn), lambda l, j: (l, 0, j)),
                  pl.BlockSpec((1, 1, tn), lambda l, j: (l, 0, j))],
        out_specs=pl.BlockSpec((1, 8, tn), lambda l, j: (l, 0, j)),
        out_shape=jax.ShapeDtypeStruct((L, 8, n), F32),
        compiler_params=_params(("arbitrary", "arbitrary")),
        name="adaln",
    )(cc, w_ada, b_ada.reshape(L, 1, n))


def _mla_proj_kernel(x_ref, mod_ref, wa_ref, qn_ref, kvn_ref, wuq_ref, wukv_ref, cos_ref, sa_ref, sb_ref,
                     h_ref, q_ref, k_ref, v_ref, *, tm, s_lat, q_scale):
    row0 = pl.program_id(0) * tm
    sh = _row_mod(mod_ref, 0, row0, tm, s_lat)
    sc = _row_mod(mod_ref, 1, row0, tm, s_lat)
    h = (x_ref[...] * (1.0 + sc) + sh).astype(BF16)
    h_ref[...] = h
    a = jnp.dot(h, wa_ref[...], preferred_element_type=F32)
    cqn = _rms(a[:, :MLA_Q_LORA], qn_ref[...]).astype(BF16)
    ckvn = _rms(a[:, MLA_Q_LORA:MLA_Q_LORA + MLA_KV_LORA], kvn_ref[...]).astype(BF16)
    kr = a[:, MLA_Q_LORA + MLA_KV_LORA:]
    q = jnp.dot(cqn, wuq_ref[...], preferred_element_type=F32)
    kv = jnp.dot(ckvn, wukv_ref[...], preferred_element_type=F32)
    cos, sa, sb = cos_ref[...], sa_ref[...], sb_ref[...]
    quarter = MLA_ROPE // 4
    kr2 = _rope(kr, cos, sa, sb, quarter).astype(BF16)
    for hh in range(MLA_HEADS):
        c0 = hh * MLA_QK_PAD
        q_ref[:, c0:c0 + LANE] = (q[:, c0:c0 + LANE] * q_scale).astype(BF16)
        q_ref[:, c0 + LANE:c0 + 2 * LANE] = (_rope(q[:, c0 + LANE:c0 + 2 * LANE], cos, sa, sb, quarter)
                                             * q_scale).astype(BF16)
        k_ref[:, c0:c0 + LANE] = kv[:, hh * LANE:(hh + 1) * LANE].astype(BF16)
        k_ref[:, c0 + LANE:c0 + 2 * LANE] = kr2
    v_ref[...] = kv[:, MLA_HEADS * MLA_NOPE:].astype(BF16)


def _mla_proj(xs, mod, wa, qn, kvn, wuq, wukv, tabs, s_lat):
    rows, d = xs.shape
    tm = _tile(rows, (640, 512, 256))
    cos, sa, sb = tabs
    row = lambda i: (i, 0)
    full = lambda i: (0, 0)
    qk_w = MLA_HEADS * MLA_QK_PAD
    v_w = MLA_HEADS * MLA_V
    q_scale = (MLA_NOPE + MLA_ROPE) ** -0.5 * LOG2E
    return pl.pallas_call(
        functools.partial(_mla_proj_kernel, tm=tm, s_lat=s_lat, q_scale=q_scale),
        grid=(rows // tm,),
        in_specs=[pl.BlockSpec((tm, d), row), pl.BlockSpec(mod.shape, full),
                  pl.BlockSpec(wa.shape, full), pl.BlockSpec(qn.shape, full), pl.BlockSpec(kvn.shape, full),
                  pl.BlockSpec(wuq.shape, full), pl.BlockSpec(wukv.shape, full),
                  pl.BlockSpec((tm, LANE), row), pl.BlockSpec((tm, LANE), row), pl.BlockSpec((tm, LANE), row)],
        out_specs=[pl.BlockSpec((tm, d), row), pl.BlockSpec((tm, qk_w), row),
                   pl.BlockSpec((tm, qk_w), row), pl.BlockSpec((tm, v_w), row)],
        out_shape=[jax.ShapeDtypeStruct((rows, d), BF16), jax.ShapeDtypeStruct((rows, qk_w), BF16),
                   jax.ShapeDtypeStruct((rows, qk_w), BF16), jax.ShapeDtypeStruct((rows, v_w), BF16)],
        compiler_params=_params(("parallel",)),
        name="mla_proj",
    )(xs, mod, wa, qn, kvn, wuq, wukv, cos, sa, sb)


def _gqa_proj_kernel(h_ref, w_ref, qn_ref, kn_ref, cos_ref, sa_ref, sb_ref, q_ref, k_ref, v_ref, *, q_scale):
    a = jnp.dot(h_ref[...], w_ref[...], preferred_element_type=F32)
    cos, sa, sb = cos_ref[...], sa_ref[...], sb_ref[...]
    quarter = GQA_HEAD_DIM // 4
    for hh in range(GQA_HEADS):
        x = _rms(a[:, hh * LANE:(hh + 1) * LANE], qn_ref[...])
        q_ref[:, hh * LANE:(hh + 1) * LANE] = (_rope(x, cos, sa, sb, quarter) * q_scale).astype(BF16)
    k0 = GQA_HEADS * GQA_HEAD_DIM
    for hh in range(GQA_KV_HEADS):
        x = _rms(a[:, k0 + hh * LANE:k0 + (hh + 1) * LANE], kn_ref[...])
        k_ref[:, hh * LANE:(hh + 1) * LANE] = _rope(x, cos, sa, sb, quarter).astype(BF16)
    v_ref[...] = a[:, k0 + GQA_KV_HEADS * GQA_HEAD_DIM:].astype(BF16)


def _gqa_proj(h, w, qn, kn, tabs):
    rows, d = h.shape
    tm = _tile(rows, (640, 512, 256))
    cos, sa, sb = tabs
    row = lambda i: (i, 0)
    full = lambda i: (0, 0)
    qw = GQA_HEADS * GQA_HEAD_DIM
    kw = GQA_KV_HEADS * GQA_HEAD_DIM
    return pl.pallas_call(
        functools.partial(_gqa_proj_kernel, q_scale=GQA_HEAD_DIM ** -0.5 * LOG2E),
        grid=(rows // tm,),
        in_specs=[pl.BlockSpec((tm, d), row), pl.BlockSpec(w.shape, full),
                  pl.BlockSpec(qn.shape, full), pl.BlockSpec(kn.shape, full),
                  pl.BlockSpec((tm, LANE), row), pl.BlockSpec((tm, LANE), row), pl.BlockSpec((tm, LANE), row)],
        out_specs=[pl.BlockSpec((tm, qw), row), pl.BlockSpec((tm, kw), row), pl.BlockSpec((tm, kw), row)],
        out_shape=[jax.ShapeDtypeStruct((rows, qw), BF16), jax.ShapeDtypeStruct((rows, kw), BF16),
                   jax.ShapeDtypeStruct((rows, kw), BF16)],
        compiler_params=_params(("parallel",)),
        name="gqa_proj",
    )(h, w, qn, kn, cos, sa, sb)


def _diff_proj_kernel(h_ref, w_ref, cos_ref, sa_ref, sb_ref, q_ref, k_ref, v_ref, *, q_scale):
    a = jnp.dot(h_ref[...], w_ref[...], preferred_element_type=F32)
    cos, sa, sb = cos_ref[...], sa_ref[...], sb_ref[...]
    quarter = DIFF_HEAD_DIM // 4
    n = 2 * DIFF_HEADS
    for hh in range(n):
        q_ref[:, hh * LANE:(hh + 1) * LANE] = (_rope(a[:, hh * LANE:(hh + 1) * LANE], cos, sa, sb, quarter)
                                               * q_scale).astype(BF16)
        k_ref[:, hh * LANE:(hh + 1) * LANE] = _rope(a[:, (n + hh) * LANE:(n + hh + 1) * LANE],
                                                    cos, sa, sb, quarter).astype(BF16)
    v_ref[...] = a[:, 2 * n * LANE:].astype(BF16)


def _diff_proj(h, w, tabs):
    rows, d = h.shape
    tm = _tile(rows, (640, 512, 256))
    cos, sa, sb = tabs
    row = lambda i: (i, 0)
    full = lambda i: (0, 0)
    ww = 2 * DIFF_HEADS * DIFF_HEAD_DIM
    vw = DIFF_HEADS * DIFF_V_DIM
    return pl.pallas_call(
        functools.partial(_diff_proj_kernel, q_scale=DIFF_HEAD_DIM ** -0.5 * LOG2E),
        grid=(rows // tm,),
        in_specs=[pl.BlockSpec((tm, d), row), pl.BlockSpec(w.shape, full),
                  pl.BlockSpec((tm, LANE), row), pl.BlockSpec((tm, LANE), row), pl.BlockSpec((tm, LANE), row)],
        out_specs=[pl.BlockSpec((tm, ww), row), pl.BlockSpec((tm, ww), row), pl.BlockSpec((tm, vw), row)],
        out_shape=[jax.ShapeDtypeStruct((rows, ww), BF16), jax.ShapeDtypeStruct((rows, ww), BF16),
                   jax.ShapeDtypeStruct((rows, vw), BF16)],
        compiler_params=_params(("parallel",)),
        name="diff_proj",
    )(h, w, cos, sa, sb)


def _gate_proj_kernel(h_ref, w_ref, o_ref):
    a = jnp.dot(h_ref[...], w_ref[...], preferred_element_type=F32)
    o_ref[...] = jax.nn.sigmoid(a).astype(BF16)


def _gate_proj(h, w):
    rows, d = h.shape
    n = w.shape[1]
    tm = _tile(rows, (640, 512, 256))
    tn = 1536
    return pl.pallas_call(
        _gate_proj_kernel,
        grid=(rows // tm, n // tn),
        in_specs=[pl.BlockSpec((tm, d), lambda i, j: (i, 0)), pl.BlockSpec((d, tn), lambda i, j: (0, j))],
        out_specs=pl.BlockSpec((tm, tn), lambda i, j: (i, j)),
        out_shape=jax.ShapeDtypeStruct((rows, n), BF16),
        compiler_params=_params(("parallel", "arbitrary")),
        name="gate_proj",
    )(h, w)


ITEMS_PER_TRIP = 2
MXU_DEPTH = 256


def _rows(start, size):
    return pl.ds(start if isinstance(start, int) else pl.multiple_of(start, size), size)


def _dot_nt(a, b):
    return lax.dot_general(a, b, (((1,), (1,)), ((), ())), preferred_element_type=F32)


def _flash(score, v_ref, n_sub, n_chunks, bkv, s_ref, mx_ref, m_ref, l_ref, acc_ref, block_cols=False):
    m_ref[...] = jnp.full(m_ref.shape, -jnp.inf, F32)
    l_ref[...] = jnp.zeros(l_ref.shape, F32)
    acc_ref[...] = jnp.zeros(acc_ref.shape, F32)

    def item(k):
        if isinstance(k, int):
            return k // n_chunks, k % n_chunks
        t = lax.div(k, n_chunks)
        return t, k - t * n_chunks

    def issue(k, slot):
        t, c = item(k)
        score(t, c, slot)

    def absorb(k, slot):
        t, c = item(k)
        m_prev = m_ref[t]
        m_new = jnp.maximum(m_prev, mx_ref[slot])
        alpha = jnp.exp2(m_prev - m_new)
        kt = MXU_DEPTH if (block_cols and bkv % MXU_DEPTH == 0) else bkv
        start = c * bkv
        lsum = jnp.zeros(m_prev.shape, F32)
        acc = alpha * acc_ref[t]
        for j in range(bkv // kt):
            p = jnp.exp2(s_ref[slot, :, j * kt:(j + 1) * kt] - m_new)
            lsum = lsum + jnp.sum(p, axis=1, keepdims=True)
            acc = acc + jnp.dot(p.astype(BF16), v_ref[_rows(start + j * kt, kt), :], preferred_element_type=F32)
        l_ref[t] = alpha * l_ref[t] + lsum
        acc_ref[t] = acc
        m_ref[t] = m_new

    n_items = n_sub * n_chunks
    issue(0, 0)

    def body(j, carry):
        k = ITEMS_PER_TRIP * j
        for i in range(ITEMS_PER_TRIP):
            issue(k + i + 1, (i + 1) % 2)
            absorb(k + i, i % 2)
        return carry

    n_trips = (n_items - 1) // ITEMS_PER_TRIP
    if n_trips:
        lax.fori_loop(0, n_trips, body, 0)
    for k in range(ITEMS_PER_TRIP * n_trips, n_items):
        if k + 1 < n_items:
            issue(k + 1, (k + 1) % 2)
        absorb(k, k % 2)


def _score_into(s_ref, mx_ref, slot, row0, q, kc):
    s = _dot_nt(q, kc)
    rows = q.shape[0]
    s_ref[slot, row0:row0 + rows] = s
    mx_ref[slot, row0:row0 + rows] = jnp.max(s, axis=1, keepdims=True)


def _mla_attn_kernel(q_ref, k_ref, v_ref, *rest, n_sub, n_chunks, bkv):
    o_ref, s_ref, mx_ref, m_ref, l_ref, acc_ref = rest[-6:]
    m_rows = s_ref.shape[1]

    def score(t, c, slot):
        _score_into(s_ref, mx_ref, slot, 0, q_ref[_rows(t * m_rows, m_rows), :], k_ref[_rows(c * bkv, bkv), :])

    _flash(score, v_ref, n_sub, n_chunks, bkv, s_ref, mx_ref, m_ref, l_ref, acc_ref)
    for t in range(n_sub):
        o_ref[t * m_rows:(t + 1) * m_rows, :] = (acc_ref[t] / l_ref[t]).astype(BF16)


def _gqa_attn_kernel(q_ref, k_ref, v_ref, *rest, n_sub, n_chunks, bkv):
    o_ref, qs_ref, s_ref, mx_ref, m_ref, l_ref, acc_ref = rest[-7:]
    g = GQA_HEADS // GQA_KV_HEADS
    bqs = s_ref.shape[1] // g
    for t in range(n_sub):
        for j in range(g):
            qs_ref[t, j * bqs:(j + 1) * bqs] = q_ref[t * bqs:(t + 1) * bqs, j * LANE:(j + 1) * LANE]

    def score(t, c, slot):
        _score_into(s_ref, mx_ref, slot, 0, qs_ref[t], k_ref[_rows(c * bkv, bkv), :])

    _flash(score, v_ref, n_sub, n_chunks, bkv, s_ref, mx_ref, m_ref, l_ref, acc_ref)
    for t in range(n_sub):
        o = acc_ref[t] / l_ref[t]
        for j in range(g):
            o_ref[t * bqs:(t + 1) * bqs, j * LANE:(j + 1) * LANE] = o[j * bqs:(j + 1) * bqs].astype(BF16)


def _diff_attn_kernel(q_ref, k_ref, v_ref, lam_ref, dn_ref, *rest, n_sub, n_chunks, bkv, lam_init):
    o_ref, s_ref, mx_ref, m_ref, l_ref, acc_ref = rest[-6:]
    d = DIFF_HEAD_DIM
    bqs = s_ref.shape[1] // 2
    lp = lam_ref[...]
    lam = (jnp.exp(jnp.sum(lp[0:1] * lp[1:2], axis=1, keepdims=True))
           - jnp.exp(jnp.sum(lp[2:3] * lp[3:4], axis=1, keepdims=True)) + lam_init)

    def score(t, c, slot):
        qr, kr = _rows(t * bqs, bqs), _rows(c * bkv, bkv)
        _score_into(s_ref, mx_ref, slot, 0, q_ref[qr, 0:d], k_ref[kr, 0:d])
        _score_into(s_ref, mx_ref, slot, bqs, q_ref[qr, d:2 * d], k_ref[kr, d:2 * d])

    _flash(score, v_ref, n_sub, n_chunks, bkv, s_ref, mx_ref, m_ref, l_ref, acc_ref, block_cols=True)
    for t in range(n_sub):
        o = acc_ref[t] / l_ref[t]
        o = o[:bqs] - lam * o[bqs:]
        o_ref[t * bqs:(t + 1) * bqs, :] = (_rms(o, dn_ref[...]) * (1.0 - lam_init)).astype(BF16)


def _attention(kind, q, k, v, s_lat, out_rows, extra=(), ctx=False, into=None):
    s_all = q.shape[0]
    n_ctx = s_all - s_lat
    assert s_lat % n_ctx == 0
    if kind == "mla":
        heads, qw, kw, vw, ow, stack = MLA_HEADS, MLA_QK_PAD, MLA_QK_PAD, MLA_V, MLA_V, 1
        bqs = n_ctx if ctx else _tile(s_lat, (1024, 512, 256))
        body = _mla_attn_kernel
    elif kind == "gqa":
        g = GQA_HEADS // GQA_KV_HEADS
        heads, qw, kw, vw, ow, stack = GQA_KV_HEADS, g * LANE, LANE, LANE, g * LANE, g
        bqs = n_ctx if ctx else 256
        body = _gqa_attn_kernel
    else:
        heads, qw, kw, vw, ow, stack = DIFF_HEADS, 2 * LANE, 2 * LANE, DIFF_V_DIM, DIFF_V_DIM, 2
        bqs = n_ctx if ctx else _tile(s_lat, (512, 256))
        body = functools.partial(_diff_attn_kernel, lam_init=extra[2])
        extra = extra[:2]
    m_rows = stack * bqs
    n_sub = 1 if ctx else _tile(s_lat // bqs, (4, 3, 2, 1))
    bq = n_sub * bqs
    if ctx:
        kv_rows, bkv, row_blk, nq = n_ctx, n_ctx, s_lat // n_ctx, 1
    else:
        kv_rows, bkv, row_blk, nq = s_all, _tile(s_all, (1280, 1024, 512, 256)), 0, s_lat // bq
    body = functools.partial(body, n_sub=n_sub, n_chunks=kv_rows // bkv, bkv=bkv)
    kv_blk = row_blk
    in_specs = [pl.BlockSpec((bq, qw), lambda h, i: (row_blk + i, h)),
                pl.BlockSpec((kv_rows, kw), lambda h, i: (kv_blk, h), pipeline_mode=pl.Buffered(1)),
                pl.BlockSpec((kv_rows, vw), lambda h, i: (kv_blk, h), pipeline_mode=pl.Buffered(1))]
    in_specs += [pl.BlockSpec(e.shape, lambda h, i: (0, 0)) for e in extra]
    args = [q, k, v, *extra]
    aliases = {}
    if into is not None:
        in_specs.append(pl.BlockSpec(memory_space=pl.ANY))
        aliases = {len(args): 0}
        args.append(into)
    scratch = []
    if kind == "gqa":
        scratch.append(pltpu.VMEM((n_sub, m_rows, LANE), BF16))
    scratch += [pltpu.VMEM((2, m_rows, bkv), F32),
                pltpu.VMEM((2, m_rows, 1), F32), pltpu.VMEM((n_sub, m_rows, 1), F32),
                pltpu.VMEM((n_sub, m_rows, 1), F32), pltpu.VMEM((n_sub, m_rows, vw), F32)]
    return pl.pallas_call(
        body,
        grid=(heads, nq),
        in_specs=in_specs,
        out_specs=pl.BlockSpec((bq, ow), lambda h, i: (row_blk + i, h)),
        out_shape=jax.ShapeDtypeStruct((out_rows, heads * ow), BF16),
        scratch_shapes=scratch,
        input_output_aliases=aliases,
        compiler_params=_params(("parallel", "arbitrary")),
        name=kind + ("_attn_ctx" if ctx else "_attn"),
    )(*args)


def _layer_norm(z, g, b):
    mu = jnp.mean(z, axis=-1, keepdims=True)
    zc = z - mu
    var = jnp.mean(zc * zc, axis=-1, keepdims=True)
    return zc * lax.rsqrt(var + LN_EPS) * g + b


def _merge_kernel(om_ref, og_ref, od_ref, g_ref, x_ref, mod_ref, wbm_ref, wbg_ref, wbd_ref, wo_ref,
                  lng_ref, lnb_ref, xo_ref, h2_ref, *, tm, s_lat):
    d = D_MODEL
    row0 = pl.program_id(0) * tm
    merged = g_ref[:, 0:d].astype(F32) * jnp.dot(om_ref[...], wbm_ref[...], preferred_element_type=F32)
    merged += g_ref[:, d:2 * d].astype(F32) * jnp.dot(og_ref[...], wbg_ref[...], preferred_element_type=F32)
    merged += g_ref[:, 2 * d:3 * d].astype(F32) * jnp.dot(od_ref[...], wbd_ref[...], preferred_element_type=F32)
    y = jnp.dot(merged.astype(BF16), wo_ref[...], preferred_element_type=F32)
    z = DEEPNORM_ALPHA * x_ref[...] + _row_mod(mod_ref, 2, row0, tm, s_lat) * y
    xn = _layer_norm(z, lng_ref[...], lnb_ref[...])
    xo_ref[...] = xn
    h2 = xn * (1.0 + _row_mod(mod_ref, 4, row0, tm, s_lat)) + _row_mod(mod_ref, 3, row0, tm, s_lat)
    h2_ref[...] = _pack_pairs(h2)


def _merge(om, og, od, gates, xs, mod, wbm, wbg, wbd, wo, lng, lnb, rows, s_lat):
    d = D_MODEL
    tm = 256
    row = lambda i: (i, 0)
    full = lambda i: (0, 0)
    resident = lambda a: pl.BlockSpec(a.shape, full, pipeline_mode=pl.Buffered(1))
    return pl.pallas_call(
        functools.partial(_merge_kernel, tm=tm, s_lat=s_lat),
        grid=(rows // tm,),
        in_specs=[pl.BlockSpec((tm, om.shape[1]), row), pl.BlockSpec((tm, og.shape[1]), row),
                  pl.BlockSpec((tm, od.shape[1]), row), pl.BlockSpec((tm, gates.shape[1]), row),
                  pl.BlockSpec((tm, d), row), pl.BlockSpec(mod.shape, full),
                  resident(wbm), resident(wbg), resident(wbd), resident(wo),
                  pl.BlockSpec(lng.shape, full), pl.BlockSpec(lnb.shape, full)],
        out_specs=[pl.BlockSpec((tm, d), row), pl.BlockSpec((tm, d // 2), row)],
        out_shape=[jax.ShapeDtypeStruct((rows, d), F32), jax.ShapeDtypeStruct((rows, d // 2), jnp.uint32)],
        compiler_params=_params(("parallel",)),
        name="merge",
    )(om, og, od, gates, xs, mod, wbm, wbg, wbd, wo, lng, lnb)


def _router_kernel(x_ref, mod_ref, wr_ref, br_ref, ri_ref, rw_ref, cnt_ref, run_ref, *, tm, s_lat):
    @pl.when(pl.program_id(0) == 0)
    def _():
        run_ref[...] = jnp.zeros(run_ref.shape, F32)

    row0 = pl.program_id(0) * tm
    h2 = x_ref[...] * (1.0 + _row_mod(mod_ref, 4, row0, tm, s_lat)) + _row_mod(mod_ref, 3, row0, tm, s_lat)
    logits = lax.dot_general(wr_ref[...], h2, (((1,), (1,)), ((), ())),
                             precision=HIGHEST, preferred_element_type=F32)
    scores = jax.nn.sigmoid(logits)
    biased = scores + br_ref[...]
    sc = [scores[e:e + 1, :] for e in range(N_EXPERTS)]
    bi = [biased[e:e + 1, :] for e in range(N_EXPERTS)]
    gs = []
    for g in range(N_GROUPS):
        a, b, c, dd = bi[4 * g:4 * g + 4]
        hi1, lo1, hi2, lo2 = jnp.maximum(a, b), jnp.minimum(a, b), jnp.maximum(c, dd), jnp.minimum(c, dd)
        gs.append(jnp.maximum(hi1, hi2) + jnp.maximum(jnp.minimum(hi1, hi2), jnp.maximum(lo1, lo2)))
    best = jnp.maximum(jnp.maximum(gs[0], gs[1]), jnp.maximum(gs[2], gs[3]))
    gsel = jnp.where(gs[0] == best, 0, jnp.where(gs[1] == best, 1, jnp.where(gs[2] == best, 2, 3)))
    pick = lambda vals, j: jnp.where(gsel == 0, vals[j], jnp.where(gsel == 1, vals[4 + j],
                                     jnp.where(gsel == 2, vals[8 + j], vals[12 + j])))
    xb = [pick(bi, j) for j in range(EXPERTS_PER_GROUP)]
    xs = [pick(sc, j) for j in range(EXPERTS_PER_GROUP)]
    sel, w = [], []
    for i in range(EXPERTS_PER_GROUP):
        rank = jnp.zeros_like(gsel)
        for j in range(EXPERTS_PER_GROUP):
            if j == i:
                continue
            beats = (xb[j] >= xb[i]) if j < i else (xb[j] > xb[i])
            rank = rank + beats.astype(jnp.int32)
        sel.append(rank < 2)
        w.append(jnp.where(rank < 2, xs[i], 0.0))
    inv = ROUTED_SCALE / (w[0] + w[1] + w[2] + w[3])
    ia = jnp.where(sel[0], 0, jnp.where(sel[1], 1, jnp.where(sel[2], 2, 3)))
    ib = jnp.where(sel[3], 3, jnp.where(sel[2], 2, jnp.where(sel[1], 1, 0)))
    local = lambda idx: jnp.where(idx == 0, w[0], jnp.where(idx == 1, w[1], jnp.where(idx == 2, w[2], w[3])))
    ea = gsel * EXPERTS_PER_GROUP + ia
    eb = gsel * EXPERTS_PER_GROUP + ib
    eidx = lax.broadcasted_iota(jnp.int32, (N_EXPERTS, tm), 0)
    is_a, is_b = eidx == ea, eidx == eb
    onehot = jnp.where(is_a | is_b, 1.0, 0.0)
    earlier = jnp.where(lax.broadcasted_iota(jnp.int32, (tm, tm), 0) < lax.broadcasted_iota(jnp.int32, (tm, tm), 1),
                        1.0, 0.0).astype(BF16)
    pos = run_ref[...] + jnp.dot(onehot.astype(BF16), earlier, preferred_element_type=F32)
    ri_ref[0:1, :] = ea
    ri_ref[1:2, :] = eb
    ri_ref[2:3, :] = jnp.sum(jnp.where(is_a, pos, 0.0), axis=0, keepdims=True).astype(jnp.int32)
    ri_ref[3:4, :] = jnp.sum(jnp.where(is_b, pos, 0.0), axis=0, keepdims=True).astype(jnp.int32)
    rw_ref[0:1, :] = local(ia) * inv
    rw_ref[1:2, :] = local(ib) * inv
    run = run_ref[...] + jnp.sum(onehot, axis=1, keepdims=True)
    run_ref[...] = run
    cnt_ref[...] = jnp.broadcast_to(run, cnt_ref.shape)


def _router(xs, mod, wr_t, br, rows, s_lat):
    d = D_MODEL
    tm = _tile(rows, (640, 512, 256))
    return pl.pallas_call(
        functools.partial(_router_kernel, tm=tm, s_lat=s_lat),
        grid=(rows // tm,),
        in_specs=[pl.BlockSpec((tm, d), lambda i: (i, 0)), pl.BlockSpec(mod.shape, lambda i: (0, 0)),
                  pl.BlockSpec(wr_t.shape, lambda i: (0, 0)), pl.BlockSpec(br.shape, lambda i: (0, 0))],
        out_specs=[pl.BlockSpec((4, tm), lambda i: (0, i)), pl.BlockSpec((2, tm), lambda i: (0, i)),
                   pl.BlockSpec((N_EXPERTS, LANE), lambda i: (0, 0))],
        out_shape=[jax.ShapeDtypeStruct((4, rows), jnp.int32), jax.ShapeDtypeStruct((2, rows), F32),
                   jax.ShapeDtypeStruct((N_EXPERTS, LANE), F32)],
        scratch_shapes=[pltpu.VMEM((N_EXPERTS, 1), F32)],
        compiler_params=_params(("arbitrary",)),
        name="router",
    )(xs, mod, wr_t, br)


MOE_TILE = 512
HALF = D_MODEL // 2
DMA_UNROLL = 8


def _pack_pairs(h):
    hi = pltpu.bitcast(h[:, :HALF].astype(BF16).astype(F32), jnp.uint32)
    lo = pltpu.bitcast(h[:, HALF:].astype(BF16).astype(F32), jnp.uint32)
    return (hi & jnp.uint32(0xFFFF0000)) | (lo >> jnp.uint32(16))


def _unpack_pairs(w):
    hi = pltpu.bitcast(w & jnp.uint32(0xFFFF0000), F32)
    lo = pltpu.bitcast(w << jnp.uint32(16), F32)
    return jnp.concatenate([hi, lo], axis=1).astype(BF16)


def _swiglu(x, wg, wu, wd):
    a = jnp.dot(x, wg, preferred_element_type=F32)
    b = jnp.dot(x, wu, preferred_element_type=F32)
    return jnp.dot((a * jax.nn.sigmoid(a) * b).astype(BF16), wd, preferred_element_type=F32)


def _moe_sort_kernel(dest_ref, h_ref, init_ref, o_ref, sem, *, tg):
    del init_ref
    base = pl.program_id(0) * tg

    def copy(r, j):
        return pltpu.make_async_copy(h_ref.at[pl.ds(r, 1)], o_ref.at[pl.ds(dest_ref[j, base + r], 1)], sem)

    def start(r, carry):
        for j in range(2):
            copy(r, j).start()
        return carry

    def wait(r, carry):
        for j in range(2):
            copy(r, j).wait()
        return carry

    lax.fori_loop(0, tg, start, 0, unroll=DMA_UNROLL)
    lax.fori_loop(0, tg, wait, 0, unroll=DMA_UNROLL)


def _moe_sort(dest, h2p, n_sorted):
    rows, half = h2p.shape
    tg = _tile(rows, (1280, 1024, 512, 256))
    return pl.pallas_call(
        functools.partial(_moe_sort_kernel, tg=tg),
        grid_spec=pltpu.PrefetchScalarGridSpec(
            num_scalar_prefetch=1, grid=(rows // tg,),
            in_specs=[pl.BlockSpec((tg, half), lambda i, dref: (i, 0)), pl.BlockSpec(memory_space=pl.ANY)],
            out_specs=pl.BlockSpec(memory_space=pl.ANY),
            scratch_shapes=[pltpu.SemaphoreType.DMA]),
        out_shape=jax.ShapeDtypeStruct((n_sorted, half), jnp.uint32),
        input_output_aliases={2: 0},
        compiler_params=_params(("arbitrary",)),
        name="moe_sort",
    )(dest, h2p, jnp.zeros((n_sorted, half), jnp.uint32))


def _moe_expert_kernel(te_ref, na_ref, x_ref, wg_ref, wu_ref, wd_ref, y_ref):
    del te_ref
    live = pl.program_id(0) < na_ref[0]

    @pl.when(live)
    def _():
        y_ref[...] = _swiglu(_unpack_pairs(x_ref[...]), wg_ref[0].astype(BF16), wu_ref[0].astype(BF16),
                             wd_ref[0].astype(BF16))

    @pl.when(jnp.logical_not(live))
    def _():
        y_ref[...] = jnp.zeros(y_ref.shape, F32)


def _moe_experts(tile_expert, n_active, xsorted, wg, wu, wd):
    n_sorted, half = xsorted.shape
    _, d, f = wg.shape
    tme = MOE_TILE
    wsel = lambda i, te, na: (te[i], 0, 0)
    return pl.pallas_call(
        _moe_expert_kernel,
        grid_spec=pltpu.PrefetchScalarGridSpec(
            num_scalar_prefetch=2, grid=(n_sorted // tme,),
            in_specs=[pl.BlockSpec((tme, half), lambda i, te, na: (i, 0)),
                      pl.BlockSpec((1, d, f), wsel), pl.BlockSpec((1, d, f), wsel), pl.BlockSpec((1, f, d), wsel)],
            out_specs=pl.BlockSpec((tme, d), lambda i, te, na: (i, 0))),
        out_shape=jax.ShapeDtypeStruct((n_sorted, d), F32),
        compiler_params=_params(("arbitrary",)),
        name="moe_experts",
    )(tile_expert, n_active, xsorted, wg, wu, wd)


def _moe_final_kernel(dest_ref, hp_ref, rw_ref, y_ref, sg_ref, su_ref, sd_ref, x_ref, mod_ref, lng_ref, lnb_ref,
                      o_ref, ya_ref, yb_ref, sem, *, tm, s_lat):
    base = pl.program_id(0) * tm
    bufs = (ya_ref, yb_ref)

    def copy(r, j):
        return pltpu.make_async_copy(y_ref.at[pl.ds(dest_ref[j, base + r], 1)], bufs[j].at[pl.ds(r, 1)], sem.at[j])

    def start(r, carry):
        for j in range(2):
            copy(r, j).start()
        return carry

    def wait(r, carry):
        for j in range(2):
            copy(r, j).wait()
        return carry

    lax.fori_loop(0, tm, start, 0, unroll=DMA_UNROLL)
    shared = _swiglu(_unpack_pairs(hp_ref[...]), sg_ref[...], su_ref[...], sd_ref[...])
    lax.fori_loop(0, tm, wait, 0, unroll=DMA_UNROLL)
    y = shared + rw_ref[:, 0:1] * ya_ref[...] + rw_ref[:, 1:2] * yb_ref[...]
    z = DEEPNORM_ALPHA * x_ref[...] + _row_mod(mod_ref, 5, base, tm, s_lat) * y
    o_ref[...] = _layer_norm(z, lng_ref[...], lnb_ref[...])


def _moe_final(dest, h2p, rw, ysorted, sg, su, sd, xs, mod, lng, lnb, rows, s_lat):
    d = D_MODEL
    tm = 256
    row = lambda i, dref: (i, 0)
    full = lambda i, dref: (0, 0)
    return pl.pallas_call(
        functools.partial(_moe_final_kernel, tm=tm, s_lat=s_lat),
        grid_spec=pltpu.PrefetchScalarGridSpec(
            num_scalar_prefetch=1, grid=(rows // tm,),
            in_specs=[pl.BlockSpec((tm, HALF), row), pl.BlockSpec((tm, 2), row), pl.BlockSpec(memory_space=pl.ANY),
                      pl.BlockSpec(sg.shape, full), pl.BlockSpec(su.shape, full), pl.BlockSpec(sd.shape, full),
                      pl.BlockSpec((tm, d), row), pl.BlockSpec(mod.shape, full),
                      pl.BlockSpec(lng.shape, full), pl.BlockSpec(lnb.shape, full)],
            out_specs=pl.BlockSpec((tm, d), row),
            scratch_shapes=[pltpu.VMEM((tm, d), F32), pltpu.VMEM((tm, d), F32), pltpu.SemaphoreType.DMA((2,))]),
        out_shape=jax.ShapeDtypeStruct((rows, d), F32),
        compiler_params=_params(("arbitrary",)),
        name="moe_final",
    )(dest, h2p, rw, ysorted, sg, su, sd, xs, mod, lng, lnb)


def _moe_plan(ri, cnt, rows):
    tme = MOE_TILE
    n_sorted = 2 * rows + N_EXPERTS * tme
    counts = cnt[:, 0].astype(jnp.int32)
    padded = (counts + tme - 1) // tme * tme
    ends = jnp.cumsum(padded)
    offs = ends - padded
    dest = jnp.stack([offs[ri[0]] + ri[2], offs[ri[1]] + ri[3]])
    tile_start = jnp.arange(n_sorted // tme, dtype=jnp.int32) * tme
    tile_expert = jnp.minimum(jnp.sum((ends[None, :] <= tile_start[:, None]).astype(jnp.int32), axis=1),
                              N_EXPERTS - 1)
    n_active = (ends[-1:] // tme).astype(jnp.int32)
    return dest, tile_expert, n_active, n_sorted


def _moe(xs, h2p, mod, wr_t, br, wg, wu, wd, sg, su, sd, lng, lnb, rows, s_lat):
    ri, rw, cnt = _router(xs, mod, wr_t, br, rows, s_lat)
    dest, tile_expert, n_active, n_sorted = _moe_plan(ri, cnt, rows)
    xsorted = _moe_sort(dest, h2p, n_sorted)
    ysorted = _moe_experts(tile_expert, n_active, xsorted, wg, wu, wd)
    return _moe_final(dest, h2p, rw.T, ysorted, sg, su, sd, xs, mod, lng, lnb, rows, s_lat)


def _rope_tables(s_lat, n_ctx, dim):
    quarter = dim // 4
    rows = s_lat // GRID_W
    inv_freq = ROPE_THETA ** (-jnp.arange(quarter, dtype=F32) / quarter)
    ang_r = jnp.arange(rows, dtype=F32)[:, None] * inv_freq
    ang_c = jnp.arange(GRID_W, dtype=F32)[:, None] * inv_freq
    per_row = lambda t: jnp.repeat(t, GRID_W, axis=0)
    per_col = lambda t: jnp.tile(t, (rows, 1))
    cos_r, sin_r = per_row(jnp.cos(ang_r)), per_row(jnp.sin(ang_r))
    cos_c, sin_c = per_col(jnp.cos(ang_c)), per_col(jnp.sin(ang_c))
    zero = jnp.zeros_like(cos_r)
    pad = jnp.zeros((s_lat, LANE - dim), F32)
    cos = jnp.concatenate([cos_r, cos_r, cos_c, cos_c, pad + 1.0], axis=1)
    sa = jnp.concatenate([-sin_r, zero, -sin_c, zero, pad], axis=1)
    sb = jnp.concatenate([zero, sin_r, zero, sin_c, pad], axis=1)
    ctx1 = jnp.ones((n_ctx, LANE), F32)
    ctx0 = jnp.zeros((n_ctx, LANE), F32)
    return (jnp.concatenate([cos, ctx1], axis=0), jnp.concatenate([sa, ctx0], axis=0),
            jnp.concatenate([sb, ctx0], axis=0))


def kernel(x, c, ctx, c_ctx, w_ada, b_ada, w_in, mla_q_norm, mla_w_uq, mla_kv_norm, mla_w_ukv, gqa_q_norm,
           gqa_k_norm, diff_lambda, diff_norm, w_br_mla, w_br_gqa, w_br_diff, w_o, ln1_g, ln1_b, w_router,
           b_router, moe_w_gate, moe_w_up, moe_w_down, shared_w_gate, shared_w_up, shared_w_down, ln2_g, ln2_b):
    B, s_lat, d = x.shape
    n_ctx = ctx.shape[1]
    assert B == 1 and d == D_MODEL
    s_all = s_lat + n_ctx

    xs = jnp.concatenate([x[0], ctx[0]], axis=0)
    cc = jnp.zeros((8, d), F32).at[0].set(c[0]).at[1].set(c_ctx)
    mod_all = _adaln(cc, w_ada, b_ada)

    tabs64 = _rope_tables(s_lat, n_ctx, MLA_ROPE)
    tabs128 = _rope_tables(s_lat, n_ctx, GQA_HEAD_DIM)
    wr_t = w_router.T
    br = b_router.reshape(N_EXPERTS, 1)
    row2 = lambda v: v.reshape(1, -1)

    o0 = MLA_Q_LORA + MLA_KV_LORA + MLA_ROPE
    o1 = o0 + GQA_HEADS * GQA_HEAD_DIM + 2 * GQA_KV_HEADS * GQA_HEAD_DIM
    o2 = o1 + 2 * (2 * DIFF_HEADS * DIFF_HEAD_DIM) + DIFF_HEADS * DIFF_V_DIM

    for l in range(DEPTH):
        last = l == DEPTH - 1
        mod = mod_all[l]
        w = w_in[l]
        wa = jnp.pad(w[:, :o0], ((0, 0), (0, LANE - MLA_ROPE))).astype(BF16)
        wb = w[:, o0:o1].astype(BF16)
        wc = w[:, o1:o2].astype(BF16)
        wd_gate = w[:, o2:].astype(BF16)
        wuq = jnp.pad(mla_w_uq[l].reshape(MLA_Q_LORA, MLA_HEADS, MLA_NOPE + MLA_ROPE),
                      ((0, 0), (0, 0), (0, MLA_QK_PAD - MLA_NOPE - MLA_ROPE))
                      ).reshape(MLA_Q_LORA, MLA_HEADS * MLA_QK_PAD).astype(BF16)
        ukv = mla_w_ukv[l].reshape(MLA_KV_LORA, MLA_HEADS, MLA_NOPE + MLA_V)
        wukv = jnp.concatenate([ukv[:, :, :MLA_NOPE].reshape(MLA_KV_LORA, -1),
                                ukv[:, :, MLA_NOPE:].reshape(MLA_KV_LORA, -1)], axis=1).astype(BF16)

        h, q_mla, k_mla, v_mla = _mla_proj(xs, mod, wa, row2(mla_q_norm[l]), row2(mla_kv_norm[l]),
                                           wuq, wukv, tabs64, s_lat)
        q_gqa, k_gqa, v_gqa = _gqa_proj(h, wb, row2(gqa_q_norm[l]), row2(gqa_k_norm[l]), tabs128)
        q_dif, k_dif, v_dif = _diff_proj(h, wc, tabs128)
        gates = _gate_proj(h, wd_gate)

        lam_init = 0.8 - 0.6 * math.exp(-0.3 * l)
        rows = s_lat if last else s_all
        dif_extra = (diff_lambda[l], row2(diff_norm[l]), lam_init)
        fresh = lambda width: None if last else jnp.zeros((rows, width), BF16)
        o_mla = _attention("mla", q_mla, k_mla, v_mla, s_lat, rows, into=fresh(MLA_HEADS * MLA_V))
        o_gqa = _attention("gqa", q_gqa, k_gqa, v_gqa, s_lat, rows, into=fresh(GQA_HEADS * GQA_HEAD_DIM))
        o_dif = _attention("diff", q_dif, k_dif, v_dif, s_lat, rows, extra=dif_extra,
                           into=fresh(DIFF_HEADS * DIFF_V_DIM))
        if not last:
            o_mla = _attention("mla", q_mla, k_mla, v_mla, s_lat, rows, ctx=True, into=o_mla)
            o_gqa = _attention("gqa", q_gqa, k_gqa, v_gqa, s_lat, rows, ctx=True, into=o_gqa)
            o_dif = _attention("diff", q_dif, k_dif, v_dif, s_lat, rows, extra=dif_extra, ctx=True, into=o_dif)

        xs, h2p = _merge(o_mla, o_gqa, o_dif, gates, xs, mod, w_br_mla[l].astype(BF16), w_br_gqa[l].astype(BF16),
                         w_br_diff[l].astype(BF16), w_o[l].astype(BF16), row2(ln1_g[l]), row2(ln1_b[l]), rows, s_lat)

        xs = _moe(xs, h2p, mod, wr_t, br, moe_w_gate[l], moe_w_up[l], moe_w_down[l],
                  shared_w_gate[l].astype(BF16), shared_w_up[l].astype(BF16), shared_w_down[l].astype(BF16),
                  row2(ln2_g[l]), row2(ln2_b[l]), rows, s_lat)

    return xs[None]
```

```python
import functools
import math

import jax
import jax.numpy as jnp
from jax import lax
from jax.experimental import pallas as pl
from jax.experimental.pallas import tpu as pltpu

F32 = jnp.float32
BF16 = jnp.bfloat16
HIGHEST = lax.Precision.HIGHEST

D_MODEL = 2048
DEPTH = 2
GRID_W = 64
ROPE_THETA = 10000.0
LN_EPS = 1e-5
RMS_EPS = 1e-6
MLA_HEADS = 8
MLA_Q_LORA = 512
MLA_KV_LORA = 512
MLA_NOPE = 128
MLA_ROPE = 64
MLA_V = 128
GQA_HEADS = 8
GQA_KV_HEADS = 2
GQA_HEAD_DIM = 128
DIFF_HEADS = 4
DIFF_HEAD_DIM = 128
DIFF_V_DIM = 2 * DIFF_HEAD_DIM
N_EXPERTS = 16
N_GROUPS = 4
EXPERTS_PER_GROUP = N_EXPERTS // N_GROUPS
MOE_D_FF = 512
ROUTED_SCALE = 1.0
DEEPNORM_ALPHA = (2 * DEPTH) ** 0.25

LANE = 128
MLA_QK_PAD = 2 * LANE
LOG2E = math.log2(math.e)
VMEM_LIMIT = 56 * 1024 * 1024


def _tile(rows, prefs):
    for t in prefs:
        if rows % t == 0:
            return t
    raise ValueError(f"no tile in {prefs} divides {rows}")


def _params(sem, vmem=VMEM_LIMIT, flags=None):
    return pltpu.CompilerParams(dimension_semantics=sem, vmem_limit_bytes=vmem, flags=flags)


def _row_mod(mod_ref, k, row0, tm, s_lat):
    d = D_MODEL
    lat = mod_ref[0:1, k * d:(k + 1) * d]
    ctx = mod_ref[1:2, k * d:(k + 1) * d]
    rows = row0 + lax.broadcasted_iota(jnp.int32, (tm, 1), 0)
    return jnp.where(rows >= s_lat, ctx, lat)


def _rms(x, g):
    return x * lax.rsqrt(jnp.mean(x * x, axis=-1, keepdims=True) + RMS_EPS) * g


def _rope(t, cos, sa, sb, quarter):
    return t * cos + pltpu.roll(t, LANE - quarter, 1) * sa + pltpu.roll(t, quarter, 1) * sb


def _adaln_kernel(c_ref, w_ref, b_ref, o_ref):
    a = c_ref[...]
    a = a * jax.nn.sigmoid(a)
    o_ref[0] = jnp.dot(a, w_ref[0], precision=HIGHEST, preferred_element_type=F32) + b_ref[0]


def _adaln(cc, w_ada, b_ada):
    L, d, n = w_ada.shape
    tn = 1024
    return pl.pallas_call(
        _adaln_kernel,
        grid=(L, n // tn),
        in_specs=[pl.BlockSpec((8, d), lambda l, j: (0, 0)),
                  pl.BlockSpec((1, d, tn), lambda l, j: (l, 0, j)),
                  pl.BlockSpec((1, 1, tn), lambda l, j: (l, 0, j))],
        out_specs=pl.BlockSpec((1, 8, tn), lambda l, j: (l, 0, j)),
        out_shape=jax.ShapeDtypeStruct((L, 8, n), F32),
        compiler_params=_params(("arbitrary", "arbitrary")),
        name="adaln",
    )(cc, w_ada, b_ada.reshape(L, 1, n))


def _mla_proj_kernel(x_ref, mod_ref, wa_ref, qn_ref, kvn_ref, wuq_ref, wukv_ref, cos_ref, sa_ref, sb_ref,
                     h_ref, q_ref, k_ref, v_ref, *, tm, s_lat, q_scale):
    row0 = pl.program_id(0) * tm
    sh = _row_mod(mod_ref, 0, row0, tm, s_lat)
    sc = _row_mod(mod_ref, 1, row0, tm, s_lat)
    h = (x_ref[...] * (1.0 + sc) + sh).astype(BF16)
    h_ref[...] = h
    a = jnp.dot(h, wa_ref[...], preferred_element_type=F32)
    cqn = _rms(a[:, :MLA_Q_LORA], qn_ref[...]).astype(BF16)
    ckvn = _rms(a[:, MLA_Q_LORA:MLA_Q_LORA + MLA_KV_LORA], kvn_ref[...]).astype(BF16)
    kr = a[:, MLA_Q_LORA + MLA_KV_LORA:]
    q = jnp.dot(cqn, wuq_ref[...], preferred_element_type=F32)
    kv = jnp.dot(ckvn, wukv_ref[...], preferred_element_type=F32)
    cos, sa, sb = cos_ref[...], sa_ref[...], sb_ref[...]
    quarter = MLA_ROPE // 4
    kr2 = _rope(kr, cos, sa, sb, quarter).astype(BF16)
    for hh in range(MLA_HEADS):
        c0 = hh * MLA_QK_PAD
        q_ref[:, c0:c0 + LANE] = (q[:, c0:c0 + LANE] * q_scale).astype(BF16)
        q_ref[:, c0 + LANE:c0 + 2 * LANE] = (_rope(q[:, c0 + LANE:c0 + 2 * LANE], cos, sa, sb, quarter)
                                             * q_scale).astype(BF16)
        k_ref[:, c0:c0 + LANE] = kv[:, hh * LANE:(hh + 1) * LANE].astype(BF16)
        k_ref[:, c0 + LANE:c0 + 2 * LANE] = kr2
    v_ref[...] = kv[:, MLA_HEADS * MLA_NOPE:].astype(BF16)


def _mla_proj(xs, mod, wa, qn, kvn, wuq, wukv, tabs, s_lat):
    rows, d = xs.shape
    tm = _tile(rows, (640, 512, 256))
    cos, sa, sb = tabs
    row = lambda i: (i, 0)
    full = lambda i: (0, 0)
    qk_w = MLA_HEADS * MLA_QK_PAD
    v_w = MLA_HEADS * MLA_V
    q_scale = (MLA_NOPE + MLA_ROPE) ** -0.5 * LOG2E
    return pl.pallas_call(
        functools.partial(_mla_proj_kernel, tm=tm, s_lat=s_lat, q_scale=q_scale),
        grid=(rows // tm,),
        in_specs=[pl.BlockSpec((tm, d), row), pl.BlockSpec(mod.shape, full),
                  pl.BlockSpec(wa.shape, full), pl.BlockSpec(qn.shape, full), pl.BlockSpec(kvn.shape, full),
                  pl.BlockSpec(wuq.shape, full), pl.BlockSpec(wukv.shape, full),
                  pl.BlockSpec((tm, LANE), row), pl.BlockSpec((tm, LANE), row), pl.BlockSpec((tm, LANE), row)],
        out_specs=[pl.BlockSpec((tm, d), row), pl.BlockSpec((tm, qk_w), row),
                   pl.BlockSpec((tm, qk_w), row), pl.BlockSpec((tm, v_w), row)],
        out_shape=[jax.ShapeDtypeStruct((rows, d), BF16), jax.ShapeDtypeStruct((rows, qk_w), BF16),
                   jax.ShapeDtypeStruct((rows, qk_w), BF16), jax.ShapeDtypeStruct((rows, v_w), BF16)],
        compiler_params=_params(("parallel",)),
        name="mla_proj",
    )(xs, mod, wa, qn, kvn, wuq, wukv, cos, sa, sb)


def _gqa_proj_kernel(h_ref, w_ref, qn_ref, kn_ref, cos_ref, sa_ref, sb_ref, q_ref, k_ref, v_ref, *, q_scale):
    a = jnp.dot(h_ref[...], w_ref[...], preferred_element_type=F32)
    cos, sa, sb = cos_ref[...], sa_ref[...], sb_ref[...]
    quarter = GQA_HEAD_DIM // 4
    for hh in range(GQA_HEADS):
        x = _rms(a[:, hh * LANE:(hh + 1) * LANE], qn_ref[...])
        q_ref[:, hh * LANE:(hh + 1) * LANE] = (_rope(x, cos, sa, sb, quarter) * q_scale).astype(BF16)
    k0 = GQA_HEADS * GQA_HEAD_DIM
    for hh in range(GQA_KV_HEADS):
        x = _rms(a[:, k0 + hh * LANE:k0 + (hh + 1) * LANE], kn_ref[...])
        k_ref[:, hh * LANE:(hh + 1) * LANE] = _rope(x, cos, sa, sb, quarter).astype(BF16)
    v_ref[...] = a[:, k0 + GQA_KV_HEADS * GQA_HEAD_DIM:].astype(BF16)


def _gqa_proj(h, w, qn, kn, tabs):
    rows, d = h.shape
    tm = _tile(rows, (640, 512, 256))
    cos, sa, sb = tabs
    row = lambda i: (i, 0)
    full = lambda i: (0, 0)
    qw = GQA_HEADS * GQA_HEAD_DIM
    kw = GQA_KV_HEADS * GQA_HEAD_DIM
    return pl.pallas_call(
        functools.partial(_gqa_proj_kernel, q_scale=GQA_HEAD_DIM ** -0.5 * LOG2E),
        grid=(rows // tm,),
        in_specs=[pl.BlockSpec((tm, d), row), pl.BlockSpec(w.shape, full),
                  pl.BlockSpec(qn.shape, full), pl.BlockSpec(kn.shape, full),
                  pl.BlockSpec((tm, LANE), row), pl.BlockSpec((tm, LANE), row), pl.BlockSpec((tm, LANE), row)],
        out_specs=[pl.BlockSpec((tm, qw), row), pl.BlockSpec((tm, kw), row), pl.BlockSpec((tm, kw), row)],
        out_shape=[jax.ShapeDtypeStruct((rows, qw), BF16), jax.ShapeDtypeStruct((rows, kw), BF16),
                   jax.ShapeDtypeStruct((rows, kw), BF16)],
        compiler_params=_params(("parallel",)),
        name="gqa_proj",
    )(h, w, qn, kn, cos, sa, sb)


def _diff_proj_kernel(h_ref, w_ref, cos_ref, sa_ref, sb_ref, q_ref, k_ref, v_ref, *, q_scale):
    a = jnp.dot(h_ref[...], w_ref[...], preferred_element_type=F32)
    cos, sa, sb = cos_ref[...], sa_ref[...], sb_ref[...]
    quarter = DIFF_HEAD_DIM // 4
    n = 2 * DIFF_HEADS
    for hh in range(n):
        q_ref[:, hh * LANE:(hh + 1) * LANE] = (_rope(a[:, hh * LANE:(hh + 1) * LANE], cos, sa, sb, quarter)
                                               * q_scale).astype(BF16)
        k_ref[:, hh * LANE:(hh + 1) * LANE] = _rope(a[:, (n + hh) * LANE:(n + hh + 1) * LANE],
                                                    cos, sa, sb, quarter).astype(BF16)
    v_ref[...] = a[:, 2 * n * LANE:].astype(BF16)


def _diff_proj(h, w, tabs):
    rows, d = h.shape
    tm = _tile(rows, (640, 512, 256))
    cos, sa, sb = tabs
    row = lambda i: (i, 0)
    full = lambda i: (0, 0)
    ww = 2 * DIFF_HEADS * DIFF_HEAD_DIM
    vw = DIFF_HEADS * DIFF_V_DIM
    return pl.pallas_call(
        functools.partial(_diff_proj_kernel, q_scale=DIFF_HEAD_DIM ** -0.5 * LOG2E),
        grid=(rows // tm,),
        in_specs=[pl.BlockSpec((tm, d), row), pl.BlockSpec(w.shape, full),
                  pl.BlockSpec((tm, LANE), row), pl.BlockSpec((tm, LANE), row), pl.BlockSpec((tm, LANE), row)],
        out_specs=[pl.BlockSpec((tm, ww), row), pl.BlockSpec((tm, ww), row), pl.BlockSpec((tm, vw), row)],
        out_shape=[jax.ShapeDtypeStruct((rows, ww), BF16), jax.ShapeDtypeStruct((rows, ww), BF16),
                   jax.ShapeDtypeStruct((rows, vw), BF16)],
        compiler_params=_params(("parallel",)),
        name="diff_proj",
    )(h, w, cos, sa, sb)


def _gate_proj_kernel(h_ref, w_ref, o_ref):
    a = jnp.dot(h_ref[...], w_ref[...], preferred_element_type=F32)
    o_ref[...] = jax.nn.sigmoid(a).astype(BF16)


def _gate_proj(h, w):
    rows, d = h.shape
    n = w.shape[1]
    tm = _tile(rows, (640, 512, 256))
    tn = 1536
    return pl.pallas_call(
        _gate_proj_kernel,
        grid=(rows // tm, n // tn),
        in_specs=[pl.BlockSpec((tm, d), lambda i, j: (i, 0)), pl.BlockSpec((d, tn), lambda i, j: (0, j))],
        out_specs=pl.BlockSpec((tm, tn), lambda i, j: (i, j)),
        out_shape=jax.ShapeDtypeStruct((rows, n), BF16),
        compiler_params=_params(("parallel", "arbitrary")),
        name="gate_proj",
    )(h, w)


ITEMS_PER_TRIP = 2
MXU_DEPTH = 256


def _rows(start, size):
    return pl.ds(start if isinstance(start, int) else pl.multiple_of(start, size), size)


def _dot_nt(a, b):
    return lax.dot_general(a, b, (((1,), (1,)), ((), ())), preferred_element_type=F32)


def _flash(score, v_ref, n_sub, n_chunks, bkv, s_ref, mx_ref, m_ref, l_ref, acc_ref, block_cols=False):
    m_ref[...] = jnp.full(m_ref.shape, -jnp.inf, F32)
    l_ref[...] = jnp.zeros(l_ref.shape, F32)
    acc_ref[...] = jnp.zeros(acc_ref.shape, F32)

    def item(k):
        if isinstance(k, int):
            return k // n_chunks, k % n_chunks
        t = lax.div(k, n_chunks)
        return t, k - t * n_chunks

    def issue(k, slot):
        t, c = item(k)
        score(t, c, slot)

    def absorb(k, slot):
        t, c = item(k)
        m_prev = m_ref[t]
        m_new = jnp.maximum(m_prev, mx_ref[slot])
        alpha = jnp.exp2(m_prev - m_new)
        kt = MXU_DEPTH if (block_cols and bkv % MXU_DEPTH == 0) else bkv
        start = c * bkv
        lsum = jnp.zeros(m_prev.shape, F32)
        acc = alpha * acc_ref[t]
        for j in range(bkv // kt):
            p = jnp.exp2(s_ref[slot, :, j * kt:(j + 1) * kt] - m_new)
            lsum = lsum + jnp.sum(p, axis=1, keepdims=True)
            acc = acc + jnp.dot(p.astype(BF16), v_ref[_rows(start + j * kt, kt), :], preferred_element_type=F32)
        l_ref[t] = alpha * l_ref[t] + lsum
        acc_ref[t] = acc
        m_ref[t] = m_new

    n_items = n_sub * n_chunks
    issue(0, 0)

    def body(j, carry):
        k = ITEMS_PER_TRIP * j
        for i in range(ITEMS_PER_TRIP):
            issue(k + i + 1, (i + 1) % 2)
            absorb(k + i, i % 2)
        return carry

    n_trips = (n_items - 1) // ITEMS_PER_TRIP
    if n_trips:
        lax.fori_loop(0, n_trips, body, 0)
    for k in range(ITEMS_PER_TRIP * n_trips, n_items):
        if k + 1 < n_items:
            issue(k + 1, (k + 1) % 2)
        absorb(k, k % 2)


def _score_into(s_ref, mx_ref, slot, row0, q, kc):
    s = _dot_nt(q, kc)
    rows = q.shape[0]
    s_ref[slot, row0:row0 + rows] = s
    mx_ref[slot, row0:row0 + rows] = jnp.max(s, axis=1, keepdims=True)


def _mla_attn_kernel(q_ref, k_ref, v_ref, *rest, n_sub, n_chunks, bkv):
    o_ref, s_ref, mx_ref, m_ref, l_ref, acc_ref = rest[-6:]
    m_rows = s_ref.shape[1]

    def score(t, c, slot):
        _score_into(s_ref, mx_ref, slot, 0, q_ref[_rows(t * m_rows, m_rows), :], k_ref[_rows(c * bkv, bkv), :])

    _flash(score, v_ref, n_sub, n_chunks, bkv, s_ref, mx_ref, m_ref, l_ref, acc_ref)
    for t in range(n_sub):
        o_ref[t * m_rows:(t + 1) * m_rows, :] = (acc_ref[t] / l_ref[t]).astype(BF16)


def _gqa_attn_kernel(q_ref, k_ref, v_ref, *rest, n_sub, n_chunks, bkv):
    o_ref, qs_ref, s_ref, mx_ref, m_ref, l_ref, acc_ref = rest[-7:]
    g = GQA_HEADS // GQA_KV_HEADS
    bqs = s_ref.shape[1] // g
    for t in range(n_sub):
        for j in range(g):
            qs_ref[t, j * bqs:(j + 1) * bqs] = q_ref[t * bqs:(t + 1) * bqs, j * LANE:(j + 1) * LANE]

    def score(t, c, slot):
        _score_into(s_ref, mx_ref, slot, 0, qs_ref[t], k_ref[_rows(c * bkv, bkv), :])

    _flash(score, v_ref, n_sub, n_chunks, bkv, s_ref, mx_ref, m_ref, l_ref, acc_ref)
    for t in range(n_sub):
        o = acc_ref[t] / l_ref[t]
        for j in range(g):
            o_ref[t * bqs:(t + 1) * bqs, j * LANE:(j + 1) * LANE] = o[j * bqs:(j + 1) * bqs].astype(BF16)


def _diff_attn_kernel(q_ref, k_ref, v_ref, lam_ref, dn_ref, *rest, n_sub, n_chunks, bkv, lam_init):
    o_ref, s_ref, mx_ref, m_ref, l_ref, acc_ref = rest[-6:]
    d = DIFF_HEAD_DIM
    bqs = s_ref.shape[1] // 2
    lp = lam_ref[...]
    lam = (jnp.exp(jnp.sum(lp[0:1] * lp[1:2], axis=1, keepdims=True))
           - jnp.exp(jnp.sum(lp[2:3] * lp[3:4], axis=1, keepdims=True)) + lam_init)

    def score(t, c, slot):
        qr, kr = _rows(t * bqs, bqs), _rows(c * bkv, bkv)
        _score_into(s_ref, mx_ref, slot, 0, q_ref[qr, 0:d], k_ref[kr, 0:d])
        _score_into(s_ref, mx_ref, slot, bqs, q_ref[qr, d:2 * d], k_ref[kr, d:2 * d])

    _flash(score, v_ref, n_sub, n_chunks, bkv, s_ref, mx_ref, m_ref, l_ref, acc_ref, block_cols=True)
    for t in range(n_sub):
        o = acc_ref[t] / l_ref[t]
        o = o[:bqs] - lam * o[bqs:]
        o_ref[t * bqs:(t + 1) * bqs, :] = (_rms(o, dn_ref[...]) * (1.0 - lam_init)).astype(BF16)


def _attention(kind, q, k, v, s_lat, out_rows, extra=(), ctx=False, into=None):
    s_all = q.shape[0]
    n_ctx = s_all - s_lat
    assert s_lat % n_ctx == 0
    if kind == "mla":
        heads, qw, kw, vw, ow, stack = MLA_HEADS, MLA_QK_PAD, MLA_QK_PAD, MLA_V, MLA_V, 1
        bqs = n_ctx if ctx else _tile(s_lat, (1024, 512, 256))
        body = _mla_attn_kernel
    elif kind == "gqa":
        g = GQA_HEADS // GQA_KV_HEADS
        heads, qw, kw, vw, ow, stack = GQA_KV_HEADS, g * LANE, LANE, LANE, g * LANE, g
        bqs = n_ctx if ctx else 256
        body = _gqa_attn_kernel
    else:
        heads, qw, kw, vw, ow, stack = DIFF_HEADS, 2 * LANE, 2 * LANE, DIFF_V_DIM, DIFF_V_DIM, 2
        bqs = n_ctx if ctx else _tile(s_lat, (512, 256))
        body = functools.partial(_diff_attn_kernel, lam_init=extra[2])
        extra = extra[:2]
    m_rows = stack * bqs
    n_sub = 1 if ctx else _tile(s_lat // bqs, (4, 3, 2, 1))
    bq = n_sub * bqs
    if ctx:
        kv_rows, bkv, row_blk, nq = n_ctx, n_ctx, s_lat // n_ctx, 1
    else:
        kv_rows, bkv, row_blk, nq = s_all, _tile(s_all, (1280, 1024, 512, 256)), 0, s_lat // bq
    body = functools.partial(body, n_sub=n_sub, n_chunks=kv_rows // bkv, bkv=bkv)
    kv_blk = row_blk
    in_specs = [pl.BlockSpec((bq, qw), lambda h, i: (row_blk + i, h)),
                pl.BlockSpec((kv_rows, kw), lambda h, i: (kv_blk, h), pipeline_mode=pl.Buffered(1)),
                pl.BlockSpec((kv_rows, vw), lambda h, i: (kv_blk, h), pipeline_mode=pl.Buffered(1))]
    in_specs += [pl.BlockSpec(e.shape, lambda h, i: (0, 0)) for e in extra]
    args = [q, k, v, *extra]
    aliases = {}
    if into is not None:
        in_specs.append(pl.BlockSpec(memory_space=pl.ANY))
        aliases = {len(args): 0}
        args.append(into)
    scratch = []
    if kind == "gqa":
        scratch.append(pltpu.VMEM((n_sub, m_rows, LANE), BF16))
    scratch += [pltpu.VMEM((2, m_rows, bkv), F32),
                pltpu.VMEM((2, m_rows, 1), F32), pltpu.VMEM((n_sub, m_rows, 1), F32),
                pltpu.VMEM((n_sub, m_rows, 1), F32), pltpu.VMEM((n_sub, m_rows, vw), F32)]
    return pl.pallas_call(
        body,
        grid=(heads, nq),
        in_specs=in_specs,
        out_specs=pl.BlockSpec((bq, ow), lambda h, i: (row_blk + i, h)),
        out_shape=jax.ShapeDtypeStruct((out_rows, heads * ow), BF16),
        scratch_shapes=scratch,
        input_output_aliases=aliases,
        compiler_params=_params(("parallel", "arbitrary")),
        name=kind + ("_attn_ctx" if ctx else "_attn"),
    )(*args)


def _layer_norm(z, g, b):
    mu = jnp.mean(z, axis=-1, keepdims=True)
    zc = z - mu
    var = jnp.mean(zc * zc, axis=-1, keepdims=True)
    return zc * lax.rsqrt(var + LN_EPS) * g + b


def _merge_kernel(om_ref, og_ref, od_ref, g_ref, x_ref, mod_ref, wbm_ref, wbg_ref, wbd_ref, wo_ref,
                  lng_ref, lnb_ref, xo_ref, h2_ref, *, tm, s_lat):
    d = D_MODEL
    row0 = pl.program_id(0) * tm
    merged = g_ref[:, 0:d].astype(F32) * jnp.dot(om_ref[...], wbm_ref[...], preferred_element_type=F32)
    merged += g_ref[:, d:2 * d].astype(F32) * jnp.dot(og_ref[...], wbg_ref[...], preferred_element_type=F32)
    merged += g_ref[:, 2 * d:3 * d].astype(F32) * jnp.dot(od_ref[...], wbd_ref[...], preferred_element_type=F32)
    y = jnp.dot(merged.astype(BF16), wo_ref[...], preferred_element_type=F32)
    z = DEEPNORM_ALPHA * x_ref[...] + _row_mod(mod_ref, 2, row0, tm, s_lat) * y
    xn = _layer_norm(z, lng_ref[...], lnb_ref[...])
    xo_ref[...] = xn
    h2 = xn * (1.0 + _row_mod(mod_ref, 4, row0, tm, s_lat)) + _row_mod(mod_ref, 3, row0, tm, s_lat)
    h2_ref[...] = _pack_pairs(h2)


def _merge(om, og, od, gates, xs, mod, wbm, wbg, wbd, wo, lng, lnb, rows, s_lat):
    d = D_MODEL
    tm = 256
    row = lambda i: (i, 0)
    full = lambda i: (0, 0)
    resident = lambda a: pl.BlockSpec(a.shape, full, pipeline_mode=pl.Buffered(1))
    return pl.pallas_call(
        functools.partial(_merge_kernel, tm=tm, s_lat=s_lat),
        grid=(rows // tm,),
        in_specs=[pl.BlockSpec((tm, om.shape[1]), row), pl.BlockSpec((tm, og.shape[1]), row),
                  pl.BlockSpec((tm, od.shape[1]), row), pl.BlockSpec((tm, gates.shape[1]), row),
                  pl.BlockSpec((tm, d), row), pl.BlockSpec(mod.shape, full),
                  resident(wbm), resident(wbg), resident(wbd), resident(wo),
                  pl.BlockSpec(lng.shape, full), pl.BlockSpec(lnb.shape, full)],
        out_specs=[pl.BlockSpec((tm, d), row), pl.BlockSpec((tm, d // 2), row)],
        out_shape=[jax.ShapeDtypeStruct((rows, d), F32), jax.ShapeDtypeStruct((rows, d // 2), jnp.uint32)],
        compiler_params=_params(("parallel",)),
        name="merge",
    )(om, og, od, gates, xs, mod, wbm, wbg, wbd, wo, lng, lnb)


def _router_kernel(x_ref, mod_ref, wr_ref, br_ref, ri_ref, rw_ref, cnt_ref, run_ref, *, tm, s_lat):
    @pl.when(pl.program_id(0) == 0)
    def _():
        run_ref[...] = jnp.zeros(run_ref.shape, F32)

    row0 = pl.program_id(0) * tm
    h2 = x_ref[...] * (1.0 + _row_mod(mod_ref, 4, row0, tm, s_lat)) + _row_mod(mod_ref, 3, row0, tm, s_lat)
    logits = lax.dot_general(wr_ref[...], h2, (((1,), (1,)), ((), ())),
                             precision=HIGHEST, preferred_element_type=F32)
    scores = jax.nn.sigmoid(logits)
    biased = scores + br_ref[...]
    sc = [scores[e:e + 1, :] for e in range(N_EXPERTS)]
    bi = [biased[e:e + 1, :] for e in range(N_EXPERTS)]
    gs = []
    for g in range(N_GROUPS):
        a, b, c, dd = bi[4 * g:4 * g + 4]
        hi1, lo1, hi2, lo2 = jnp.maximum(a, b), jnp.minimum(a, b), jnp.maximum(c, dd), jnp.minimum(c, dd)
        gs.append(jnp.maximum(hi1, hi2) + jnp.maximum(jnp.minimum(hi1, hi2), jnp.maximum(lo1, lo2)))
    best = jnp.maximum(jnp.maximum(gs[0], gs[1]), jnp.maximum(gs[2], gs[3]))
    gsel = jnp.where(gs[0] == best, 0, jnp.where(gs[1] == best, 1, jnp.where(gs[2] == best, 2, 3)))
    pick = lambda vals, j: jnp.where(gsel == 0, vals[j], jnp.where(gsel == 1, vals[4 + j],
                                     jnp.where(gsel == 2, vals[8 + j], vals[12 + j])))
    xb = [pick(bi, j) for j in range(EXPERTS_PER_GROUP)]
    xs = [pick(sc, j) for j in range(EXPERTS_PER_GROUP)]
    sel, w = [], []
    for i in range(EXPERTS_PER_GROUP):
        rank = jnp.zeros_like(gsel)
        for j in range(EXPERTS_PER_GROUP):
            if j == i:
                continue
            beats = (xb[j] >= xb[i]) if j < i else (xb[j] > xb[i])
            rank = rank + beats.astype(jnp.int32)
        sel.append(rank < 2)
        w.append(jnp.where(rank < 2, xs[i], 0.0))
    inv = ROUTED_SCALE / (w[0] + w[1] + w[2] + w[3])
    ia = jnp.where(sel[0], 0, jnp.where(sel[1], 1, jnp.where(sel[2], 2, 3)))
    ib = jnp.where(sel[3], 3, jnp.where(sel[2], 2, jnp.where(sel[1], 1, 0)))
    local = lambda idx: jnp.where(idx == 0, w[0], jnp.where(idx == 1, w[1], jnp.where(idx == 2, w[2], w[3])))
    ea = gsel * EXPERTS_PER_GROUP + ia
    eb = gsel * EXPERTS_PER_GROUP + ib
    eidx = lax.broadcasted_iota(jnp.int32, (N_EXPERTS, tm), 0)
    is_a, is_b = eidx == ea, eidx == eb
    onehot = jnp.where(is_a | is_b, 1.0, 0.0)
    earlier = jnp.where(lax.broadcasted_iota(jnp.int32, (tm, tm), 0) < lax.broadcasted_iota(jnp.int32, (tm, tm), 1),
                        1.0, 0.0).astype(BF16)
    pos = run_ref[...] + jnp.dot(onehot.astype(BF16), earlier, preferred_element_type=F32)
    ri_ref[0:1, :] = ea
    ri_ref[1:2, :] = eb
    ri_ref[2:3, :] = jnp.sum(jnp.where(is_a, pos, 0.0), axis=0, keepdims=True).astype(jnp.int32)
    ri_ref[3:4, :] = jnp.sum(jnp.where(is_b, pos, 0.0), axis=0, keepdims=True).astype(jnp.int32)
    rw_ref[0:1, :] = local(ia) * inv
    rw_ref[1:2, :] = local(ib) * inv
    run = run_ref[...] + jnp.sum(onehot, axis=1, keepdims=True)
    run_ref[...] = run
    cnt_ref[...] = jnp.broadcast_to(run, cnt_ref.shape)


def _router(xs, mod, wr_t, br, rows, s_lat):
    d = D_MODEL
    tm = _tile(rows, (640, 512, 256))
    return pl.pallas_call(
        functools.partial(_router_kernel, tm=tm, s_lat=s_lat),
        grid=(rows // tm,),
        in_specs=[pl.BlockSpec((tm, d), lambda i: (i, 0)), pl.BlockSpec(mod.shape, lambda i: (0, 0)),
                  pl.BlockSpec(wr_t.shape, lambda i: (0, 0)), pl.BlockSpec(br.shape, lambda i: (0, 0))],
        out_specs=[pl.BlockSpec((4, tm), lambda i: (0, i)), pl.BlockSpec((2, tm), lambda i: (0, i)),
                   pl.BlockSpec((N_EXPERTS, LANE), lambda i: (0, 0))],
        out_shape=[jax.ShapeDtypeStruct((4, rows), jnp.int32), jax.ShapeDtypeStruct((2, rows), F32),
                   jax.ShapeDtypeStruct((N_EXPERTS, LANE), F32)],
        scratch_shapes=[pltpu.VMEM((N_EXPERTS, 1), F32)],
        compiler_params=_params(("arbitrary",)),
        name="router",
    )(xs, mod, wr_t, br)


MOE_TILE = 512
HALF = D_MODEL // 2
DMA_UNROLL = 8


def _pack_pairs(h):
    hi = pltpu.bitcast(h[:, :HALF].astype(BF16).astype(F32), jnp.uint32)
    lo = pltpu.bitcast(h[:, HALF:].astype(BF16).astype(F32), jnp.uint32)
    return (hi & jnp.uint32(0xFFFF0000)) | (lo >> jnp.uint32(16))


def _unpack_pairs(w):
    hi = pltpu.bitcast(w & jnp.uint32(0xFFFF0000), F32)
    lo = pltpu.bitcast(w << jnp.uint32(16), F32)
    return jnp.concatenate([hi, lo], axis=1).astype(BF16)


def _swiglu(x, wg, wu, wd):
    a = jnp.dot(x, wg, preferred_element_type=F32)
    b = jnp.dot(x, wu, preferred_element_type=F32)
    return jnp.dot((a * jax.nn.sigmoid(a) * b).astype(BF16), wd, preferred_element_type=F32)


def _moe_sort_kernel(dest_ref, h_ref, init_ref, o_ref, sem, *, tg):
    del init_ref
    base = pl.program_id(0) * tg

    def copy(r, j):
        return pltpu.make_async_copy(h_ref.at[pl.ds(r, 1)], o_ref.at[pl.ds(dest_ref[j, base + r], 1)], sem)

    def start(r, carry):
        for j in range(2):
            copy(r, j).start()
        return carry

    def wait(r, carry):
        for j in range(2):
            copy(r, j).wait()
        return carry

    lax.fori_loop(0, tg, start, 0, unroll=DMA_UNROLL)
    lax.fori_loop(0, tg, wait, 0, unroll=DMA_UNROLL)


def _moe_sort(dest, h2p, n_sorted):
    rows, half = h2p.shape
    tg = _tile(rows, (1280, 1024, 512, 256))
    return pl.pallas_call(
        functools.partial(_moe_sort_kernel, tg=tg),
        grid_spec=pltpu.PrefetchScalarGridSpec(
            num_scalar_prefetch=1, grid=(rows // tg,),
            in_specs=[pl.BlockSpec((tg, half), lambda i, dref: (i, 0)), pl.BlockSpec(memory_space=pl.ANY)],
            out_specs=pl.BlockSpec(memory_space=pl.ANY),
            scratch_shapes=[pltpu.SemaphoreType.DMA]),
        out_shape=jax.ShapeDtypeStruct((n_sorted, half), jnp.uint32),
        input_output_aliases={2: 0},
        compiler_params=_params(("arbitrary",)),
        name="moe_sort",
    )(dest, h2p, jnp.zeros((n_sorted, half), jnp.uint32))


def _moe_expert_kernel(te_ref, na_ref, x_ref, wg_ref, wu_ref, wd_ref, y_ref):
    del te_ref
    live = pl.program_id(0) < na_ref[0]

    @pl.when(live)
    def _():
        y_ref[...] = _swiglu(_unpack_pairs(x_ref[...]), wg_ref[0, 0].astype(BF16), wu_ref[0, 0].astype(BF16),
                             wd_ref[0, 0].astype(BF16))

    @pl.when(jnp.logical_not(live))
    def _():
        y_ref[...] = jnp.zeros(y_ref.shape, F32)


def _moe_experts(tile_expert, n_active, xsorted, wg, wu, wd, layer):
    n_sorted, half = xsorted.shape
    _, _, d, f = wg.shape
    tme = MOE_TILE
    wsel = lambda i, te, na: (layer, te[i], 0, 0)
    return pl.pallas_call(
        _moe_expert_kernel,
        grid_spec=pltpu.PrefetchScalarGridSpec(
            num_scalar_prefetch=2, grid=(n_sorted // tme,),
            in_specs=[pl.BlockSpec((tme, half), lambda i, te, na: (i, 0)),
                      pl.BlockSpec((1, 1, d, f), wsel), pl.BlockSpec((1, 1, d, f), wsel),
                      pl.BlockSpec((1, 1, f, d), wsel)],
            out_specs=pl.BlockSpec((tme, d), lambda i, te, na: (i, 0))),
        out_shape=jax.ShapeDtypeStruct((n_sorted, d), F32),
        compiler_params=_params(("arbitrary",)),
        name="moe_experts",
    )(tile_expert, n_active, xsorted, wg, wu, wd)


def _moe_final_kernel(dest_ref, hp_ref, rw_ref, y_ref, sg_ref, su_ref, sd_ref, x_ref, mod_ref, lng_ref, lnb_ref,
                      o_ref, ya_ref, yb_ref, sem, *, tm, s_lat):
    base = pl.program_id(0) * tm
    bufs = (ya_ref, yb_ref)

    def copy(r, j):
        return pltpu.make_async_copy(y_ref.at[pl.ds(dest_ref[j, base + r], 1)], bufs[j].at[pl.ds(r, 1)], sem.at[j])

    def start(r, carry):
        for j in range(2):
            copy(r, j).start()
        return carry

    def wait(r, carry):
        for j in range(2):
            copy(r, j).wait()
        return carry

    lax.fori_loop(0, tm, start, 0, unroll=DMA_UNROLL)
    shared = _swiglu(_unpack_pairs(hp_ref[...]), sg_ref[...], su_ref[...], sd_ref[...])
    lax.fori_loop(0, tm, wait, 0, unroll=DMA_UNROLL)
    y = shared + rw_ref[:, 0:1] * ya_ref[...] + rw_ref[:, 1:2] * yb_ref[...]
    z = DEEPNORM_ALPHA * x_ref[...] + _row_mod(mod_ref, 5, base, tm, s_lat) * y
    o_ref[...] = _layer_norm(z, lng_ref[...], lnb_ref[...])


def _moe_final(dest, h2p, rw, ysorted, sg, su, sd, xs, mod, lng, lnb, rows, s_lat):
    d = D_MODEL
    tm = 256
    row = lambda i, dref: (i, 0)
    full = lambda i, dref: (0, 0)
    return pl.pallas_call(
        functools.partial(_moe_final_kernel, tm=tm, s_lat=s_lat),
        grid_spec=pltpu.PrefetchScalarGridSpec(
            num_scalar_prefetch=1, grid=(rows // tm,),
            in_specs=[pl.BlockSpec((tm, HALF), row), pl.BlockSpec((tm, 2), row), pl.BlockSpec(memory_space=pl.ANY),
                      pl.BlockSpec(sg.shape, full), pl.BlockSpec(su.shape, full), pl.BlockSpec(sd.shape, full),
                      pl.BlockSpec((tm, d), row), pl.BlockSpec(mod.shape, full),
                      pl.BlockSpec(lng.shape, full), pl.BlockSpec(lnb.shape, full)],
            out_specs=pl.BlockSpec((tm, d), row),
            scratch_shapes=[pltpu.VMEM((tm, d), F32), pltpu.VMEM((tm, d), F32), pltpu.SemaphoreType.DMA((2,))]),
        out_shape=jax.ShapeDtypeStruct((rows, d), F32),
        compiler_params=_params(("arbitrary",)),
        name="moe_final",
    )(dest, h2p, rw, ysorted, sg, su, sd, xs, mod, lng, lnb)


def _moe_plan(ri, cnt, rows):
    tme = MOE_TILE
    n_sorted = 2 * rows + N_EXPERTS * tme
    counts = cnt[:, 0].astype(jnp.int32)
    padded = (counts + tme - 1) // tme * tme
    ends = jnp.cumsum(padded)
    offs = ends - padded
    eidx = jnp.arange(N_EXPERTS, dtype=jnp.int32)[:, None]
    seg = lambda e: jnp.sum(jnp.where(eidx == e[None, :], offs[:, None], 0), axis=0)
    dest = jnp.stack([seg(ri[0]) + ri[2], seg(ri[1]) + ri[3]])
    tile_start = jnp.arange(n_sorted // tme, dtype=jnp.int32) * tme
    tile_expert = jnp.minimum(jnp.sum((ends[None, :] <= tile_start[:, None]).astype(jnp.int32), axis=1),
                              N_EXPERTS - 1)
    n_active = (ends[-1:] // tme).astype(jnp.int32)
    return dest, tile_expert, n_active, n_sorted


def _moe(xs, h2p, mod, wr_t, br, wg, wu, wd, layer, sg, su, sd, lng, lnb, rows, s_lat):
    ri, rw, cnt = _router(xs, mod, wr_t, br, rows, s_lat)
    dest, tile_expert, n_active, n_sorted = _moe_plan(ri, cnt, rows)
    xsorted = _moe_sort(dest, h2p, n_sorted)
    ysorted = _moe_experts(tile_expert, n_active, xsorted, wg, wu, wd, layer)
    return _moe_final(dest, h2p, rw.T, ysorted, sg, su, sd, xs, mod, lng, lnb, rows, s_lat)


def _rope_tables(s_lat, n_ctx, dim):
    quarter = dim // 4
    rows = s_lat // GRID_W
    inv_freq = ROPE_THETA ** (-jnp.arange(quarter, dtype=F32) / quarter)
    ang_r = jnp.arange(rows, dtype=F32)[:, None] * inv_freq
    ang_c = jnp.arange(GRID_W, dtype=F32)[:, None] * inv_freq
    per_row = lambda t: jnp.repeat(t, GRID_W, axis=0)
    per_col = lambda t: jnp.tile(t, (rows, 1))
    cos_r, sin_r = per_row(jnp.cos(ang_r)), per_row(jnp.sin(ang_r))
    cos_c, sin_c = per_col(jnp.cos(ang_c)), per_col(jnp.sin(ang_c))
    zero = jnp.zeros_like(cos_r)
    pad = jnp.zeros((s_lat, LANE - dim), F32)
    cos = jnp.concatenate([cos_r, cos_r, cos_c, cos_c, pad + 1.0], axis=1)
    sa = jnp.concatenate([-sin_r, zero, -sin_c, zero, pad], axis=1)
    sb = jnp.concatenate([zero, sin_r, zero, sin_c, pad], axis=1)
    ctx1 = jnp.ones((n_ctx, LANE), F32)
    ctx0 = jnp.zeros((n_ctx, LANE), F32)
    return (jnp.concatenate([cos, ctx1], axis=0), jnp.concatenate([sa, ctx0], axis=0),
            jnp.concatenate([sb, ctx0], axis=0))


def kernel(x, c, ctx, c_ctx, w_ada, b_ada, w_in, mla_q_norm, mla_w_uq, mla_kv_norm, mla_w_ukv, gqa_q_norm,
           gqa_k_norm, diff_lambda, diff_norm, w_br_mla, w_br_gqa, w_br_diff, w_o, ln1_g, ln1_b, w_router,
           b_router, moe_w_gate, moe_w_up, moe_w_down, shared_w_gate, shared_w_up, shared_w_down, ln2_g, ln2_b):
    B, s_lat, d = x.shape
    n_ctx = ctx.shape[1]
    assert B == 1 and d == D_MODEL
    s_all = s_lat + n_ctx

    xs = jnp.concatenate([x[0], ctx[0]], axis=0)
    cc = jnp.zeros((8, d), F32).at[0].set(c[0]).at[1].set(c_ctx)
    mod_all = _adaln(cc, w_ada, b_ada)

    tabs64 = _rope_tables(s_lat, n_ctx, MLA_ROPE)
    tabs128 = _rope_tables(s_lat, n_ctx, GQA_HEAD_DIM)
    wr_t = w_router.T
    br = b_router.reshape(N_EXPERTS, 1)
    row2 = lambda v: v.reshape(1, -1)

    o0 = MLA_Q_LORA + MLA_KV_LORA + MLA_ROPE
    o1 = o0 + GQA_HEADS * GQA_HEAD_DIM + 2 * GQA_KV_HEADS * GQA_HEAD_DIM
    o2 = o1 + 2 * (2 * DIFF_HEADS * DIFF_HEAD_DIM) + DIFF_HEADS * DIFF_V_DIM

    for l in range(DEPTH):
        last = l == DEPTH - 1
        mod = mod_all[l]
        w = w_in[l]
        wa = jnp.pad(w[:, :o0], ((0, 0), (0, LANE - MLA_ROPE))).astype(BF16)
        wb = w[:, o0:o1].astype(BF16)
        wc = w[:, o1:o2].astype(BF16)
        wd_gate = w[:, o2:].astype(BF16)
        wuq = jnp.pad(mla_w_uq[l].reshape(MLA_Q_LORA, MLA_HEADS, MLA_NOPE + MLA_ROPE),
                      ((0, 0), (0, 0), (0, MLA_QK_PAD - MLA_NOPE - MLA_ROPE))
                      ).reshape(MLA_Q_LORA, MLA_HEADS * MLA_QK_PAD).astype(BF16)
        ukv = mla_w_ukv[l].reshape(MLA_KV_LORA, MLA_HEADS, MLA_NOPE + MLA_V)
        wukv = jnp.concatenate([ukv[:, :, :MLA_NOPE].reshape(MLA_KV_LORA, -1),
                                ukv[:, :, MLA_NOPE:].reshape(MLA_KV_LORA, -1)], axis=1).astype(BF16)

        h, q_mla, k_mla, v_mla = _mla_proj(xs, mod, wa, row2(mla_q_norm[l]), row2(mla_kv_norm[l]),
                                           wuq, wukv, tabs64, s_lat)
        q_gqa, k_gqa, v_gqa = _gqa_proj(h, wb, row2(gqa_q_norm[l]), row2(gqa_k_norm[l]), tabs128)
        q_dif, k_dif, v_dif = _diff_proj(h, wc, tabs128)
        gates = _gate_proj(h, wd_gate)

        lam_init = 0.8 - 0.6 * math.exp(-0.3 * l)
        rows = s_lat if last else s_all
        dif_extra = (diff_lambda[l], row2(diff_norm[l]), lam_init)
        fresh = lambda width: None if last else jnp.zeros((rows, width), BF16)
        o_mla = _attention("mla", q_mla, k_mla, v_mla, s_lat, rows, into=fresh(MLA_HEADS * MLA_V))
        o_gqa = _attention("gqa", q_gqa, k_gqa, v_gqa, s_lat, rows, into=fresh(GQA_HEADS * GQA_HEAD_DIM))
        o_dif = _attention("diff", q_dif, k_dif, v_dif, s_lat, rows, extra=dif_extra,
                           into=fresh(DIFF_HEADS * DIFF_V_DIM))
        if not last:
            o_mla = _attention("mla", q_mla, k_mla, v_mla, s_lat, rows, ctx=True, into=o_mla)
            o_gqa = _attention("gqa", q_gqa, k_gqa, v_gqa, s_lat, rows, ctx=True, into=o_gqa)
            o_dif = _attention("diff", q_dif, k_dif, v_dif, s_lat, rows, extra=dif_extra, ctx=True, into=o_dif)

        xs, h2p = _merge(o_mla, o_gqa, o_dif, gates, xs, mod, w_br_mla[l].astype(BF16), w_br_gqa[l].astype(BF16),
                         w_br_diff[l].astype(BF16), w_o[l].astype(BF16), row2(ln1_g[l]), row2(ln1_b[l]), rows, s_lat)

        xs = _moe(xs, h2p, mod, wr_t, br, moe_w_gate, moe_w_up, moe_w_down, l,
                  shared_w_gate[l].astype(BF16), shared_w_up[l].astype(BF16), shared_w_down[l].astype(BF16),
                  row2(ln2_g[l]), row2(ln2_b[l]), rows, s_lat)

    return xs[None]
```

```python
import functools
import math

import jax
import jax.numpy as jnp
from jax import lax
from jax.experimental import pallas as pl
from jax.experimental.pallas import tpu as pltpu

F32 = jnp.float32
BF16 = jnp.bfloat16
HIGHEST = lax.Precision.HIGHEST

D_MODEL = 2048
DEPTH = 2
GRID_W = 64
ROPE_THETA = 10000.0
LN_EPS = 1e-5
RMS_EPS = 1e-6
MLA_HEADS = 8
MLA_Q_LORA = 512
MLA_KV_LORA = 512
MLA_NOPE = 128
MLA_ROPE = 64
MLA_V = 128
GQA_HEADS = 8
GQA_KV_HEADS = 2
GQA_HEAD_DIM = 128
DIFF_HEADS = 4
DIFF_HEAD_DIM = 128
DIFF_V_DIM = 2 * DIFF_HEAD_DIM
N_EXPERTS = 16
N_GROUPS = 4
EXPERTS_PER_GROUP = N_EXPERTS // N_GROUPS
MOE_D_FF = 512
ROUTED_SCALE = 1.0
DEEPNORM_ALPHA = (2 * DEPTH) ** 0.25

LANE = 128
MLA_QK_PAD = 2 * LANE
LOG2E = math.log2(math.e)
VMEM_LIMIT = 56 * 1024 * 1024


def _tile(rows, prefs):
    for t in prefs:
        if rows % t == 0:
            return t
    raise ValueError(f"no tile in {prefs} divides {rows}")


W_IN_D = (0, 3 * D_MODEL)
W_IN_C = (6144, 3072)
W_IN_B = (9216, 1536)
W_IN_A = (11520, 1152)
W_IN_COLS = W_IN_A[0] + W_IN_A[1]


def _w_in_block(seg):
    off, width = seg
    assert off % width == 0
    return pl.BlockSpec((D_MODEL, width), lambda *_: (0, off // width))


def _w_in_layout(w):
    o0 = MLA_Q_LORA + MLA_KV_LORA + MLA_ROPE
    o1 = o0 + GQA_HEADS * GQA_HEAD_DIM + 2 * GQA_KV_HEADS * GQA_HEAD_DIM
    o2 = o1 + 2 * (2 * DIFF_HEADS * DIFF_HEAD_DIM) + DIFF_HEADS * DIFF_V_DIM
    assert (w.shape[1] - o2, o2 - o1, o1 - o0) == (W_IN_D[1], W_IN_C[1], W_IN_B[1])
    zeros = lambda n: jnp.zeros((w.shape[0], n), w.dtype)
    gap = W_IN_A[0] - (W_IN_B[0] + W_IN_B[1])
    out = jnp.concatenate([w[:, o2:], w[:, o1:o2], w[:, o0:o1], zeros(gap), w[:, :o0], zeros(W_IN_A[1] - o0)],
                          axis=1).astype(BF16)
    assert out.shape[1] == W_IN_COLS
    return out


def _params(sem, vmem=VMEM_LIMIT, flags=None):
    return pltpu.CompilerParams(dimension_semantics=sem, vmem_limit_bytes=vmem, flags=flags)


def _row_mod(mod_ref, k, row0, tm, s_lat):
    d = D_MODEL
    lat = mod_ref[0:1, k * d:(k + 1) * d]
    ctx = mod_ref[1:2, k * d:(k + 1) * d]
    rows = row0 + lax.broadcasted_iota(jnp.int32, (tm, 1), 0)
    return jnp.where(rows >= s_lat, ctx, lat)


def _rms(x, g):
    return x * lax.rsqrt(jnp.mean(x * x, axis=-1, keepdims=True) + RMS_EPS) * g


def _rope(t, cos, sa, sb, quarter):
    return t * cos + pltpu.roll(t, LANE - quarter, 1) * sa + pltpu.roll(t, quarter, 1) * sb


def _adaln_kernel(c_ref, w_ref, b_ref, o_ref):
    a = c_ref[...]
    a = a * jax.nn.sigmoid(a)
    o_ref[0] = jnp.dot(a, w_ref[0], precision=HIGHEST, preferred_element_type=F32) + b_ref[0]


def _adaln(cc, w_ada, b_ada):
    L, d, n = w_ada.shape
    tn = 1024
    return pl.pallas_call(
        _adaln_kernel,
        grid=(L, n // tn),
        in_specs=[pl.BlockSpec((8, d), lambda l, j: (0, 0)),
                  pl.BlockSpec((1, d, tn), lambda l, j: (l, 0, j)),
                  pl.BlockSpec((1, 1, tn), lambda l, j: (l, 0, j))],
        out_specs=pl.BlockSpec((1, 8, tn), lambda l, j: (l, 0, j)),
        out_shape=jax.ShapeDtypeStruct((L, 8, n), F32),
        compiler_params=_params(("arbitrary", "arbitrary")),
        name="adaln",
    )(cc, w_ada, b_ada.reshape(L, 1, n))


def _mla_proj_kernel(x_ref, mod_ref, wa_ref, qn_ref, kvn_ref, wuq_ref, wukv_ref, cos_ref, sa_ref, sb_ref,
                     h_ref, q_ref, k_ref, v_ref, *, tm, s_lat, q_scale):
    row0 = pl.program_id(0) * tm
    sh = _row_mod(mod_ref, 0, row0, tm, s_lat)
    sc = _row_mod(mod_ref, 1, row0, tm, s_lat)
    h = (x_ref[...] * (1.0 + sc) + sh).astype(BF16)
    h_ref[...] = h
    a = jnp.dot(h, wa_ref[...], preferred_element_type=F32)
    cqn = _rms(a[:, :MLA_Q_LORA], qn_ref[...]).astype(BF16)
    ckvn = _rms(a[:, MLA_Q_LORA:MLA_Q_LORA + MLA_KV_LORA], kvn_ref[...]).astype(BF16)
    kr = a[:, MLA_Q_LORA + MLA_KV_LORA:]
    q = jnp.dot(cqn, wuq_ref[...], preferred_element_type=F32)
    kv = jnp.dot(ckvn, wukv_ref[...], preferred_element_type=F32)
    cos, sa, sb = cos_ref[...], sa_ref[...], sb_ref[...]
    quarter = MLA_ROPE // 4
    kr2 = _rope(kr, cos, sa, sb, quarter).astype(BF16)
    for hh in range(MLA_HEADS):
        c0 = hh * MLA_QK_PAD
        q_ref[:, c0:c0 + LANE] = (q[:, c0:c0 + LANE] * q_scale).astype(BF16)
        q_ref[:, c0 + LANE:c0 + 2 * LANE] = (_rope(q[:, c0 + LANE:c0 + 2 * LANE], cos, sa, sb, quarter)
                                             * q_scale).astype(BF16)
        k_ref[:, c0:c0 + LANE] = kv[:, hh * LANE:(hh + 1) * LANE].astype(BF16)
        k_ref[:, c0 + LANE:c0 + 2 * LANE] = kr2
    v_ref[...] = kv[:, MLA_HEADS * MLA_NOPE:].astype(BF16)


def _mla_proj(xs, mod, wa, qn, kvn, wuq, wukv, tabs, s_lat):
    rows, d = xs.shape
    tm = _tile(rows, (640, 512, 256))
    cos, sa, sb = tabs
    row = lambda i: (i, 0)
    full = lambda i: (0, 0)
    qk_w = MLA_HEADS * MLA_QK_PAD
    v_w = MLA_HEADS * MLA_V
    q_scale = (MLA_NOPE + MLA_ROPE) ** -0.5 * LOG2E
    return pl.pallas_call(
        functools.partial(_mla_proj_kernel, tm=tm, s_lat=s_lat, q_scale=q_scale),
        grid=(rows // tm,),
        in_specs=[pl.BlockSpec((tm, d), row), pl.BlockSpec(mod.shape, full),
                  _w_in_block(W_IN_A), pl.BlockSpec(qn.shape, full), pl.BlockSpec(kvn.shape, full),
                  pl.BlockSpec(wuq.shape, full), pl.BlockSpec(wukv.shape, full),
                  pl.BlockSpec((tm, LANE), row), pl.BlockSpec((tm, LANE), row), pl.BlockSpec((tm, LANE), row)],
        out_specs=[pl.BlockSpec((tm, d), row), pl.BlockSpec((tm, qk_w), row),
                   pl.BlockSpec((tm, qk_w), row), pl.BlockSpec((tm, v_w), row)],
        out_shape=[jax.ShapeDtypeStruct((rows, d), BF16), jax.ShapeDtypeStruct((rows, qk_w), BF16),
                   jax.ShapeDtypeStruct((rows, qk_w), BF16), jax.ShapeDtypeStruct((rows, v_w), BF16)],
        compiler_params=_params(("parallel",)),
        name="mla_proj",
    )(xs, mod, wa, qn, kvn, wuq, wukv, cos, sa, sb)


def _gqa_proj_kernel(h_ref, w_ref, qn_ref, kn_ref, cos_ref, sa_ref, sb_ref, q_ref, k_ref, v_ref, *, q_scale):
    a = jnp.dot(h_ref[...], w_ref[...], preferred_element_type=F32)
    cos, sa, sb = cos_ref[...], sa_ref[...], sb_ref[...]
    quarter = GQA_HEAD_DIM // 4
    for hh in range(GQA_HEADS):
        x = _rms(a[:, hh * LANE:(hh + 1) * LANE], qn_ref[...])
        q_ref[:, hh * LANE:(hh + 1) * LANE] = (_rope(x, cos, sa, sb, quarter) * q_scale).astype(BF16)
    k0 = GQA_HEADS * GQA_HEAD_DIM
    for hh in range(GQA_KV_HEADS):
        x = _rms(a[:, k0 + hh * LANE:k0 + (hh + 1) * LANE], kn_ref[...])
        k_ref[:, hh * LANE:(hh + 1) * LANE] = _rope(x, cos, sa, sb, quarter).astype(BF16)
    v_ref[...] = a[:, k0 + GQA_KV_HEADS * GQA_HEAD_DIM:].astype(BF16)


def _gqa_proj(h, w, qn, kn, tabs):
    rows, d = h.shape
    tm = _tile(rows, (640, 512, 256))
    cos, sa, sb = tabs
    row = lambda i: (i, 0)
    full = lambda i: (0, 0)
    qw = GQA_HEADS * GQA_HEAD_DIM
    kw = GQA_KV_HEADS * GQA_HEAD_DIM
    return pl.pallas_call(
        functools.partial(_gqa_proj_kernel, q_scale=GQA_HEAD_DIM ** -0.5 * LOG2E),
        grid=(rows // tm,),
        in_specs=[pl.BlockSpec((tm, d), row), _w_in_block(W_IN_B),
                  pl.BlockSpec(qn.shape, full), pl.BlockSpec(kn.shape, full),
                  pl.BlockSpec((tm, LANE), row), pl.BlockSpec((tm, LANE), row), pl.BlockSpec((tm, LANE), row)],
        out_specs=[pl.BlockSpec((tm, qw), row), pl.BlockSpec((tm, kw), row), pl.BlockSpec((tm, kw), row)],
        out_shape=[jax.ShapeDtypeStruct((rows, qw), BF16), jax.ShapeDtypeStruct((rows, kw), BF16),
                   jax.ShapeDtypeStruct((rows, kw), BF16)],
        compiler_params=_params(("parallel",)),
        name="gqa_proj",
    )(h, w, qn, kn, cos, sa, sb)


def _diff_proj_kernel(h_ref, w_ref, cos_ref, sa_ref, sb_ref, q_ref, k_ref, v_ref, *, q_scale):
    a = jnp.dot(h_ref[...], w_ref[...], preferred_element_type=F32)
    cos, sa, sb = cos_ref[...], sa_ref[...], sb_ref[...]
    quarter = DIFF_HEAD_DIM // 4
    n = 2 * DIFF_HEADS
    for hh in range(n):
        q_ref[:, hh * LANE:(hh + 1) * LANE] = (_rope(a[:, hh * LANE:(hh + 1) * LANE], cos, sa, sb, quarter)
                                               * q_scale).astype(BF16)
        k_ref[:, hh * LANE:(hh + 1) * LANE] = _rope(a[:, (n + hh) * LANE:(n + hh + 1) * LANE],
                                                    cos, sa, sb, quarter).astype(BF16)
    v_ref[...] = a[:, 2 * n * LANE:].astype(BF16)


def _diff_proj(h, w, tabs):
    rows, d = h.shape
    tm = _tile(rows, (640, 512, 256))
    cos, sa, sb = tabs
    row = lambda i: (i, 0)
    full = lambda i: (0, 0)
    ww = 2 * DIFF_HEADS * DIFF_HEAD_DIM
    vw = DIFF_HEADS * DIFF_V_DIM
    return pl.pallas_call(
        functools.partial(_diff_proj_kernel, q_scale=DIFF_HEAD_DIM ** -0.5 * LOG2E),
        grid=(rows // tm,),
        in_specs=[pl.BlockSpec((tm, d), row), _w_in_block(W_IN_C),
                  pl.BlockSpec((tm, LANE), row), pl.BlockSpec((tm, LANE), row), pl.BlockSpec((tm, LANE), row)],
        out_specs=[pl.BlockSpec((tm, ww), row), pl.BlockSpec((tm, ww), row), pl.BlockSpec((tm, vw), row)],
        out_shape=[jax.ShapeDtypeStruct((rows, ww), BF16), jax.ShapeDtypeStruct((rows, ww), BF16),
                   jax.ShapeDtypeStruct((rows, vw), BF16)],
        compiler_params=_params(("parallel",)),
        name="diff_proj",
    )(h, w, cos, sa, sb)


def _gate_proj_kernel(h_ref, w_ref, o_ref):
    a = jnp.dot(h_ref[...], w_ref[...], preferred_element_type=F32)
    o_ref[...] = jax.nn.sigmoid(a).astype(BF16)


def _gate_proj(h, w):
    rows, d = h.shape
    off, n = W_IN_D
    tm = _tile(rows, (640, 512, 256))
    tn = 1536
    assert off % tn == 0 and n % tn == 0
    return pl.pallas_call(
        _gate_proj_kernel,
        grid=(rows // tm, n // tn),
        in_specs=[pl.BlockSpec((tm, d), lambda i, j: (i, 0)),
                  pl.BlockSpec((d, tn), lambda i, j: (0, off // tn + j))],
        out_specs=pl.BlockSpec((tm, tn), lambda i, j: (i, j)),
        out_shape=jax.ShapeDtypeStruct((rows, n), BF16),
        compiler_params=_params(("parallel", "arbitrary")),
        name="gate_proj",
    )(h, w)


ITEMS_PER_TRIP = 2
MXU_DEPTH = 256


def _rows(start, size):
    return pl.ds(start if isinstance(start, int) else pl.multiple_of(start, size), size)


def _dot_nt(a, b):
    return lax.dot_general(a, b, (((1,), (1,)), ((), ())), preferred_element_type=F32)


def _flash(score, v_ref, n_sub, n_chunks, bkv, s_ref, mx_ref, m_ref, l_ref, acc_ref, block_cols=False):
    m_ref[...] = jnp.full(m_ref.shape, -jnp.inf, F32)
    l_ref[...] = jnp.zeros(l_ref.shape, F32)
    acc_ref[...] = jnp.zeros(acc_ref.shape, F32)

    def item(k):
        if isinstance(k, int):
            return k // n_chunks, k % n_chunks
        t = lax.div(k, n_chunks)
        return t, k - t * n_chunks

    def issue(k, slot):
        t, c = item(k)
        score(t, c, slot)

    def absorb(k, slot):
        t, c = item(k)
        m_prev = m_ref[t]
        m_new = jnp.maximum(m_prev, mx_ref[slot])
        alpha = jnp.exp2(m_prev - m_new)
        kt = MXU_DEPTH if (block_cols and bkv % MXU_DEPTH == 0) else bkv
        start = c * bkv
        lsum = jnp.zeros(m_prev.shape, F32)
        acc = alpha * acc_ref[t]
        for j in range(bkv // kt):
            p = jnp.exp2(s_ref[slot, :, j * kt:(j + 1) * kt] - m_new)
            lsum = lsum + jnp.sum(p, axis=1, keepdims=True)
            acc = acc + jnp.dot(p.astype(BF16), v_ref[_rows(start + j * kt, kt), :], preferred_element_type=F32)
        l_ref[t] = alpha * l_ref[t] + lsum
        acc_ref[t] = acc
        m_ref[t] = m_new

    n_items = n_sub * n_chunks
    issue(0, 0)

    def body(j, carry):
        k = ITEMS_PER_TRIP * j
        for i in range(ITEMS_PER_TRIP):
            issue(k + i + 1, (i + 1) % 2)
            absorb(k + i, i % 2)
        return carry

    n_trips = (n_items - 1) // ITEMS_PER_TRIP
    if n_trips:
        lax.fori_loop(0, n_trips, body, 0)
    for k in range(ITEMS_PER_TRIP * n_trips, n_items):
        if k + 1 < n_items:
            issue(k + 1, (k + 1) % 2)
        absorb(k, k % 2)


def _score_into(s_ref, mx_ref, slot, row0, q, kc):
    s = _dot_nt(q, kc)
    rows = q.shape[0]
    s_ref[slot, row0:row0 + rows] = s
    mx_ref[slot, row0:row0 + rows] = jnp.max(s, axis=1, keepdims=True)


def _mla_attn_kernel(q_ref, k_ref, v_ref, *rest, n_sub, n_chunks, bkv):
    o_ref, s_ref, mx_ref, m_ref, l_ref, acc_ref = rest[-6:]
    m_rows = s_ref.shape[1]

    def score(t, c, slot):
        _score_into(s_ref, mx_ref, slot, 0, q_ref[_rows(t * m_rows, m_rows), :], k_ref[_rows(c * bkv, bkv), :])

    _flash(score, v_ref, n_sub, n_chunks, bkv, s_ref, mx_ref, m_ref, l_ref, acc_ref)
    for t in range(n_sub):
        o_ref[t * m_rows:(t + 1) * m_rows, :] = (acc_ref[t] / l_ref[t]).astype(BF16)


def _gqa_attn_kernel(q_ref, k_ref, v_ref, *rest, n_sub, n_chunks, bkv):
    o_ref, qs_ref, s_ref, mx_ref, m_ref, l_ref, acc_ref = rest[-7:]
    g = GQA_HEADS // GQA_KV_HEADS
    bqs = s_ref.shape[1] // g
    for t in range(n_sub):
        for j in range(g):
            qs_ref[t, j * bqs:(j + 1) * bqs] = q_ref[t * bqs:(t + 1) * bqs, j * LANE:(j + 1) * LANE]

    def score(t, c, slot):
        _score_into(s_ref, mx_ref, slot, 0, qs_ref[t], k_ref[_rows(c * bkv, bkv), :])

    _flash(score, v_ref, n_sub, n_chunks, bkv, s_ref, mx_ref, m_ref, l_ref, acc_ref)
    for t in range(n_sub):
        o = acc_ref[t] / l_ref[t]
        for j in range(g):
            o_ref[t * bqs:(t + 1) * bqs, j * LANE:(j + 1) * LANE] = o[j * bqs:(j + 1) * bqs].astype(BF16)


def _diff_attn_kernel(q_ref, k_ref, v_ref, lam_ref, dn_ref, *rest, n_sub, n_chunks, bkv, lam_init):
    o_ref, s_ref, mx_ref, m_ref, l_ref, acc_ref = rest[-6:]
    d = DIFF_HEAD_DIM
    bqs = s_ref.shape[1] // 2
    lp = lam_ref[...]
    lam = (jnp.exp(jnp.sum(lp[0:1] * lp[1:2], axis=1, keepdims=True))
           - jnp.exp(jnp.sum(lp[2:3] * lp[3:4], axis=1, keepdims=True)) + lam_init)

    def score(t, c, slot):
        qr, kr = _rows(t * bqs, bqs), _rows(c * bkv, bkv)
        _score_into(s_ref, mx_ref, slot, 0, q_ref[qr, 0:d], k_ref[kr, 0:d])
        _score_into(s_ref, mx_ref, slot, bqs, q_ref[qr, d:2 * d], k_ref[kr, d:2 * d])

    _flash(score, v_ref, n_sub, n_chunks, bkv, s_ref, mx_ref, m_ref, l_ref, acc_ref, block_cols=True)
    for t in range(n_sub):
        o = acc_ref[t] / l_ref[t]
        o = o[:bqs] - lam * o[bqs:]
        o_ref[t * bqs:(t + 1) * bqs, :] = (_rms(o, dn_ref[...]) * (1.0 - lam_init)).astype(BF16)


def _attention(kind, q, k, v, s_lat, out_rows, extra=(), ctx=False, into=None):
    s_all = q.shape[0]
    n_ctx = s_all - s_lat
    assert s_lat % n_ctx == 0
    if kind == "mla":
        heads, qw, kw, vw, ow, stack = MLA_HEADS, MLA_QK_PAD, MLA_QK_PAD, MLA_V, MLA_V, 1
        bqs = n_ctx if ctx else _tile(s_lat, (1024, 512, 256))
        body = _mla_attn_kernel
    elif kind == "gqa":
        g = GQA_HEADS // GQA_KV_HEADS
        heads, qw, kw, vw, ow, stack = GQA_KV_HEADS, g * LANE, LANE, LANE, g * LANE, g
        bqs = n_ctx if ctx else 256
        body = _gqa_attn_kernel
    else:
        heads, qw, kw, vw, ow, stack = DIFF_HEADS, 2 * LANE, 2 * LANE, DIFF_V_DIM, DIFF_V_DIM, 2
        bqs = n_ctx if ctx else _tile(s_lat, (512, 256))
        body = functools.partial(_diff_attn_kernel, lam_init=extra[2])
        extra = extra[:2]
    m_rows = stack * bqs
    n_sub = 1 if ctx else _tile(s_lat // bqs, (4, 3, 2, 1))
    bq = n_sub * bqs
    if ctx:
        kv_rows, bkv, row_blk, nq = n_ctx, n_ctx, s_lat // n_ctx, 1
    else:
        kv_rows, bkv, row_blk, nq = s_all, _tile(s_all, (1280, 1024, 512, 256)), 0, s_lat // bq
    body = functools.partial(body, n_sub=n_sub, n_chunks=kv_rows // bkv, bkv=bkv)
    kv_blk = row_blk
    in_specs = [pl.BlockSpec((bq, qw), lambda h, i: (row_blk + i, h)),
                pl.BlockSpec((kv_rows, kw), lambda h, i: (kv_blk, h), pipeline_mode=pl.Buffered(1)),
                pl.BlockSpec((kv_rows, vw), lambda h, i: (kv_blk, h), pipeline_mode=pl.Buffered(1))]
    in_specs += [pl.BlockSpec(e.shape, lambda h, i: (0, 0)) for e in extra]
    args = [q, k, v, *extra]
    aliases = {}
    if into is not None:
        in_specs.append(pl.BlockSpec(memory_space=pl.ANY))
        aliases = {len(args): 0}
        args.append(into)
    scratch = []
    if kind == "gqa":
        scratch.append(pltpu.VMEM((n_sub, m_rows, LANE), BF16))
    scratch += [pltpu.VMEM((2, m_rows, bkv), F32),
                pltpu.VMEM((2, m_rows, 1), F32), pltpu.VMEM((n_sub, m_rows, 1), F32),
                pltpu.VMEM((n_sub, m_rows, 1), F32), pltpu.VMEM((n_sub, m_rows, vw), F32)]
    return pl.pallas_call(
        body,
        grid=(heads, nq),
        in_specs=in_specs,
        out_specs=pl.BlockSpec((bq, ow), lambda h, i: (row_blk + i, h)),
        out_shape=jax.ShapeDtypeStruct((out_rows, heads * ow), BF16),
        scratch_shapes=scratch,
        input_output_aliases=aliases,
        compiler_params=_params(("parallel", "arbitrary")),
        name=kind + ("_attn_ctx" if ctx else "_attn"),
    )(*args)


def _layer_norm(z, g, b):
    mu = jnp.mean(z, axis=-1, keepdims=True)
    zc = z - mu
    var = jnp.mean(zc * zc, axis=-1, keepdims=True)
    return zc * lax.rsqrt(var + LN_EPS) * g + b


def _merge_kernel(om_ref, og_ref, od_ref, g_ref, x_ref, mod_ref, wbm_ref, wbg_ref, wbd_ref, wo_ref,
                  lng_ref, lnb_ref, xo_ref, h2_ref, *, tm, s_lat):
    d = D_MODEL
    row0 = pl.program_id(0) * tm
    merged = g_ref[:, 0:d].astype(F32) * jnp.dot(om_ref[...], wbm_ref[...], preferred_element_type=F32)
    merged += g_ref[:, d:2 * d].astype(F32) * jnp.dot(og_ref[...], wbg_ref[...], preferred_element_type=F32)
    merged += g_ref[:, 2 * d:3 * d].astype(F32) * jnp.dot(od_ref[...], wbd_ref[...], preferred_element_type=F32)
    y = jnp.dot(merged.astype(BF16), wo_ref[...], preferred_element_type=F32)
    z = DEEPNORM_ALPHA * x_ref[...] + _row_mod(mod_ref, 2, row0, tm, s_lat) * y
    xn = _layer_norm(z, lng_ref[...], lnb_ref[...])
    xo_ref[...] = xn
    h2 = xn * (1.0 + _row_mod(mod_ref, 4, row0, tm, s_lat)) + _row_mod(mod_ref, 3, row0, tm, s_lat)
    h2_ref[...] = _pack_pairs(h2)


def _merge(om, og, od, gates, xs, mod, wbm, wbg, wbd, wo, lng, lnb, rows, s_lat):
    d = D_MODEL
    tm = 256
    row = lambda i: (i, 0)
    full = lambda i: (0, 0)
    resident = lambda a: pl.BlockSpec(a.shape, full, pipeline_mode=pl.Buffered(1))
    return pl.pallas_call(
        functools.partial(_merge_kernel, tm=tm, s_lat=s_lat),
        grid=(rows // tm,),
        in_specs=[pl.BlockSpec((tm, om.shape[1]), row), pl.BlockSpec((tm, og.shape[1]), row),
                  pl.BlockSpec((tm, od.shape[1]), row), pl.BlockSpec((tm, gates.shape[1]), row),
                  pl.BlockSpec((tm, d), row), pl.BlockSpec(mod.shape, full),
                  resident(wbm), resident(wbg), resident(wbd), resident(wo),
                  pl.BlockSpec(lng.shape, full), pl.BlockSpec(lnb.shape, full)],
        out_specs=[pl.BlockSpec((tm, d), row), pl.BlockSpec((tm, d // 2), row)],
        out_shape=[jax.ShapeDtypeStruct((rows, d), F32), jax.ShapeDtypeStruct((rows, d // 2), jnp.uint32)],
        compiler_params=_params(("parallel",)),
        name="merge",
    )(om, og, od, gates, xs, mod, wbm, wbg, wbd, wo, lng, lnb)


def _router_kernel(x_ref, mod_ref, wr_ref, br_ref, ri_ref, rw_ref, cnt_ref, run_ref, *, tm, s_lat):
    @pl.when(pl.program_id(0) == 0)
    def _():
        run_ref[...] = jnp.zeros(run_ref.shape, F32)

    row0 = pl.program_id(0) * tm
    h2 = x_ref[...] * (1.0 + _row_mod(mod_ref, 4, row0, tm, s_lat)) + _row_mod(mod_ref, 3, row0, tm, s_lat)
    logits = lax.dot_general(wr_ref[...], h2, (((1,), (1,)), ((), ())),
                             precision=HIGHEST, preferred_element_type=F32)
    scores = jax.nn.sigmoid(logits)
    biased = scores + br_ref[...]
    sc = [scores[e:e + 1, :] for e in range(N_EXPERTS)]
    bi = [biased[e:e + 1, :] for e in range(N_EXPERTS)]
    gs = []
    for g in range(N_GROUPS):
        a, b, c, dd = bi[4 * g:4 * g + 4]
        hi1, lo1, hi2, lo2 = jnp.maximum(a, b), jnp.minimum(a, b), jnp.maximum(c, dd), jnp.minimum(c, dd)
        gs.append(jnp.maximum(hi1, hi2) + jnp.maximum(jnp.minimum(hi1, hi2), jnp.maximum(lo1, lo2)))
    best = jnp.maximum(jnp.maximum(gs[0], gs[1]), jnp.maximum(gs[2], gs[3]))
    gsel = jnp.where(gs[0] == best, 0, jnp.where(gs[1] == best, 1, jnp.where(gs[2] == best, 2, 3)))
    pick = lambda vals, j: jnp.where(gsel == 0, vals[j], jnp.where(gsel == 1, vals[4 + j],
                                     jnp.where(gsel == 2, vals[8 + j], vals[12 + j])))
    xb = [pick(bi, j) for j in range(EXPERTS_PER_GROUP)]
    xs = [pick(sc, j) for j in range(EXPERTS_PER_GROUP)]
    sel, w = [], []
    for i in range(EXPERTS_PER_GROUP):
        rank = jnp.zeros_like(gsel)
        for j in range(EXPERTS_PER_GROUP):
            if j == i:
                continue
            beats = (xb[j] >= xb[i]) if j < i else (xb[j] > xb[i])
            rank = rank + beats.astype(jnp.int32)
        sel.append(rank < 2)
        w.append(jnp.where(rank < 2, xs[i], 0.0))
    inv = ROUTED_SCALE / (w[0] + w[1] + w[2] + w[3])
    ia = jnp.where(sel[0], 0, jnp.where(sel[1], 1, jnp.where(sel[2], 2, 3)))
    ib = jnp.where(sel[3], 3, jnp.where(sel[2], 2, jnp.where(sel[1], 1, 0)))
    local = lambda idx: jnp.where(idx == 0, w[0], jnp.where(idx == 1, w[1], jnp.where(idx == 2, w[2], w[3])))
    ea = gsel * EXPERTS_PER_GROUP + ia
    eb = gsel * EXPERTS_PER_GROUP + ib
    eidx = lax.broadcasted_iota(jnp.int32, (N_EXPERTS, tm), 0)
    is_a, is_b = eidx == ea, eidx == eb
    onehot = jnp.where(is_a | is_b, 1.0, 0.0)
    earlier = jnp.where(lax.broadcasted_iota(jnp.int32, (tm, tm), 0) < lax.broadcasted_iota(jnp.int32, (tm, tm), 1),
                        1.0, 0.0).astype(BF16)
    pos = run_ref[...] + jnp.dot(onehot.astype(BF16), earlier, preferred_element_type=F32)
    ri_ref[0:1, :] = ea
    ri_ref[1:2, :] = eb
    ri_ref[2:3, :] = jnp.sum(jnp.where(is_a, pos, 0.0), axis=0, keepdims=True).astype(jnp.int32)
    ri_ref[3:4, :] = jnp.sum(jnp.where(is_b, pos, 0.0), axis=0, keepdims=True).astype(jnp.int32)
    rw_ref[0:1, :] = local(ia) * inv
    rw_ref[1:2, :] = local(ib) * inv
    run = run_ref[...] + jnp.sum(onehot, axis=1, keepdims=True)
    run_ref[...] = run
    cnt_ref[...] = jnp.broadcast_to(run, cnt_ref.shape)


def _router(xs, mod, wr_t, br, rows, s_lat):
    d = D_MODEL
    tm = _tile(rows, (640, 512, 256))
    return pl.pallas_call(
        functools.partial(_router_kernel, tm=tm, s_lat=s_lat),
        grid=(rows // tm,),
        in_specs=[pl.BlockSpec((tm, d), lambda i: (i, 0)), pl.BlockSpec(mod.shape, lambda i: (0, 0)),
                  pl.BlockSpec(wr_t.shape, lambda i: (0, 0)), pl.BlockSpec(br.shape, lambda i: (0, 0))],
        out_specs=[pl.BlockSpec((4, tm), lambda i: (0, i)), pl.BlockSpec((2, tm), lambda i: (0, i)),
                   pl.BlockSpec((N_EXPERTS, LANE), lambda i: (0, 0))],
        out_shape=[jax.ShapeDtypeStruct((4, rows), jnp.int32), jax.ShapeDtypeStruct((2, rows), F32),
                   jax.ShapeDtypeStruct((N_EXPERTS, LANE), F32)],
        scratch_shapes=[pltpu.VMEM((N_EXPERTS, 1), F32)],
        compiler_params=_params(("arbitrary",)),
        name="router",
    )(xs, mod, wr_t, br)


MOE_TILE = 512
HALF = D_MODEL // 2
DMA_UNROLL = 8


def _pack_pairs(h):
    hi = pltpu.bitcast(h[:, :HALF].astype(BF16).astype(F32), jnp.uint32)
    lo = pltpu.bitcast(h[:, HALF:].astype(BF16).astype(F32), jnp.uint32)
    return (hi & jnp.uint32(0xFFFF0000)) | (lo >> jnp.uint32(16))


def _unpack_pairs(w):
    hi = pltpu.bitcast(w & jnp.uint32(0xFFFF0000), F32)
    lo = pltpu.bitcast(w << jnp.uint32(16), F32)
    return jnp.concatenate([hi, lo], axis=1).astype(BF16)


def _swiglu(x, wg, wu, wd):
    a = jnp.dot(x, wg, preferred_element_type=F32)
    b = jnp.dot(x, wu, preferred_element_type=F32)
    return jnp.dot((a * jax.nn.sigmoid(a) * b).astype(BF16), wd, preferred_element_type=F32)


def _moe_sort_kernel(dest_ref, h_ref, init_ref, o_ref, sem, *, tg):
    del init_ref
    base = pl.program_id(0) * tg

    def copy(r, j):
        return pltpu.make_async_copy(h_ref.at[pl.ds(r, 1)], o_ref.at[pl.ds(dest_ref[j, base + r], 1)], sem)

    def start(r, carry):
        for j in range(2):
            copy(r, j).start()
        return carry

    def wait(r, carry):
        for j in range(2):
            copy(r, j).wait()
        return carry

    lax.fori_loop(0, tg, start, 0, unroll=DMA_UNROLL)
    lax.fori_loop(0, tg, wait, 0, unroll=DMA_UNROLL)


def _moe_sort(dest, h2p, n_sorted):
    rows, half = h2p.shape
    tg = _tile(rows, (1280, 1024, 512, 256))
    return pl.pallas_call(
        functools.partial(_moe_sort_kernel, tg=tg),
        grid_spec=pltpu.PrefetchScalarGridSpec(
            num_scalar_prefetch=1, grid=(rows // tg,),
            in_specs=[pl.BlockSpec((tg, half), lambda i, dref: (i, 0)), pl.BlockSpec(memory_space=pl.ANY)],
            out_specs=pl.BlockSpec(memory_space=pl.ANY),
            scratch_shapes=[pltpu.SemaphoreType.DMA]),
        out_shape=jax.ShapeDtypeStruct((n_sorted, half), jnp.uint32),
        input_output_aliases={2: 0},
        compiler_params=_params(("arbitrary",)),
        name="moe_sort",
    )(dest, h2p, jnp.zeros((n_sorted, half), jnp.uint32))


def _moe_expert_kernel(te_ref, na_ref, x_ref, wg_ref, wu_ref, wd_ref, y_ref):
    del te_ref
    live = pl.program_id(0) < na_ref[0]

    @pl.when(live)
    def _():
        y_ref[...] = _swiglu(_unpack_pairs(x_ref[...]), wg_ref[0, 0].astype(BF16), wu_ref[0, 0].astype(BF16),
                             wd_ref[0, 0].astype(BF16))

    @pl.when(jnp.logical_not(live))
    def _():
        y_ref[...] = jnp.zeros(y_ref.shape, F32)


def _moe_experts(tile_expert, n_active, xsorted, wg, wu, wd, layer):
    n_sorted, half = xsorted.shape
    _, _, d, f = wg.shape
    tme = MOE_TILE
    wsel = lambda i, te, na: (layer, te[i], 0, 0)
    return pl.pallas_call(
        _moe_expert_kernel,
        grid_spec=pltpu.PrefetchScalarGridSpec(
            num_scalar_prefetch=2, grid=(n_sorted // tme,),
            in_specs=[pl.BlockSpec((tme, half), lambda i, te, na: (i, 0)),
                      pl.BlockSpec((1, 1, d, f), wsel), pl.BlockSpec((1, 1, d, f), wsel),
                      pl.BlockSpec((1, 1, f, d), wsel)],
            out_specs=pl.BlockSpec((tme, d), lambda i, te, na: (i, 0))),
        out_shape=jax.ShapeDtypeStruct((n_sorted, d), F32),
        compiler_params=_params(("arbitrary",)),
        name="moe_experts",
    )(tile_expert, n_active, xsorted, wg, wu, wd)


def _moe_final_kernel(dest_ref, hp_ref, rw_ref, y_ref, sg_ref, su_ref, sd_ref, x_ref, mod_ref, lng_ref, lnb_ref,
                      o_ref, ya_ref, yb_ref, sem, *, tm, s_lat):
    base = pl.program_id(0) * tm
    bufs = (ya_ref, yb_ref)

    def copy(r, j):
        return pltpu.make_async_copy(y_ref.at[pl.ds(dest_ref[j, base + r], 1)], bufs[j].at[pl.ds(r, 1)], sem.at[j])

    def start(r, carry):
        for j in range(2):
            copy(r, j).start()
        return carry

    def wait(r, carry):
        for j in range(2):
            copy(r, j).wait()
        return carry

    lax.fori_loop(0, tm, start, 0, unroll=DMA_UNROLL)
    shared = _swiglu(_unpack_pairs(hp_ref[...]), sg_ref[...], su_ref[...], sd_ref[...])
    lax.fori_loop(0, tm, wait, 0, unroll=DMA_UNROLL)
    y = shared + rw_ref[:, 0:1] * ya_ref[...] + rw_ref[:, 1:2] * yb_ref[...]
    z = DEEPNORM_ALPHA * x_ref[...] + _row_mod(mod_ref, 5, base, tm, s_lat) * y
    o_ref[...] = _layer_norm(z, lng_ref[...], lnb_ref[...])


def _moe_final(dest, h2p, rw, ysorted, sg, su, sd, xs, mod, lng, lnb, rows, s_lat):
    d = D_MODEL
    tm = 256
    row = lambda i, dref: (i, 0)
    full = lambda i, dref: (0, 0)
    return pl.pallas_call(
        functools.partial(_moe_final_kernel, tm=tm, s_lat=s_lat),
        grid_spec=pltpu.PrefetchScalarGridSpec(
            num_scalar_prefetch=1, grid=(rows // tm,),
            in_specs=[pl.BlockSpec((tm, HALF), row), pl.BlockSpec((tm, 2), row), pl.BlockSpec(memory_space=pl.ANY),
                      pl.BlockSpec(sg.shape, full), pl.BlockSpec(su.shape, full), pl.BlockSpec(sd.shape, full),
                      pl.BlockSpec((tm, d), row), pl.BlockSpec(mod.shape, full),
                      pl.BlockSpec(lng.shape, full), pl.BlockSpec(lnb.shape, full)],
            out_specs=pl.BlockSpec((tm, d), row),
            scratch_shapes=[pltpu.VMEM((tm, d), F32), pltpu.VMEM((tm, d), F32), pltpu.SemaphoreType.DMA((2,))]),
        out_shape=jax.ShapeDtypeStruct((rows, d), F32),
        compiler_params=_params(("arbitrary",)),
        name="moe_final",
    )(dest, h2p, rw, ysorted, sg, su, sd, xs, mod, lng, lnb)


def _moe_plan(ri, cnt, rows):
    tme = MOE_TILE
    n_sorted = 2 * rows + N_EXPERTS * tme
    counts = cnt[:, 0].astype(jnp.int32)
    padded = (counts + tme - 1) // tme * tme
    ends = jnp.cumsum(padded)
    offs = ends - padded
    eidx = jnp.arange(N_EXPERTS, dtype=jnp.int32)[:, None]
    seg = lambda e: jnp.sum(jnp.where(eidx == e[None, :], offs[:, None], 0), axis=0)
    dest = jnp.stack([seg(ri[0]) + ri[2], seg(ri[1]) + ri[3]])
    tile_start = jnp.arange(n_sorted // tme, dtype=jnp.int32) * tme
    tile_expert = jnp.minimum(jnp.sum((ends[None, :] <= tile_start[:, None]).astype(jnp.int32), axis=1),
                              N_EXPERTS - 1)
    n_active = (ends[-1:] // tme).astype(jnp.int32)
    return dest, tile_expert, n_active, n_sorted


def _moe(xs, h2p, mod, wr_t, br, wg, wu, wd, layer, sg, su, sd, lng, lnb, rows, s_lat):
    ri, rw, cnt = _router(xs, mod, wr_t, br, rows, s_lat)
    dest, tile_expert, n_active, n_sorted = _moe_plan(ri, cnt, rows)
    xsorted = _moe_sort(dest, h2p, n_sorted)
    ysorted = _moe_experts(tile_expert, n_active, xsorted, wg, wu, wd, layer)
    return _moe_final(dest, h2p, rw.T, ysorted, sg, su, sd, xs, mod, lng, lnb, rows, s_lat)


def _rope_tables(s_lat, n_ctx, dim):
    quarter = dim // 4
    rows = s_lat // GRID_W
    lane = jnp.arange(LANE)
    part = lane // quarter
    inv_freq = ROPE_THETA ** (-(lane % quarter).astype(F32) / quarter)
    ang_r = jnp.arange(rows, dtype=F32)[:, None] * inv_freq
    ang_c = jnp.arange(GRID_W, dtype=F32)[:, None] * inv_freq
    spread = lambda by_row, by_col: (by_row[:, None, :] + by_col[None, :, :]).reshape(s_lat, LANE)
    on = lambda lanes, t: jnp.where(lanes, t, 0.0)
    cos = spread(on(part < 2, jnp.cos(ang_r)), on((part == 2) | (part == 3), jnp.cos(ang_c)) + on(part > 3, 1.0))
    sa = spread(on(part == 0, -jnp.sin(ang_r)), on(part == 2, -jnp.sin(ang_c)))
    sb = spread(on(part == 1, jnp.sin(ang_r)), on(part == 3, jnp.sin(ang_c)))
    ctx1 = jnp.ones((n_ctx, LANE), F32)
    ctx0 = jnp.zeros((n_ctx, LANE), F32)
    return (jnp.concatenate([cos, ctx1], axis=0), jnp.concatenate([sa, ctx0], axis=0),
            jnp.concatenate([sb, ctx0], axis=0))


def kernel(x, c, ctx, c_ctx, w_ada, b_ada, w_in, mla_q_norm, mla_w_uq, mla_kv_norm, mla_w_ukv, gqa_q_norm,
           gqa_k_norm, diff_lambda, diff_norm, w_br_mla, w_br_gqa, w_br_diff, w_o, ln1_g, ln1_b, w_router,
           b_router, moe_w_gate, moe_w_up, moe_w_down, shared_w_gate, shared_w_up, shared_w_down, ln2_g, ln2_b):
    B, s_lat, d = x.shape
    n_ctx = ctx.shape[1]
    assert B == 1 and d == D_MODEL
    s_all = s_lat + n_ctx

    xs = jnp.concatenate([x[0], ctx[0]], axis=0)
    cc = jnp.zeros((8, d), F32).at[0].set(c[0]).at[1].set(c_ctx)
    mod_all = _adaln(cc, w_ada, b_ada)

    tabs64 = _rope_tables(s_lat, n_ctx, MLA_ROPE)
    tabs128 = _rope_tables(s_lat, n_ctx, GQA_HEAD_DIM)
    wr_t = w_router.T
    br = b_router.reshape(N_EXPERTS, 1)
    row2 = lambda v: v.reshape(1, -1)

    for l in range(DEPTH):
        last = l == DEPTH - 1
        mod = mod_all[l]
        w = _w_in_layout(w_in[l])
        wuq =jnp.pad(mla_w_uq[l].reshape(MLA_Q_LORA, MLA_HEADS, MLA_NOPE + MLA_ROPE),
                      ((0, 0), (0, 0), (0, MLA_QK_PAD - MLA_NOPE - MLA_ROPE))
                      ).reshape(MLA_Q_LORA, MLA_HEADS * MLA_QK_PAD).astype(BF16)
        ukv = mla_w_ukv[l].reshape(MLA_KV_LORA, MLA_HEADS, MLA_NOPE + MLA_V)
        wukv = jnp.concatenate([ukv[:, :, :MLA_NOPE].reshape(MLA_KV_LORA, -1),
                                ukv[:, :, MLA_NOPE:].reshape(MLA_KV_LORA, -1)], axis=1).astype(BF16)

        h, q_mla, k_mla, v_mla = _mla_proj(xs, mod, w, row2(mla_q_norm[l]), row2(mla_kv_norm[l]),
                                           wuq, wukv, tabs64, s_lat)
        q_gqa, k_gqa, v_gqa = _gqa_proj(h, w, row2(gqa_q_norm[l]), row2(gqa_k_norm[l]), tabs128)
        q_dif, k_dif, v_dif = _diff_proj(h, w, tabs128)
        gates = _gate_proj(h, w)

        lam_init = 0.8 - 0.6 * math.exp(-0.3 * l)
        rows = s_lat if last else s_all
        dif_extra = (diff_lambda[l], row2(diff_norm[l]), lam_init)
        fresh = lambda width: None if last else jnp.zeros((rows, width), BF16)
        o_mla = _attention("mla", q_mla, k_mla, v_mla, s_lat, rows, into=fresh(MLA_HEADS * MLA_V))
        o_gqa = _attention("gqa", q_gqa, k_gqa, v_gqa, s_lat, rows, into=fresh(GQA_HEADS * GQA_HEAD_DIM))
        o_dif = _attention("diff", q_dif, k_dif, v_dif, s_lat, rows, extra=dif_extra,
                           into=fresh(DIFF_HEADS * DIFF_V_DIM))
        if not last:
            o_mla = _attention("mla", q_mla, k_mla, v_mla, s_lat, rows, ctx=True, into=o_mla)
            o_gqa = _attention("gqa", q_gqa, k_gqa, v_gqa, s_lat, rows, ctx=True, into=o_gqa)
            o_dif = _attention("diff", q_dif, k_dif, v_dif, s_lat, rows, extra=dif_extra, ctx=True, into=o_dif)

        xs, h2p = _merge(o_mla, o_gqa, o_dif, gates, xs, mod, w_br_mla[l].astype(BF16), w_br_gqa[l].astype(BF16),
                         w_br_diff[l].astype(BF16), w_o[l].astype(BF16), row2(ln1_g[l]), row2(ln1_b[l]), rows, s_lat)

        xs = _moe(xs, h2p, mod, wr_t, br, moe_w_gate, moe_w_up, moe_w_down, l,
                  shared_w_gate[l].astype(BF16), shared_w_up[l].astype(BF16), shared_w_down[l].astype(BF16),
                  row2(ln2_g[l]), row2(ln2_b[l]), rows, s_lat)

    return xs[None]
```

```python
import functools
import math

import jax
import jax.numpy as jnp
from jax import lax
from jax.experimental import pallas as pl
from jax.experimental.pallas import tpu as pltpu

F32 = jnp.float32
BF16 = jnp.bfloat16
HIGHEST = lax.Precision.HIGHEST

D_MODEL = 2048
DEPTH = 2
GRID_W = 64
ROPE_THETA = 10000.0
LN_EPS = 1e-5
RMS_EPS = 1e-6
MLA_HEADS = 8
MLA_Q_LORA = 512
MLA_KV_LORA = 512
MLA_NOPE = 128
MLA_ROPE = 64
MLA_V = 128
GQA_HEADS = 8
GQA_KV_HEADS = 2
GQA_HEAD_DIM = 128
DIFF_HEADS = 4
DIFF_HEAD_DIM = 128
DIFF_V_DIM = 2 * DIFF_HEAD_DIM
N_EXPERTS = 16
N_GROUPS = 4
EXPERTS_PER_GROUP = N_EXPERTS // N_GROUPS
MOE_D_FF = 512
ROUTED_SCALE = 1.0
DEEPNORM_ALPHA = (2 * DEPTH) ** 0.25

LANE = 128
MLA_QK_PAD = 2 * LANE
LOG2E = math.log2(math.e)
VMEM_LIMIT = 56 * 1024 * 1024


def _tile(rows, prefs):
    for t in prefs:
        if rows % t == 0:
            return t
    raise ValueError(f"no tile in {prefs} divides {rows}")


def _params(sem, vmem=VMEM_LIMIT, flags=None):
    return pltpu.CompilerParams(dimension_semantics=sem, vmem_limit_bytes=vmem, flags=flags)


def _row_mod(mod_ref, k, row0, tm, s_lat):
    d = D_MODEL
    lat = mod_ref[0:1, k * d:(k + 1) * d]
    ctx = mod_ref[1:2, k * d:(k + 1) * d]
    rows = row0 + lax.broadcasted_iota(jnp.int32, (tm, 1), 0)
    return jnp.where(rows >= s_lat, ctx, lat)


def _rms(x, g):
    return x * lax.rsqrt(jnp.mean(x * x, axis=-1, keepdims=True) + RMS_EPS) * g


def _rope(t, cos, sa, sb, quarter):
    return t * cos + pltpu.roll(t, LANE - quarter, 1) * sa + pltpu.roll(t, quarter, 1) * sb


def _adaln_kernel(c_ref, w_ref, b_ref, o_ref):
    a = c_ref[...]
    a = a * jax.nn.sigmoid(a)
    o_ref[0] = jnp.dot(a, w_ref[0], precision=HIGHEST, preferred_element_type=F32) + b_ref[0]


def _adaln(cc, w_ada, b_ada):
    L, d, n = w_ada.shape
    tn = 1024
    return pl.pallas_call(
        _adaln_kernel,
        grid=(L, n // tn),
        in_specs=[pl.BlockSpec((8, d), lambda l, j: (0, 0)),
                  pl.BlockSpec((1, d, tn), lambda l, j: (l, 0, j)),
                  pl.BlockSpec((1, 1, tn), lambda l, j: (l, 0, j))],
        out_specs=pl.BlockSpec((1, 8, tn), lambda l, j: (l, 0, j)),
        out_shape=jax.ShapeDtypeStruct((L, 8, n), F32),
        compiler_params=_params(("arbitrary", "arbitrary")),
        name="adaln",
    )(cc, w_ada, b_ada.reshape(L, 1, n))


def _mla_proj_kernel(x_ref, mod_ref, wa_ref, qn_ref, kvn_ref, wuq_ref, wukv_ref, cos_ref, sa_ref, sb_ref,
                     h_ref, q_ref, k_ref, v_ref, *, tm, s_lat, q_scale):
    row0 = pl.program_id(0) * tm
    sh = _row_mod(mod_ref, 0, row0, tm, s_lat)
    sc = _row_mod(mod_ref, 1, row0, tm, s_lat)
    h = (x_ref[...] * (1.0 + sc) + sh).astype(BF16)
    h_ref[...] = h
    a = jnp.dot(h, wa_ref[...], preferred_element_type=F32)
    cqn = _rms(a[:, :MLA_Q_LORA], qn_ref[...]).astype(BF16)
    ckvn = _rms(a[:, MLA_Q_LORA:MLA_Q_LORA + MLA_KV_LORA], kvn_ref[...]).astype(BF16)
    kr = a[:, MLA_Q_LORA + MLA_KV_LORA:]
    q = jnp.dot(cqn, wuq_ref[...], preferred_element_type=F32)
    kv = jnp.dot(ckvn, wukv_ref[...], preferred_element_type=F32)
    cos, sa, sb = cos_ref[...], sa_ref[...], sb_ref[...]
    quarter = MLA_ROPE // 4
    kr2 = _rope(kr, cos, sa, sb, quarter).astype(BF16)
    for hh in range(MLA_HEADS):
        c0 = hh * MLA_QK_PAD
        q_ref[:, c0:c0 + LANE] = (q[:, c0:c0 + LANE] * q_scale).astype(BF16)
        q_ref[:, c0 + LANE:c0 + 2 * LANE] = (_rope(q[:, c0 + LANE:c0 + 2 * LANE], cos, sa, sb, quarter)
                                             * q_scale).astype(BF16)
        k_ref[:, c0:c0 + LANE] = kv[:, hh * LANE:(hh + 1) * LANE].astype(BF16)
        k_ref[:, c0 + LANE:c0 + 2 * LANE] = kr2
    v_ref[...] = kv[:, MLA_HEADS * MLA_NOPE:].astype(BF16)


def _mla_proj(xs, mod, wa, qn, kvn, wuq, wukv, tabs, s_lat):
    rows, d = xs.shape
    tm = _tile(rows, (640, 512, 256))
    cos, sa, sb = tabs
    row = lambda i: (i, 0)
    full = lambda i: (0, 0)
    qk_w = MLA_HEADS * MLA_QK_PAD
    v_w = MLA_HEADS * MLA_V
    q_scale = (MLA_NOPE + MLA_ROPE) ** -0.5 * LOG2E
    return pl.pallas_call(
        functools.partial(_mla_proj_kernel, tm=tm, s_lat=s_lat, q_scale=q_scale),
        grid=(rows // tm,),
        in_specs=[pl.BlockSpec((tm, d), row), pl.BlockSpec(mod.shape, full),
                  pl.BlockSpec(wa.shape, full), pl.BlockSpec(qn.shape, full), pl.BlockSpec(kvn.shape, full),
                  pl.BlockSpec(wuq.shape, full), pl.BlockSpec(wukv.shape, full),
                  pl.BlockSpec((tm, LANE), row), pl.BlockSpec((tm, LANE), row), pl.BlockSpec((tm, LANE), row)],
        out_specs=[pl.BlockSpec((tm, d), row), pl.BlockSpec((tm, qk_w), row),
                   pl.BlockSpec((tm, qk_w), row), pl.BlockSpec((tm, v_w), row)],
        out_shape=[jax.ShapeDtypeStruct((rows, d), BF16), jax.ShapeDtypeStruct((rows, qk_w), BF16),
                   jax.ShapeDtypeStruct((rows, qk_w), BF16), jax.ShapeDtypeStruct((rows, v_w), BF16)],
        compiler_params=_params(("parallel",)),
        name="mla_proj",
    )(xs, mod, wa, qn, kvn, wuq, wukv, cos, sa, sb)


def _gqa_proj_kernel(h_ref, w_ref, qn_ref, kn_ref, cos_ref, sa_ref, sb_ref, q_ref, k_ref, v_ref, *, q_scale):
    a = jnp.dot(h_ref[...], w_ref[...], preferred_element_type=F32)
    cos, sa, sb = cos_ref[...], sa_ref[...], sb_ref[...]
    quarter = GQA_HEAD_DIM // 4
    for hh in range(GQA_HEADS):
        x = _rms(a[:, hh * LANE:(hh + 1) * LANE], qn_ref[...])
        q_ref[:, hh * LANE:(hh + 1) * LANE] = (_rope(x, cos, sa, sb, quarter) * q_scale).astype(BF16)
    k0 = GQA_HEADS * GQA_HEAD_DIM
    for hh in range(GQA_KV_HEADS):
        x = _rms(a[:, k0 + hh * LANE:k0 + (hh + 1) * LANE], kn_ref[...])
        k_ref[:, hh * LANE:(hh + 1) * LANE] = _rope(x, cos, sa, sb, quarter).astype(BF16)
    v_ref[...] = a[:, k0 + GQA_KV_HEADS * GQA_HEAD_DIM:].astype(BF16)


def _gqa_proj(h, w, qn, kn, tabs):
    rows, d = h.shape
    tm = _tile(rows, (640, 512, 256))
    cos, sa, sb = tabs
    row = lambda i: (i, 0)
    full = lambda i: (0, 0)
    qw = GQA_HEADS * GQA_HEAD_DIM
    kw = GQA_KV_HEADS * GQA_HEAD_DIM
    return pl.pallas_call(
        functools.partial(_gqa_proj_kernel, q_scale=GQA_HEAD_DIM ** -0.5 * LOG2E),
        grid=(rows // tm,),
        in_specs=[pl.BlockSpec((tm, d), row), pl.BlockSpec(w.shape, full),
                  pl.BlockSpec(qn.shape, full), pl.BlockSpec(kn.shape, full),
                  pl.BlockSpec((tm, LANE), row), pl.BlockSpec((tm, LANE), row), pl.BlockSpec((tm, LANE), row)],
        out_specs=[pl.BlockSpec((tm, qw), row), pl.BlockSpec((tm, kw), row), pl.BlockSpec((tm, kw), row)],
        out_shape=[jax.ShapeDtypeStruct((rows, qw), BF16), jax.ShapeDtypeStruct((rows, kw), BF16),
                   jax.ShapeDtypeStruct((rows, kw), BF16)],
        compiler_params=_params(("parallel",)),
        name="gqa_proj",
    )(h, w, qn, kn, cos, sa, sb)


def _diff_proj_kernel(h_ref, w_ref, cos_ref, sa_ref, sb_ref, q_ref, k_ref, v_ref, *, q_scale):
    a = jnp.dot(h_ref[...], w_ref[...], preferred_element_type=F32)
    cos, sa, sb = cos_ref[...], sa_ref[...], sb_ref[...]
    quarter = DIFF_HEAD_DIM // 4
    n = 2 * DIFF_HEADS
    for hh in range(n):
        q_ref[:, hh * LANE:(hh + 1) * LANE] = (_rope(a[:, hh * LANE:(hh + 1) * LANE], cos, sa, sb, quarter)
                                               * q_scale).astype(BF16)
        k_ref[:, hh * LANE:(hh + 1) * LANE] = _rope(a[:, (n + hh) * LANE:(n + hh + 1) * LANE],
                                                    cos, sa, sb, quarter).astype(BF16)
    v_ref[...] = a[:, 2 * n * LANE:].astype(BF16)


def _diff_proj(h, w, tabs):
    rows, d = h.shape
    tm = _tile(rows, (640, 512, 256))
    cos, sa, sb = tabs
    row = lambda i: (i, 0)
    full = lambda i: (0, 0)
    ww = 2 * DIFF_HEADS * DIFF_HEAD_DIM
    vw = DIFF_HEADS * DIFF_V_DIM
    return pl.pallas_call(
        functools.partial(_diff_proj_kernel, q_scale=DIFF_HEAD_DIM ** -0.5 * LOG2E),
        grid=(rows // tm,),
        in_specs=[pl.BlockSpec((tm, d), row), pl.BlockSpec(w.shape, full),
                  pl.BlockSpec((tm, LANE), row), pl.BlockSpec((tm, LANE), row), pl.BlockSpec((tm, LANE), row)],
        out_specs=[pl.BlockSpec((tm, ww), row), pl.BlockSpec((tm, ww), row), pl.BlockSpec((tm, vw), row)],
        out_shape=[jax.ShapeDtypeStruct((rows, ww), BF16), jax.ShapeDtypeStruct((rows, ww), BF16),
                   jax.ShapeDtypeStruct((rows, vw), BF16)],
        compiler_params=_params(("parallel",)),
        name="diff_proj",
    )(h, w, cos, sa, sb)


def _gate_proj_kernel(h_ref, w_ref, o_ref):
    a = jnp.dot(h_ref[...], w_ref[...], preferred_element_type=F32)
    o_ref[...] = jax.nn.sigmoid(a).astype(BF16)


def _gate_proj(h, w):
    rows, d = h.shape
    n = w.shape[1]
    tm = _tile(rows, (640, 512, 256))
    tn = 1536
    return pl.pallas_call(
        _gate_proj_kernel,
        grid=(rows // tm, n // tn),
        in_specs=[pl.BlockSpec((tm, d), lambda i, j: (i, 0)), pl.BlockSpec((d, tn), lambda i, j: (0, j))],
        out_specs=pl.BlockSpec((tm, tn), lambda i, j: (i, j)),
        out_shape=jax.ShapeDtypeStruct((rows, n), BF16),
        compiler_params=_params(("parallel", "arbitrary")),
        name="gate_proj",
    )(h, w)


ITEMS_PER_TRIP = 2
MXU_DEPTH = 256


def _rows(start, size):
    return pl.ds(start if isinstance(start, int) else pl.multiple_of(start, size), size)


def _dot_nt(a, b):
    return lax.dot_general(a, b, (((1,), (1,)), ((), ())), preferred_element_type=F32)


def _flash(score, v_ref, n_sub, n_chunks, bkv, s_ref, mx_ref, m_ref, l_ref, acc_ref, block_cols=False):
    m_ref[...] = jnp.full(m_ref.shape, -jnp.inf, F32)
    l_ref[...] = jnp.zeros(l_ref.shape, F32)
    acc_ref[...] = jnp.zeros(acc_ref.shape, F32)

    def item(k):
        if isinstance(k, int):
            return k // n_chunks, k % n_chunks
        t = lax.div(k, n_chunks)
        return t, k - t * n_chunks

    def issue(k, slot):
        t, c = item(k)
        score(t, c, slot)

    def absorb(k, slot):
        t, c = item(k)
        m_prev = m_ref[t]
        m_new = jnp.maximum(m_prev, mx_ref[slot])
        alpha = jnp.exp2(m_prev - m_new)
        kt = MXU_DEPTH if (block_cols and bkv % MXU_DEPTH == 0) else bkv
        start = c * bkv
        lsum = jnp.zeros(m_prev.shape, F32)
        acc = alpha * acc_ref[t]
        for j in range(bkv // kt):
            p = jnp.exp2(s_ref[slot, :, j * kt:(j + 1) * kt] - m_new)
            lsum = lsum + jnp.sum(p, axis=1, keepdims=True)
            acc = acc + jnp.dot(p.astype(BF16), v_ref[_rows(start + j * kt, kt), :], preferred_element_type=F32)
        l_ref[t] = alpha * l_ref[t] + lsum
        acc_ref[t] = acc
        m_ref[t] = m_new

    n_items = n_sub * n_chunks
    issue(0, 0)

    def body(j, carry):
        k = ITEMS_PER_TRIP * j
        for i in range(ITEMS_PER_TRIP):
            issue(k + i + 1, (i + 1) % 2)
            absorb(k + i, i % 2)
        return carry

    n_trips = (n_items - 1) // ITEMS_PER_TRIP
    if n_trips:
        lax.fori_loop(0, n_trips, body, 0)
    for k in range(ITEMS_PER_TRIP * n_trips, n_items):
        if k + 1 < n_items:
            issue(k + 1, (k + 1) % 2)
        absorb(k, k % 2)


def _score_into(s_ref, mx_ref, slot, row0, q, kc):
    s = _dot_nt(q, kc)
    rows = q.shape[0]
    s_ref[slot, row0:row0 + rows] = s
    mx_ref[slot, row0:row0 + rows] = jnp.max(s, axis=1, keepdims=True)


def _mla_attn_kernel(q_ref, k_ref, v_ref, *rest, n_sub, n_chunks, bkv):
    o_ref, s_ref, mx_ref, m_ref, l_ref, acc_ref = rest[-6:]
    m_rows = s_ref.shape[1]

    def score(t, c, slot):
        _score_into(s_ref, mx_ref, slot, 0, q_ref[_rows(t * m_rows, m_rows), :], k_ref[_rows(c * bkv, bkv), :])

    _flash(score, v_ref, n_sub, n_chunks, bkv, s_ref, mx_ref, m_ref, l_ref, acc_ref)
    for t in range(n_sub):
        o_ref[t * m_rows:(t + 1) * m_rows, :] = (acc_ref[t] / l_ref[t]).astype(BF16)


def _gqa_attn_kernel(q_ref, k_ref, v_ref, *rest, n_sub, n_chunks, bkv):
    o_ref, qs_ref, s_ref, mx_ref, m_ref, l_ref, acc_ref = rest[-7:]
    g = GQA_HEADS // GQA_KV_HEADS
    bqs = s_ref.shape[1] // g
    for t in range(n_sub):
        for j in range(g):
            qs_ref[t, j * bqs:(j + 1) * bqs] = q_ref[t * bqs:(t + 1) * bqs, j * LANE:(j + 1) * LANE]

    def score(t, c, slot):
        _score_into(s_ref, mx_ref, slot, 0, qs_ref[t], k_ref[_rows(c * bkv, bkv), :])

    _flash(score, v_ref, n_sub, n_chunks, bkv, s_ref, mx_ref, m_ref, l_ref, acc_ref)
    for t in range(n_sub):
        o = acc_ref[t] / l_ref[t]
        for j in range(g):
            o_ref[t * bqs:(t + 1) * bqs, j * LANE:(j + 1) * LANE] = o[j * bqs:(j + 1) * bqs].astype(BF16)


def _diff_attn_kernel(q_ref, k_ref, v_ref, lam_ref, dn_ref, *rest, n_sub, n_chunks, bkv, lam_init):
    o_ref, s_ref, mx_ref, m_ref, l_ref, acc_ref = rest[-6:]
    d = DIFF_HEAD_DIM
    bqs = s_ref.shape[1] // 2
    lp = lam_ref[...]
    lam = (jnp.exp(jnp.sum(lp[0:1] * lp[1:2], axis=1, keepdims=True))
           - jnp.exp(jnp.sum(lp[2:3] * lp[3:4], axis=1, keepdims=True)) + lam_init)

    def score(t, c, slot):
        qr, kr = _rows(t * bqs, bqs), _rows(c * bkv, bkv)
        _score_into(s_ref, mx_ref, slot, 0, q_ref[qr, 0:d], k_ref[kr, 0:d])
        _score_into(s_ref, mx_ref, slot, bqs, q_ref[qr, d:2 * d], k_ref[kr, d:2 * d])

    _flash(score, v_ref, n_sub, n_chunks, bkv, s_ref, mx_ref, m_ref, l_ref, acc_ref, block_cols=True)
    for t in range(n_sub):
        o = acc_ref[t] / l_ref[t]
        o = o[:bqs] - lam * o[bqs:]
        o_ref[t * bqs:(t + 1) * bqs, :] = (_rms(o, dn_ref[...]) * (1.0 - lam_init)).astype(BF16)


def _attention(kind, q, k, v, s_lat, out_rows, extra=(), ctx=False, into=None):
    s_all = q.shape[0]
    n_ctx = s_all - s_lat
    assert s_lat % n_ctx == 0
    if kind == "mla":
        heads, qw, kw, vw, ow, stack = MLA_HEADS, MLA_QK_PAD, MLA_QK_PAD, MLA_V, MLA_V, 1
        bqs = n_ctx if ctx else _tile(s_lat, (1024, 512, 256))
        body = _mla_attn_kernel
    elif kind == "gqa":
        g = GQA_HEADS // GQA_KV_HEADS
        heads, qw, kw, vw, ow, stack = GQA_KV_HEADS, g * LANE, LANE, LANE, g * LANE, g
        bqs = n_ctx if ctx else 256
        body = _gqa_attn_kernel
    else:
        heads, qw, kw, vw, ow, stack = DIFF_HEADS, 2 * LANE, 2 * LANE, DIFF_V_DIM, DIFF_V_DIM, 2
        bqs = n_ctx if ctx else _tile(s_lat, (512, 256))
        body = functools.partial(_diff_attn_kernel, lam_init=extra[2])
        extra = extra[:2]
    m_rows = stack * bqs
    n_sub = 1 if ctx else _tile(s_lat // bqs, (4, 3, 2, 1))
    bq = n_sub * bqs
    if ctx:
        kv_rows, bkv, row_blk, nq = n_ctx, n_ctx, s_lat // n_ctx, 1
    else:
        kv_rows, bkv, row_blk, nq = s_all, _tile(s_all, (1280, 1024, 512, 256)), 0, s_lat // bq
    body = functools.partial(body, n_sub=n_sub, n_chunks=kv_rows // bkv, bkv=bkv)
    kv_blk = row_blk
    in_specs = [pl.BlockSpec((bq, qw), lambda h, i: (row_blk + i, h)),
                pl.BlockSpec((kv_rows, kw), lambda h, i: (kv_blk, h), pipeline_mode=pl.Buffered(1)),
                pl.BlockSpec((kv_rows, vw), lambda h, i: (kv_blk, h), pipeline_mode=pl.Buffered(1))]
    in_specs += [pl.BlockSpec(e.shape, lambda h, i: (0, 0)) for e in extra]
    args = [q, k, v, *extra]
    aliases = {}
    if into is not None:
        in_specs.append(pl.BlockSpec(memory_space=pl.ANY))
        aliases = {len(args): 0}
        args.append(into)
    scratch = []
    if kind == "gqa":
        scratch.append(pltpu.VMEM((n_sub, m_rows, LANE), BF16))
    scratch += [pltpu.VMEM((2, m_rows, bkv), F32),
                pltpu.VMEM((2, m_rows, 1), F32), pltpu.VMEM((n_sub, m_rows, 1), F32),
                pltpu.VMEM((n_sub, m_rows, 1), F32), pltpu.VMEM((n_sub, m_rows, vw), F32)]
    return pl.pallas_call(
        body,
        grid=(heads, nq),
        in_specs=in_specs,
        out_specs=pl.BlockSpec((bq, ow), lambda h, i: (row_blk + i, h)),
        out_shape=jax.ShapeDtypeStruct((out_rows, heads * ow), BF16),
        scratch_shapes=scratch,
        input_output_aliases=aliases,
        compiler_params=_params(("parallel", "arbitrary")),
        name=kind + ("_attn_ctx" if ctx else "_attn"),
    )(*args)


def _layer_norm(z, g, b):
    mu = jnp.mean(z, axis=-1, keepdims=True)
    zc = z - mu
    var = jnp.mean(zc * zc, axis=-1, keepdims=True)
    return zc * lax.rsqrt(var + LN_EPS) * g + b


def _merge_kernel(om_ref, og_ref, od_ref, g_ref, x_ref, mod_ref, wbm_ref, wbg_ref, wbd_ref, wo_ref,
                  lng_ref, lnb_ref, xo_ref, h2_ref, *, tm, s_lat):
    d = D_MODEL
    row0 = pl.program_id(0) * tm
    merged = g_ref[:, 0:d].astype(F32) * jnp.dot(om_ref[...], wbm_ref[...], preferred_element_type=F32)
    merged += g_ref[:, d:2 * d].astype(F32) * jnp.dot(og_ref[...], wbg_ref[...], preferred_element_type=F32)
    merged += g_ref[:, 2 * d:3 * d].astype(F32) * jnp.dot(od_ref[...], wbd_ref[...], preferred_element_type=F32)
    y = jnp.dot(merged.astype(BF16), wo_ref[...], preferred_element_type=F32)
    z = DEEPNORM_ALPHA * x_ref[...] + _row_mod(mod_ref, 2, row0, tm, s_lat) * y
    xn = _layer_norm(z, lng_ref[...], lnb_ref[...])
    xo_ref[...] = xn
    h2 = xn * (1.0 + _row_mod(mod_ref, 4, row0, tm, s_lat)) + _row_mod(mod_ref, 3, row0, tm, s_lat)
    h2_ref[...] = _pack_pairs(h2)


def _merge(om, og, od, gates, xs, mod, wbm, wbg, wbd, wo, lng, lnb, rows, s_lat):
    d = D_MODEL
    tm = 256
    row = lambda i: (i, 0)
    full = lambda i: (0, 0)
    resident = lambda a: pl.BlockSpec(a.shape, full, pipeline_mode=pl.Buffered(1))
    return pl.pallas_call(
        functools.partial(_merge_kernel, tm=tm, s_lat=s_lat),
        grid=(rows // tm,),
        in_specs=[pl.BlockSpec((tm, om.shape[1]), row), pl.BlockSpec((tm, og.shape[1]), row),
                  pl.BlockSpec((tm, od.shape[1]), row), pl.BlockSpec((tm, gates.shape[1]), row),
                  pl.BlockSpec((tm, d), row), pl.BlockSpec(mod.shape, full),
                  resident(wbm), resident(wbg), resident(wbd), resident(wo),
                  pl.BlockSpec(lng.shape, full), pl.BlockSpec(lnb.shape, full)],
        out_specs=[pl.BlockSpec((tm, d), row), pl.BlockSpec((tm, d // 2), row)],
        out_shape=[jax.ShapeDtypeStruct((rows, d), F32), jax.ShapeDtypeStruct((rows, d // 2), jnp.uint32)],
        compiler_params=_params(("parallel",)),
        name="merge",
    )(om, og, od, gates, xs, mod, wbm, wbg, wbd, wo, lng, lnb)


def _router_kernel(x_ref, mod_ref, wr_ref, br_ref, ri_ref, rw_ref, cnt_ref, run_ref, *, tm, s_lat):
    @pl.when(pl.program_id(0) == 0)
    def _():
        run_ref[...] = jnp.zeros(run_ref.shape, F32)

    row0 = pl.program_id(0) * tm
    h2 = x_ref[...] * (1.0 + _row_mod(mod_ref, 4, row0, tm, s_lat)) + _row_mod(mod_ref, 3, row0, tm, s_lat)
    logits = lax.dot_general(wr_ref[...], h2, (((1,), (1,)), ((), ())),
                             precision=HIGHEST, preferred_element_type=F32)
    scores = jax.nn.sigmoid(logits)
    biased = scores + br_ref[...]
    sc = [scores[e:e + 1, :] for e in range(N_EXPERTS)]
    bi = [biased[e:e + 1, :] for e in range(N_EXPERTS)]
    gs = []
    for g in range(N_GROUPS):
        a, b, c, dd = bi[4 * g:4 * g + 4]
        hi1, lo1, hi2, lo2 = jnp.maximum(a, b), jnp.minimum(a, b), jnp.maximum(c, dd), jnp.minimum(c, dd)
        gs.append(jnp.maximum(hi1, hi2) + jnp.maximum(jnp.minimum(hi1, hi2), jnp.maximum(lo1, lo2)))
    best = jnp.maximum(jnp.maximum(gs[0], gs[1]), jnp.maximum(gs[2], gs[3]))
    gsel = jnp.where(gs[0] == best, 0, jnp.where(gs[1] == best, 1, jnp.where(gs[2] == best, 2, 3)))
    pick = lambda vals, j: jnp.where(gsel == 0, vals[j], jnp.where(gsel == 1, vals[4 + j],
                                     jnp.where(gsel == 2, vals[8 + j], vals[12 + j])))
    xb = [pick(bi, j) for j in range(EXPERTS_PER_GROUP)]
    xs = [pick(sc, j) for j in range(EXPERTS_PER_GROUP)]
    sel, w = [], []
    for i in range(EXPERTS_PER_GROUP):
        rank = jnp.zeros_like(gsel)
        for j in range(EXPERTS_PER_GROUP):
            if j == i:
                continue
            beats = (xb[j] >= xb[i]) if j < i else (xb[j] > xb[i])
            rank = rank + beats.astype(jnp.int32)
        sel.append(rank < 2)
        w.append(jnp.where(rank < 2, xs[i], 0.0))
    inv = ROUTED_SCALE / (w[0] + w[1] + w[2] + w[3])
    ia = jnp.where(sel[0], 0, jnp.where(sel[1], 1, jnp.where(sel[2], 2, 3)))
    ib = jnp.where(sel[3], 3, jnp.where(sel[2], 2, jnp.where(sel[1], 1, 0)))
    local = lambda idx: jnp.where(idx == 0, w[0], jnp.where(idx == 1, w[1], jnp.where(idx == 2, w[2], w[3])))
    ea = gsel * EXPERTS_PER_GROUP + ia
    eb = gsel * EXPERTS_PER_GROUP + ib
    eidx = lax.broadcasted_iota(jnp.int32, (N_EXPERTS, tm), 0)
    is_a, is_b = eidx == ea, eidx == eb
    onehot = jnp.where(is_a | is_b, 1.0, 0.0)
    earlier = jnp.where(lax.broadcasted_iota(jnp.int32, (tm, tm), 0) < lax.broadcasted_iota(jnp.int32, (tm, tm), 1),
                        1.0, 0.0).astype(BF16)
    pos = run_ref[...] + jnp.dot(onehot.astype(BF16), earlier, preferred_element_type=F32)
    ri_ref[0:1, :] = ea
    ri_ref[1:2, :] = eb
    ri_ref[2:3, :] = jnp.sum(jnp.where(is_a, pos, 0.0), axis=0, keepdims=True).astype(jnp.int32)
    ri_ref[3:4, :] = jnp.sum(jnp.where(is_b, pos, 0.0), axis=0, keepdims=True).astype(jnp.int32)
    rw_ref[0:1, :] = local(ia) * inv
    rw_ref[1:2, :] = local(ib) * inv
    run = run_ref[...] + jnp.sum(onehot, axis=1, keepdims=True)
    run_ref[...] = run
    cnt_ref[...] = jnp.broadcast_to(run, cnt_ref.shape)


def _router(xs, mod, wr_t, br, rows, s_lat):
    d = D_MODEL
    tm = _tile(rows, (640, 512, 256))
    return pl.pallas_call(
        functools.partial(_router_kernel, tm=tm, s_lat=s_lat),
        grid=(rows // tm,),
        in_specs=[pl.BlockSpec((tm, d), lambda i: (i, 0)), pl.BlockSpec(mod.shape, lambda i: (0, 0)),
                  pl.BlockSpec(wr_t.shape, lambda i: (0, 0)), pl.BlockSpec(br.shape, lambda i: (0, 0))],
        out_specs=[pl.BlockSpec((4, tm), lambda i: (0, i)), pl.BlockSpec((2, tm), lambda i: (0, i)),
                   pl.BlockSpec((N_EXPERTS, LANE), lambda i: (0, 0))],
        out_shape=[jax.ShapeDtypeStruct((4, rows), jnp.int32), jax.ShapeDtypeStruct((2, rows), F32),
                   jax.ShapeDtypeStruct((N_EXPERTS, LANE), F32)],
        scratch_shapes=[pltpu.VMEM((N_EXPERTS, 1), F32)],
        compiler_params=_params(("arbitrary",)),
        name="router",
    )(xs, mod, wr_t, br)


MOE_TILE = 512
HALF = D_MODEL // 2
DMA_UNROLL = 8


def _pack_pairs(h):
    hi = pltpu.bitcast(h[:, :HALF].astype(BF16).astype(F32), jnp.uint32)
    lo = pltpu.bitcast(h[:, HALF:].astype(BF16).astype(F32), jnp.uint32)
    return (hi & jnp.uint32(0xFFFF0000)) | (lo >> jnp.uint32(16))


def _unpack_pairs(w):
    hi = pltpu.bitcast(w & jnp.uint32(0xFFFF0000), F32)
    lo = pltpu.bitcast(w << jnp.uint32(16), F32)
    return jnp.concatenate([hi, lo], axis=1).astype(BF16)


def _swiglu(x, wg, wu, wd):
    a = jnp.dot(x, wg, preferred_element_type=F32)
    b = jnp.dot(x, wu, preferred_element_type=F32)
    return jnp.dot((a * jax.nn.sigmoid(a) * b).astype(BF16), wd, preferred_element_type=F32)


def _moe_sort_kernel(dest_ref, h_ref, init_ref, o_ref, sem, *, tg):
    del init_ref
    base = pl.program_id(0) * tg

    def copy(r, j):
        return pltpu.make_async_copy(h_ref.at[pl.ds(r, 1)], o_ref.at[pl.ds(dest_ref[j, base + r], 1)], sem)

    def start(r, carry):
        for j in range(2):
            copy(r, j).start()
        return carry

    def wait(r, carry):
        for j in range(2):
            copy(r, j).wait()
        return carry

    lax.fori_loop(0, tg, start, 0, unroll=DMA_UNROLL)
    lax.fori_loop(0, tg, wait, 0, unroll=DMA_UNROLL)


def _moe_sort(dest, h2p, n_sorted):
    rows, half = h2p.shape
    tg = _tile(rows, (1280, 1024, 512, 256))
    return pl.pallas_call(
        functools.partial(_moe_sort_kernel, tg=tg),
        grid_spec=pltpu.PrefetchScalarGridSpec(
            num_scalar_prefetch=1, grid=(rows // tg,),
            in_specs=[pl.BlockSpec((tg, half), lambda i, dref: (i, 0)), pl.BlockSpec(memory_space=pl.ANY)],
            out_specs=pl.BlockSpec(memory_space=pl.ANY),
            scratch_shapes=[pltpu.SemaphoreType.DMA]),
        out_shape=jax.ShapeDtypeStruct((n_sorted, half), jnp.uint32),
        input_output_aliases={2: 0},
        compiler_params=_params(("arbitrary",)),
        name="moe_sort",
    )(dest, h2p, jnp.zeros((n_sorted, half), jnp.uint32))


def _moe_expert_kernel(te_ref, na_ref, x_ref, wg_ref, wu_ref, wd_ref, y_ref):
    del te_ref
    live = pl.program_id(0) < na_ref[0]

    @pl.when(live)
    def _():
        y_ref[...] = _swiglu(_unpack_pairs(x_ref[...]), wg_ref[0, 0].astype(BF16), wu_ref[0, 0].astype(BF16),
                             wd_ref[0, 0].astype(BF16))

    @pl.when(jnp.logical_not(live))
    def _():
        y_ref[...] = jnp.zeros(y_ref.shape, F32)


def _moe_experts(tile_expert, n_active, xsorted, wg, wu, wd, layer):
    n_sorted, half = xsorted.shape
    _, _, d, f = wg.shape
    tme = MOE_TILE
    wsel = lambda i, te, na: (layer, te[i], 0, 0)
    return pl.pallas_call(
        _moe_expert_kernel,
        grid_spec=pltpu.PrefetchScalarGridSpec(
            num_scalar_prefetch=2, grid=(n_sorted // tme,),
            in_specs=[pl.BlockSpec((tme, half), lambda i, te, na: (i, 0)),
                      pl.BlockSpec((1, 1, d, f), wsel), pl.BlockSpec((1, 1, d, f), wsel),
                      pl.BlockSpec((1, 1, f, d), wsel)],
            out_specs=pl.BlockSpec((tme, d), lambda i, te, na: (i, 0))),
        out_shape=jax.ShapeDtypeStruct((n_sorted, d), F32),
        compiler_params=_params(("arbitrary",)),
        name="moe_experts",
    )(tile_expert, n_active, xsorted, wg, wu, wd)


def _moe_final_kernel(dest_ref, hp_ref, rw_ref, y_ref, sg_ref, su_ref, sd_ref, x_ref, mod_ref, lng_ref, lnb_ref,
                      o_ref, ya_ref, yb_ref, sem, *, tm, s_lat):
    base = pl.program_id(0) * tm
    bufs = (ya_ref, yb_ref)

    def copy(r, j):
        return pltpu.make_async_copy(y_ref.at[pl.ds(dest_ref[j, base + r], 1)], bufs[j].at[pl.ds(r, 1)], sem.at[j])

    def start(r, carry):
        for j in range(2):
            copy(r, j).start()
        return carry

    def wait(r, carry):
        for j in range(2):
            copy(r, j).wait()
        return carry

    lax.fori_loop(0, tm, start, 0, unroll=DMA_UNROLL)
    shared = _swiglu(_unpack_pairs(hp_ref[...]), sg_ref[...], su_ref[...], sd_ref[...])
    lax.fori_loop(0, tm, wait, 0, unroll=DMA_UNROLL)
    y = shared + rw_ref[:, 0:1] * ya_ref[...] + rw_ref[:, 1:2] * yb_ref[...]
    z = DEEPNORM_ALPHA * x_ref[...] + _row_mod(mod_ref, 5, base, tm, s_lat) * y
    o_ref[...] = _layer_norm(z, lng_ref[...], lnb_ref[...])


def _moe_final(dest, h2p, rw, ysorted, sg, su, sd, xs, mod, lng, lnb, rows, s_lat):
    d = D_MODEL
    tm = 256
    row = lambda i, dref: (i, 0)
    full = lambda i, dref: (0, 0)
    return pl.pallas_call(
        functools.partial(_moe_final_kernel, tm=tm, s_lat=s_lat),
        grid_spec=pltpu.PrefetchScalarGridSpec(
            num_scalar_prefetch=1, grid=(rows // tm,),
            in_specs=[pl.BlockSpec((tm, HALF), row), pl.BlockSpec((tm, 2), row), pl.BlockSpec(memory_space=pl.ANY),
                      pl.BlockSpec(sg.shape, full), pl.BlockSpec(su.shape, full), pl.BlockSpec(sd.shape, full),
                      pl.BlockSpec((tm, d), row), pl.BlockSpec(mod.shape, full),
                      pl.BlockSpec(lng.shape, full), pl.BlockSpec(lnb.shape, full)],
            out_specs=pl.BlockSpec((tm, d), row),
            scratch_shapes=[pltpu.VMEM((tm, d), F32), pltpu.VMEM((tm, d), F32), pltpu.SemaphoreType.DMA((2,))]),
        out_shape=jax.ShapeDtypeStruct((rows, d), F32),
        compiler_params=_params(("arbitrary",)),
        name="moe_final",
    )(dest, h2p, rw, ysorted, sg, su, sd, xs, mod, lng, lnb)


def _moe_plan(ri, cnt, rows):
    tme = MOE_TILE
    n_sorted = 2 * rows + N_EXPERTS * tme
    counts = cnt[:, 0].astype(jnp.int32)
    padded = (counts + tme - 1) // tme * tme
    ends = jnp.cumsum(padded)
    offs = ends - padded
    eidx = jnp.arange(N_EXPERTS, dtype=jnp.int32)[:, None]
    seg = lambda e: jnp.sum(jnp.where(eidx == e[None, :], offs[:, None], 0), axis=0)
    dest = jnp.stack([seg(ri[0]) + ri[2], seg(ri[1]) + ri[3]])
    tile_start = jnp.arange(n_sorted // tme, dtype=jnp.int32) * tme
    tile_expert = jnp.minimum(jnp.sum((ends[None, :] <= tile_start[:, None]).astype(jnp.int32), axis=1),
                              N_EXPERTS - 1)
    n_active = (ends[-1:] // tme).astype(jnp.int32)
    return dest, tile_expert, n_active, n_sorted


def _moe(xs, h2p, mod, wr_t, br, wg, wu, wd, layer, sg, su, sd, lng, lnb, rows, s_lat):
    ri, rw, cnt = _router(xs, mod, wr_t, br, rows, s_lat)
    dest, tile_expert, n_active, n_sorted = _moe_plan(ri, cnt, rows)
    xsorted = _moe_sort(dest, h2p, n_sorted)
    ysorted = _moe_experts(tile_expert, n_active, xsorted, wg, wu, wd, layer)
    return _moe_final(dest, h2p, rw.T, ysorted, sg, su, sd, xs, mod, lng, lnb, rows, s_lat)


def _rope_tables(s_lat, n_ctx, dim):
    quarter = dim // 4
    rows = s_lat // GRID_W
    lane = jnp.arange(LANE)
    part = lane // quarter
    inv_freq = ROPE_THETA ** (-(lane % quarter).astype(F32) / quarter)
    ang_r = jnp.arange(rows, dtype=F32)[:, None] * inv_freq
    ang_c = jnp.arange(GRID_W, dtype=F32)[:, None] * inv_freq
    spread = lambda by_row, by_col: (by_row[:, None, :] + by_col[None, :, :]).reshape(s_lat, LANE)
    on = lambda lanes, t: jnp.where(lanes, t, 0.0)
    cos = spread(on(part < 2, jnp.cos(ang_r)), on((part == 2) | (part == 3), jnp.cos(ang_c)) + on(part > 3, 1.0))
    sa = spread(on(part == 0, -jnp.sin(ang_r)), on(part == 2, -jnp.sin(ang_c)))
    sb = spread(on(part == 1, jnp.sin(ang_r)), on(part == 3, jnp.sin(ang_c)))
    ctx1 = jnp.ones((n_ctx, LANE), F32)
    ctx0 = jnp.zeros((n_ctx, LANE), F32)
    return (jnp.concatenate([cos, ctx1], axis=0), jnp.concatenate([sa, ctx0], axis=0),
            jnp.concatenate([sb, ctx0], axis=0))


def kernel(x, c, ctx, c_ctx, w_ada, b_ada, w_in, mla_q_norm, mla_w_uq, mla_kv_norm, mla_w_ukv, gqa_q_norm,
           gqa_k_norm, diff_lambda, diff_norm, w_br_mla, w_br_gqa, w_br_diff, w_o, ln1_g, ln1_b, w_router,
           b_router, moe_w_gate, moe_w_up, moe_w_down, shared_w_gate, shared_w_up, shared_w_down, ln2_g, ln2_b):
    B, s_lat, d = x.shape
    n_ctx = ctx.shape[1]
    assert B == 1 and d == D_MODEL
    s_all = s_lat + n_ctx

    xs = jnp.concatenate([x[0], ctx[0]], axis=0)
    cc = jnp.zeros((8, d), F32).at[0].set(c[0]).at[1].set(c_ctx)
    mod_all = _adaln(cc, w_ada, b_ada)

    tabs64 = _rope_tables(s_lat, n_ctx, MLA_ROPE)
    tabs128 = _rope_tables(s_lat, n_ctx, GQA_HEAD_DIM)
    wr_t = w_router.T
    br = b_router.reshape(N_EXPERTS, 1)
    row2 = lambda v: v.reshape(1, -1)

    o0 = MLA_Q_LORA + MLA_KV_LORA + MLA_ROPE
    o1 = o0 + GQA_HEADS * GQA_HEAD_DIM + 2 * GQA_KV_HEADS * GQA_HEAD_DIM
    o2 = o1 + 2 * (2 * DIFF_HEADS * DIFF_HEAD_DIM) + DIFF_HEADS * DIFF_V_DIM

    for l in range(DEPTH):
        last = l == DEPTH - 1
        mod = mod_all[l]
        w = w_in[l]
        wa = jnp.pad(w[:, :o0], ((0, 0), (0, LANE - MLA_ROPE))).astype(BF16)
        wb = w[:, o0:o1].astype(BF16)
        wc = w[:, o1:o2].astype(BF16)
        wd_gate = w[:, o2:].astype(BF16)
        wuq = jnp.pad(mla_w_uq[l].reshape(MLA_Q_LORA, MLA_HEADS, MLA_NOPE + MLA_ROPE),
                      ((0, 0), (0, 0), (0, MLA_QK_PAD - MLA_NOPE - MLA_ROPE))
                      ).reshape(MLA_Q_LORA, MLA_HEADS * MLA_QK_PAD).astype(BF16)
        ukv = mla_w_ukv[l].reshape(MLA_KV_LORA, MLA_HEADS, MLA_NOPE + MLA_V)
        wukv = jnp.concatenate([ukv[:, :, :MLA_NOPE].reshape(MLA_KV_LORA, -1),
                                ukv[:, :, MLA_NOPE:].reshape(MLA_KV_LORA, -1)], axis=1).astype(BF16)

        h, q_mla, k_mla, v_mla = _mla_proj(xs, mod, wa, row2(mla_q_norm[l]), row2(mla_kv_norm[l]),
                                           wuq, wukv, tabs64, s_lat)
        q_gqa, k_gqa, v_gqa = _gqa_proj(h, wb, row2(gqa_q_norm[l]), row2(gqa_k_norm[l]), tabs128)
        q_dif, k_dif, v_dif = _diff_proj(h, wc, tabs128)
        gates = _gate_proj(h, wd_gate)

        lam_init = 0.8 - 0.6 * math.exp(-0.3 * l)
        rows = s_lat if last else s_all
        dif_extra = (diff_lambda[l], row2(diff_norm[l]), lam_init)
        fresh = lambda width: None if last else jnp.zeros((rows, width), BF16)
        o_mla = _attention("mla", q_mla, k_mla, v_mla, s_lat, rows, into=fresh(MLA_HEADS * MLA_V))
        o_gqa = _attention("gqa", q_gqa, k_gqa, v_gqa, s_lat, rows, into=fresh(GQA_HEADS * GQA_HEAD_DIM))
        o_dif = _attention("diff", q_dif, k_dif, v_dif, s_lat, rows, extra=dif_extra,
                           into=fresh(DIFF_HEADS * DIFF_V_DIM))
        if not last:
            o_mla = _attention("mla", q_mla, k_mla, v_mla, s_lat, rows, ctx=True, into=o_mla)
            o_gqa = _attention("gqa", q_gqa, k_gqa, v_gqa, s_lat, rows, ctx=True, into=o_gqa)
            o_dif = _attention("diff", q_dif, k_dif, v_dif, s_lat, rows, extra=dif_extra, ctx=True, into=o_dif)

        xs, h2p = _merge(o_mla, o_gqa, o_dif, gates, xs, mod, w_br_mla[l].astype(BF16), w_br_gqa[l].astype(BF16),
                         w_br_diff[l].astype(BF16), w_o[l].astype(BF16), row2(ln1_g[l]), row2(ln1_b[l]), rows, s_lat)

        xs = _moe(xs, h2p, mod, wr_t, br, moe_w_gate, moe_w_up, moe_w_down, l,
                  shared_w_gate[l].astype(BF16), shared_w_up[l].astype(BF16), shared_w_down[l].astype(BF16),
                  row2(ln2_g[l]), row2(ln2_b[l]), rows, s_lat)

    return xs[None]
```
